```python
import math
import jax, jax.numpy as jnp
from jax import lax
import numpy as np

D_MODEL = 1024
BATCH = 4
SEQ = 8192
DEPTH = 4

CHUNK = 64
N_MIXERS = 2
N_POOL = (DEPTH + 1) // 2
N_FOX = DEPTH // 2
POOL_WINDOWS = (2, 4, 8, 16)
N_POOL_GROUPS = len(POOL_WINDOWS)
POOL_GROUP = D_MODEL // N_POOL_GROUPS
HEAD_DIM = 64
N_HEADS = D_MODEL // HEAD_DIM
Q_BLOCK = 128
NEG_INF = -1e30
N_EXPERTS = 32
TOP_K = 4
D_EXPERT = D_MODEL
SWIGLU_LIMIT = 7.0
SWIGLU_ALPHA = 1.702
EXPERT_BLOCK = 128
EPS = 1e-6

kernel_name = 'hybrid_pool_fox_moe_adaln_encoder'


def rmsnorm(x, g):
    x32 = x.astype(jnp.float32)
    y = x32 * lax.rsqrt(jnp.mean(x32 * x32, axis=-1, keepdims=True) + EPS)
    return (y * g.astype(jnp.float32)).astype(x.dtype)


def modulate(h, shift, scale):
    return h * (1 + scale[:, None, :]) + shift[:, None, :]


def pool_mixer(h, w_grp, scale):
    B, S, D = h.shape
    h32 = h.astype(jnp.float32)
    cs = jnp.concatenate([jnp.zeros((B, 1, D), jnp.float32), jnp.cumsum(h32, axis=1)], axis=1)
    t = jnp.arange(S)
    parts = []
    for g, w in enumerate(POOL_WINDOWS):
        sl = slice(g * POOL_GROUP, (g + 1) * POOL_GROUP)
        lo = jnp.maximum(t + 1 - w, 0)
        win_sum = cs[:, 1:, sl] - cs[:, lo, sl]
        cnt = jnp.minimum(t + 1, w).astype(jnp.float32)
        parts.append(win_sum / cnt[None, :, None] - h32[..., sl])
    d = jnp.stack(parts, axis=2).astype(h.dtype)
    y = jnp.einsum('bsgc,gcd->bsgd', d, w_grp).reshape(B, S, D)
    return y * scale


def forgetting_attention(h, w_in, b_f, w_o):
    B, S, D = h.shape
    proj = h @ w_in
    q = proj[..., :D].reshape(B, S, N_HEADS, HEAD_DIM)
    k = proj[..., D:2 * D].reshape(B, S, N_HEADS, HEAD_DIM)
    v = proj[..., 2 * D:3 * D].reshape(B, S, N_HEADS, HEAD_DIM)
    logf = jax.nn.log_sigmoid((proj[..., 3 * D:] + b_f).astype(jnp.float32))
    Ft = jnp.cumsum(logf, axis=1).transpose(0, 2, 1)
    kpos = jnp.arange(S)
    sm_scale = HEAD_DIM ** -0.5

    def attend(qi):
        s0 = qi * Q_BLOCK
        qb = lax.dynamic_slice_in_dim(q, s0, Q_BLOCK, axis=1)
        Fq = lax.dynamic_slice_in_dim(Ft, s0, Q_BLOCK, axis=2)
        logits = jnp.einsum('bqhd,bkhd->bhqk', qb, k, preferred_element_type=jnp.float32) * sm_scale
        logits = logits + Fq[..., :, None] - Ft[:, :, None, :]
        qpos = s0 + jnp.arange(Q_BLOCK)
        mask = qpos[:, None] >= kpos[None, :]
        p = jax.nn.softmax(jnp.where(mask, logits, NEG_INF), axis=-1)
        return jnp.einsum('bhqk,bkhd->bqhd', p.astype(v.dtype), v)

    o = lax.map(attend, jnp.arange(S // Q_BLOCK))
    o = o.transpose(1, 0, 2, 3, 4).reshape(B, S, D)
    return o @ w_o


def moe_ffn(h, router_w, router_b, w_in, b_in, w_out, b_out):
    B, S, D = h.shape
    T = B * S
    TK = T * TOP_K
    ht = h.reshape(T, D)
    logits = (ht @ router_w + router_b).astype(jnp.float32)
    top_v, top_i = lax.top_k(logits, TOP_K)
    gates = jax.nn.softmax(top_v, axis=-1)
    e_flat = top_i.reshape(-1)
    tok_flat = jnp.repeat(jnp.arange(T, dtype=jnp.int32), TOP_K)
    g_flat = gates.reshape(-1)
    order = jnp.argsort(e_flat, stable=True)
    e_sorted = e_flat[order]
    counts = jnp.bincount(e_flat, length=N_EXPERTS)
    padded = ((counts + EXPERT_BLOCK - 1) // EXPERT_BLOCK) * EXPERT_BLOCK
    start = jnp.cumsum(counts) - counts
    pend = jnp.cumsum(padded)
    pstart = pend - padded
    dest = pstart[e_sorted] + jnp.arange(TK) - start[e_sorted]
    n_blocks = TK // EXPERT_BLOCK + N_EXPERTS
    P = n_blocks * EXPERT_BLOCK
    row_tok = jnp.zeros((P,), jnp.int32).at[dest].set(tok_flat[order])
    row_gate = jnp.zeros((P,), jnp.float32).at[dest].set(g_flat[order])
    block_e = jnp.minimum(jnp.searchsorted(pend, jnp.arange(n_blocks) * EXPERT_BLOCK, side='right'), N_EXPERTS - 1)
    xs = ht[row_tok].reshape(n_blocks, EXPERT_BLOCK, D)

    def expert_block(args):
        xb, e = args
        gu = xb @ w_in[e] + b_in[e]
        gate = jnp.minimum(gu[:, :D_EXPERT], SWIGLU_LIMIT)
        up = jnp.clip(gu[:, D_EXPERT:], -SWIGLU_LIMIT, SWIGLU_LIMIT)
        act = (up + 1) * (gate * jax.nn.sigmoid(SWIGLU_ALPHA * gate))
        return act @ w_out[e] + b_out[e]

    ys = lax.map(expert_block, (xs, block_e)).reshape(P, D)
    out = jnp.zeros((T, D), jnp.float32).at[row_tok].add(ys.astype(jnp.float32) * row_gate[:, None])
    return out.astype(h.dtype).reshape(B, S, D)


def setup_inputs(seed: int = 0) -> dict:
    key = jax.random.key(seed)
    ks = jax.random.split(key, 20)
    D, H, E, F = D_MODEL, N_HEADS, N_EXPERTS, D_EXPERT
    nrm = jax.random.normal
    return {
        'x': nrm(ks[0], (BATCH, SEQ, D), jnp.float32),
        'c': nrm(ks[1], (BATCH, D), jnp.float32),
        'norm_mix_g': 1.0 + 0.05 * nrm(ks[2], (DEPTH, D), jnp.float32),
        'norm_ffn_g': 1.0 + 0.05 * nrm(ks[3], (DEPTH, D), jnp.float32),
        'ada_w': 0.5 * D ** -0.5 * nrm(ks[4], (DEPTH, D, 6 * D), jnp.float32),
        'ada_b': 0.02 * nrm(ks[5], (DEPTH, 6 * D), jnp.float32),
        'pool_w': POOL_GROUP ** -0.5 * nrm(ks[6], (N_POOL, N_POOL_GROUPS, POOL_GROUP, POOL_GROUP), jnp.float32),
        'pool_scale': 1.0 + 0.1 * nrm(ks[7], (N_POOL, D), jnp.float32),
        'fox_w_in': D ** -0.5 * nrm(ks[8], (N_FOX, D, 3 * D + H), jnp.float32),
        'fox_b_f': 2.0 + 0.1 * nrm(ks[9], (N_FOX, H), jnp.float32),
        'fox_w_o': D ** -0.5 * nrm(ks[10], (N_FOX, D, D), jnp.float32),
        'router_w': D ** -0.5 * nrm(ks[11], (DEPTH, D, E), jnp.float32),
        'router_b': 0.01 * nrm(ks[12], (DEPTH, E), jnp.float32),
        'exp_w_in': D ** -0.5 * nrm(ks[13], (DEPTH, E, D, 2 * F), jnp.float32),
        'exp_b_in': 0.02 * nrm(ks[14], (DEPTH, E, 2 * F), jnp.float32),
        'exp_w_out': F ** -0.5 * nrm(ks[15], (DEPTH, E, F, D), jnp.float32),
        'exp_b_out': 0.02 * nrm(ks[16], (DEPTH, E, D), jnp.float32),
        'final_g': 1.0 + 0.05 * nrm(ks[17], (D,), jnp.float32),
    }


def reference(x, c, norm_mix_g, norm_ffn_g, ada_w, ada_b, pool_w, pool_scale, fox_w_in, fox_b_f, fox_w_o,
              router_w, router_b, exp_w_in, exp_b_in, exp_w_out, exp_b_out, final_g):
    cond = jax.nn.silu(c)
    for i in range(DEPTH):
        mod = cond @ ada_w[i] + ada_b[i]
        sh1, sc1, g1, sh2, sc2, g2 = jnp.split(mod, 6, axis=-1)
        h = modulate(rmsnorm(x, norm_mix_g[i]), sh1, sc1)
        j = i // N_MIXERS
        if i % N_MIXERS == 0:
            m = pool_mixer(h, pool_w[j], pool_scale[j])
        else:
            m = forgetting_attention(h, fox_w_in[j], fox_b_f[j], fox_w_o[j])
        x = x + g1[:, None, :] * m
        h = modulate(rmsnorm(x, norm_ffn_g[i]), sh2, sc2)
        x = x + g2[:, None, :] * moe_ffn(h, router_w[i], router_b[i], exp_w_in[i], exp_b_in[i], exp_w_out[i], exp_b_out[i])
    return rmsnorm(x, final_g)
```

```python
import functools

import jax
import jax.numpy as jnp
from jax import lax
from jax.experimental import pallas as pl
from jax.experimental.pallas import tpu as pltpu

F32 = jnp.float32
BF16 = jnp.bfloat16
U32 = jnp.uint32
I32 = jnp.int32

POOL_WINDOWS = (2, 4, 8, 16)
POOL_HALO = 16
HEAD_DIM = 64
HEADS_PER_LANE_TILE = 2
TOP_K = 4
SWIGLU_LIMIT = 7.0
SWIGLU_ALPHA = 1.702
EPS = 1e-6
NEG_BIG = -1e30
LOG2E = 1.4426950408889634

LANES = 128
SEQ_TILE = 512
ATTN_TILE = 512
ROUTE_TILE = 512
EXPERT_ROWS = 512
DISPATCH_TILE = 256
ADA_COLS = 1536
VMEM_LIMIT_BYTES = 56 * 1024 * 1024

NT_DIMS = (((1,), (1,)), ((), ()))


def _params(sem, vmem=VMEM_LIMIT_BYTES):
    return pltpu.CompilerParams(dimension_semantics=sem, vmem_limit_bytes=vmem)


def _sigmoid(z):
    return 1.0 / (1.0 + jnp.exp(-z))


def _norm_mod(x, g, scale, shift):
    r = lax.rsqrt(jnp.mean(x * x, axis=-1, keepdims=True) + EPS)
    return (x * r) * (g * (1.0 + scale)) + shift


def _pack_pairs(y):
    w = y.shape[1] // 2
    hi = lax.bitcast_convert_type(y[:, :w].astype(BF16).astype(F32), U32)
    lo = lax.bitcast_convert_type(y[:, w:].astype(BF16).astype(F32), U32)
    return hi | (lo >> 16)


def _unpack_pairs(w):
    hi = lax.bitcast_convert_type(w & jnp.uint32(0xFFFF0000), F32)
    lo = lax.bitcast_convert_type(w << 16, F32)
    return jnp.concatenate([hi, lo], axis=1)


def _split_bf16(x, parts):
    out = []
    r = x
    for _ in range(parts):
        p = r.astype(BF16)
        out.append(p)
        r = r - p.astype(F32)
    return out


def _ffn_prenorm(xn, gf, sc2, sh2, rwt, rb):
    h2 = _norm_mod(xn, gf, sc2, sh2)
    h_hi, h_lo = _split_bf16(h2, 2)
    w_hi, w_lo = _split_bf16(rwt, 2)
    lg = (lax.dot_general(w_hi, h_hi, NT_DIMS, preferred_element_type=F32)
          + lax.dot_general(w_lo, h_hi, NT_DIMS, preferred_element_type=F32)
          + lax.dot_general(w_hi, h_lo, NT_DIMS, preferred_element_type=F32)) + rb
    return _pack_pairs(h2), lg


def _mod_spec(layer, batch, j):
    def index(b, s):
        return ((layer * batch + b) * 6 + j, 0, 0)
    return index


def _ada_kernel(ct_ref, w_ref, b_ref, o_ref):
    ct = ct_ref[...]
    cond = ct * _sigmoid(ct)
    w = w_ref[0]
    rows = []
    for b in range(ct.shape[1]):
        rows.append(jnp.sum(w * cond[:, b:b + 1], axis=0, keepdims=True))
    o_ref[0] = jnp.concatenate(rows, axis=0) + b_ref[0]


def _ada_call(c, ada_w, ada_b):
    depth, d, n = ada_w.shape
    batch = c.shape[0]
    tn = ADA_COLS
    return pl.pallas_call(
        _ada_kernel,
        grid=(depth, n // tn),
        in_specs=[
            pl.BlockSpec((d, batch), lambda i, j: (0, 0)),
            pl.BlockSpec((1, d, tn), lambda i, j: (i, 0, j)),
            pl.BlockSpec((1, 1, tn), lambda i, j: (i, 0, j)),
        ],
        out_specs=pl.BlockSpec((1, batch, tn), lambda i, j: (i, 0, j)),
        out_shape=jax.ShapeDtypeStruct((depth, batch, n), F32),
        compiler_params=_params(("arbitrary", "arbitrary")),
        name="ada_mod",
    )(c.T, ada_w, ada_b.reshape(depth, 1, n))


def _pool_kernel(x_ref, sh1, sc1, g1, sh2, sc2, gm, gf, pw_ref, ps_ref, rwt_ref, rb_ref,
                 xo_ref, hp_ref, lg_ref, hbuf):
    s = pl.program_id(1)
    ts = x_ref.shape[1]
    d = x_ref.shape[2]
    pg = d // len(POOL_WINDOWS)
    x = x_ref[0]
    h = _norm_mod(x, gm[...], sc1[0], sh1[0])

    @pl.when(s == 0)
    def _():
        hbuf[0:POOL_HALO, :] = jnp.zeros((POOL_HALO, d), F32)

    @pl.when(s > 0)
    def _():
        hbuf[0:POOL_HALO, :] = hbuf[ts:ts + POOL_HALO, :]

    hbuf[POOL_HALO:POOL_HALO + ts, :] = h
    pos = s * ts + lax.broadcasted_iota(I32, (ts, 1), 0)
    ys = []
    for g, w in enumerate(POOL_WINDOWS):
        lo = g * pg
        hg = h[:, lo:lo + pg]
        acc = hg
        for k in range(1, w):
            acc = acc + hbuf[POOL_HALO - k:POOL_HALO - k + ts, lo:lo + pg]
        cnt = jnp.minimum(pos + 1, w).astype(F32)
        dlt = acc / cnt - hg
        ys.append(jnp.dot(dlt.astype(BF16), pw_ref[g].astype(BF16), preferred_element_type=F32))
    y = jnp.concatenate(ys, axis=1) * ps_ref[...]
    xn = x + g1[0] * y
    xo_ref[0] = xn
    hp, lg = _ffn_prenorm(xn, gf[...], sc2[0], sh2[0], rwt_ref[...], rb_ref[...])
    hp_ref[...] = hp
    lg_ref[...] = lg


def _row_layer_specs(layer, batch, ns, ts, d, e):
    row = lambda j: pl.BlockSpec((1, 1, d), _mod_spec(layer, batch, j))
    vec = pl.BlockSpec((1, d), lambda b, s: (0, 0))
    x_spec = pl.BlockSpec((1, ts, d), lambda b, s: (b, s, 0))
    out_specs = [
        x_spec,
        pl.BlockSpec((ts, d // 2), lambda b, s: (b * ns + s, 0)),
        pl.BlockSpec((e, ts), lambda b, s: (0, b * ns + s)),
    ]
    return row, vec, x_spec, out_specs


def _pool_call(layer, x, modr, gm, gf, pw, ps, rwt, rb):
    batch, seq, d = x.shape
    e = rwt.shape[0]
    ts = SEQ_TILE
    ns = seq // ts
    g = len(POOL_WINDOWS)
    pg = d // g
    row, vec, x_spec, out_specs = _row_layer_specs(layer, batch, ns, ts, d, e)
    return pl.pallas_call(
        _pool_kernel,
        grid=(batch, ns),
        in_specs=[
            x_spec, row(0), row(1), row(2), row(3), row(4), vec, vec,
            pl.BlockSpec((g, pg, pg), lambda b, s: (0, 0, 0)),
            vec,
            pl.BlockSpec((e, d), lambda b, s: (0, 0)),
            pl.BlockSpec((e, 1), lambda b, s: (0, 0)),
        ],
        out_specs=out_specs,
        out_shape=[
            jax.ShapeDtypeStruct((batch, seq, d), F32),
            jax.ShapeDtypeStruct((batch * seq, d // 2), U32),
            jax.ShapeDtypeStruct((e, batch * seq), F32),
        ],
        scratch_shapes=[pltpu.VMEM((ts + POOL_HALO, d), F32)],
        compiler_params=_params(("arbitrary", "arbitrary")),
        name="pool_layer",
    )(x, modr, modr, modr, modr, modr, gm, gf, pw, ps, rwt, rb)


def _qkv_kernel(x_ref, sh1, sc1, gm, w_ref, wft_ref, bf_ref, q_ref, k_ref, v_ref, f_ref, carry):
    s = pl.program_id(1)
    ts = x_ref.shape[1]
    d = x_ref.shape[2]
    nh = wft_ref.shape[0]
    h = _norm_mod(x_ref[0], gm[...], sc1[0], sh1[0])
    h_hi, h_lo = _split_bf16(h, 2)
    qkv = jnp.dot(h_hi, w_ref[...], preferred_element_type=F32)
    q_ref[0] = (qkv[:, :d] * (HEAD_DIM ** -0.5 * LOG2E)).astype(BF16)
    k_ref[0] = qkv[:, d:2 * d].astype(BF16)
    v_ref[0] = qkv[:, 2 * d:].astype(BF16)

    w_hi, w_lo = _split_bf16(wft_ref[...], 2)
    fl = (lax.dot_general(w_hi, h_hi, NT_DIMS, preferred_element_type=F32)
          + lax.dot_general(w_lo, h_hi, NT_DIMS, preferred_element_type=F32)
          + lax.dot_general(w_hi, h_lo, NT_DIMS, preferred_element_type=F32)) + bf_ref[...]
    logf = jnp.minimum(fl, 0.0) - jnp.log(1.0 + jnp.exp(-jnp.abs(fl)))

    @pl.when(s == 0)
    def _():
        carry[...] = jnp.zeros(carry.shape, F32)

    r = lax.broadcasted_iota(I32, (ts, ts), 0)
    c = lax.broadcasted_iota(I32, (ts, ts), 1)
    upper = jnp.where(r <= c, 1.0, 0.0).astype(BF16)
    parts = jnp.concatenate(_split_bf16(logf, 3), axis=0)
    cs = jnp.dot(parts, upper, preferred_element_type=F32)
    cum = cs[:nh] + cs[nh:2 * nh] + cs[2 * nh:] + carry[...]
    carry[...] = cum[:, ts - 1:ts]
    f_ref[0] = cum * LOG2E


def _qkv_call(layer, x, modr, gm, wqkv, wft, bf):
    batch, seq, d = x.shape
    nh = wft.shape[0]
    ts = SEQ_TILE
    ns = seq // ts
    row = lambda j: pl.BlockSpec((1, 1, d), _mod_spec(layer, batch, j))
    x_spec = pl.BlockSpec((1, ts, d), lambda b, s: (b, s, 0))
    return pl.pallas_call(
        _qkv_kernel,
        grid=(batch, ns),
        in_specs=[
            x_spec, row(0), row(1),
            pl.BlockSpec((1, d), lambda b, s: (0, 0)),
            pl.BlockSpec((d, 3 * d), lambda b, s: (0, 0)),
            pl.BlockSpec((nh, d), lambda b, s: (0, 0)),
            pl.BlockSpec((nh, 1), lambda b, s: (0, 0)),
        ],
        out_specs=[x_spec, x_spec, x_spec, pl.BlockSpec((1, nh, ts), lambda b, s: (b, 0, s))],
        out_shape=[jax.ShapeDtypeStruct((batch, seq, d), BF16)] * 3
        + [jax.ShapeDtypeStruct((batch, nh, seq), F32)],
        scratch_shapes=[pltpu.VMEM((nh, 1), F32)],
        compiler_params=_params(("arbitrary", "arbitrary")),
        name="fox_qkv",
    )(x, modr, modr, gm, wqkv, wft, bf)


def _attn_kernel(q_ref, k_ref, v_ref, f_ref, o_ref, m_s, l_s, acc_s):
    qi = pl.program_id(2)
    tq = q_ref.shape[1]
    tk = tq
    q2 = q_ref[0]
    lane = lax.broadcasted_iota(I32, (tq, LANES), 1)
    first = lane < HEAD_DIM
    zero = jnp.zeros_like(q2)
    qh = [jnp.where(first, q2, zero), jnp.where(first, zero, q2)]
    m_s[...] = jnp.full(m_s.shape, NEG_BIG, F32)
    l_s[...] = jnp.zeros(l_s.shape, F32)
    acc_s[...] = jnp.zeros(acc_s.shape, F32)
    row = lax.broadcasted_iota(I32, (tq, tk), 0)
    col = lax.broadcasted_iota(I32, (tq, tk), 1)

    def step(kb, diagonal):
        start = pl.multiple_of(kb * tk, tk)
        k2 = k_ref[0, pl.ds(start, tk), :]
        v2 = v_ref[0, pl.ds(start, tk), :]
        fk = f_ref[0, 0, :, pl.ds(start, tk)]
        for hh in range(HEADS_PER_LANE_TILE):
            z = lax.dot_general(qh[hh], k2, NT_DIMS, preferred_element_type=F32) - fk[hh:hh + 1, :]
            if diagonal:
                z = jnp.where(row >= col, z, NEG_BIG)
            m_prev = m_s[hh]
            m_new = jnp.maximum(m_prev, jnp.max(z, axis=1, keepdims=True))
            alpha = jnp.exp2(m_prev - m_new)
            p = jnp.exp2(z - jnp.concatenate([m_new] * (tk // LANES), axis=1))
            l_s[hh] = alpha * l_s[hh] + jnp.sum(p, axis=1, keepdims=True)
            acc_s[hh] = alpha * acc_s[hh] + jnp.dot(p.astype(BF16), v2, preferred_element_type=F32)
            m_s[hh] = m_new

    def body(kb, carry):
        step(kb, False)
        return carry

    lax.fori_loop(0, qi, body, 0)
    step(qi, True)
    o = jnp.where(first, acc_s[0] / l_s[0], acc_s[1] / l_s[1])
    o_ref[0] = o.astype(BF16)


def _attn_call(q, k, v, f2):
    batch, seq, d = q.shape
    nh = f2.shape[1]
    hp = nh // HEADS_PER_LANE_TILE
    tq = ATTN_TILE
    f4 = f2.reshape(batch, hp, HEADS_PER_LANE_TILE, seq)
    kv_spec = pl.BlockSpec((1, seq, LANES), lambda b, h, i: (b, 0, h))
    q_spec = pl.BlockSpec((1, tq, LANES), lambda b, h, i: (b, i, h))
    return pl.pallas_call(
        _attn_kernel,
        grid=(batch, hp, seq // tq),
        in_specs=[q_spec, kv_spec, kv_spec,
                  pl.BlockSpec((1, 1, HEADS_PER_LANE_TILE, seq), lambda b, h, i: (b, h, 0, 0))],
        out_specs=q_spec,
        out_shape=jax.ShapeDtypeStruct((batch, seq, d), BF16),
        scratch_shapes=[pltpu.VMEM((HEADS_PER_LANE_TILE, tq, LANES), F32)] * 3,
        compiler_params=_params(("arbitrary", "arbitrary", "arbitrary")),
        name="fox_attention",
    )(q, k, v, f4)


def _wo_kernel(x_ref, o_ref, g1, sh2, sc2, gf, wo_ref, rwt_ref, rb_ref, xo_ref, hp_ref, lg_ref):
    m = jnp.dot(o_ref[0], wo_ref[...], preferred_element_type=F32)
    xn = x_ref[0] + g1[0] * m
    xo_ref[0] = xn
    hp, lg = _ffn_prenorm(xn, gf[...], sc2[0], sh2[0], rwt_ref[...], rb_ref[...])
    hp_ref[...] = hp
    lg_ref[...] = lg


def _wo_call(layer, x, o, modr, gf, wo, rwt, rb):
    batch, seq, d = x.shape
    e = rwt.shape[0]
    ts = SEQ_TILE
    ns = seq // ts
    row, vec, x_spec, out_specs = _row_layer_specs(layer, batch, ns, ts, d, e)
    return pl.pallas_call(
        _wo_kernel,
        grid=(batch, ns),
        in_specs=[
            x_spec, x_spec, row(2), row(3), row(4), vec,
            pl.BlockSpec((d, d), lambda b, s: (0, 0)),
            pl.BlockSpec((e, d), lambda b, s: (0, 0)),
            pl.BlockSpec((e, 1), lambda b, s: (0, 0)),
        ],
        out_specs=out_specs,
        out_shape=[
            jax.ShapeDtypeStruct((batch, seq, d), F32),
            jax.ShapeDtypeStruct((batch * seq, d // 2), U32),
            jax.ShapeDtypeStruct((e, batch * seq), F32),
        ],
        compiler_params=_params(("arbitrary", "arbitrary")),
        name="fox_out",
    )(x, o, modr, modr, modr, gf, wo, rwt, rb)


def _route_kernel(lg_ref, eid_ref, gate_ref, pos_ref, cnt_ref, carry):
    i = pl.program_id(0)

    @pl.when(i == 0)
    def _():
        carry[...] = jnp.zeros(carry.shape, F32)

    l = lg_ref[...]
    e, ts = l.shape
    eidx = lax.broadcasted_iota(I32, (e, ts), 0)
    work = l
    top_v, top_i, hot = [], [], []
    for _ in range(TOP_K):
        m = jnp.max(work, axis=0, keepdims=True)
        sel = jnp.min(jnp.where(work == m, eidx, e), axis=0, keepdims=True)
        o = eidx == sel
        top_v.append(m)
        top_i.append(sel)
        hot.append(o)
        work = jnp.where(o, -jnp.inf, work)
    ex = [jnp.exp(v - top_v[0]) for v in top_v]
    den = ex[0] + ex[1] + ex[2] + ex[3]
    chosen = jnp.where(hot[0] | hot[1] | hot[2] | hot[3], 1.0, 0.0)
    r = lax.broadcasted_iota(I32, (ts, ts), 0)
    c = lax.broadcasted_iota(I32, (ts, ts), 1)
    before = jnp.where(r < c, 1.0, 0.0).astype(BF16)
    rank = jnp.dot(chosen.astype(BF16), before, preferred_element_type=F32) + carry[...]
    pos = [jnp.sum(jnp.where(o, rank, 0.0), axis=0, keepdims=True) for o in hot]
    carry[...] = carry[...] + jnp.sum(chosen, axis=1, keepdims=True)
    eid_ref[...] = jnp.concatenate(top_i, axis=0)
    gate_ref[...] = jnp.concatenate([x / den for x in ex], axis=0)
    pos_ref[...] = jnp.concatenate(pos, axis=0).astype(I32)
    cnt_ref[...] = jnp.broadcast_to(carry[...], cnt_ref.shape)


def _route_call(logits_t):
    e, t = logits_t.shape
    ts = ROUTE_TILE
    out = pl.BlockSpec((TOP_K, ts), lambda i: (0, i))
    return pl.pallas_call(
        _route_kernel,
        grid=(t // ts,),
        in_specs=[pl.BlockSpec((e, ts), lambda i: (0, i))],
        out_specs=[out, out, out, pl.BlockSpec((e, LANES), lambda i: (0, 0))],
        out_shape=[
            jax.ShapeDtypeStruct((TOP_K, t), I32),
            jax.ShapeDtypeStruct((TOP_K, t), F32),
            jax.ShapeDtypeStruct((TOP_K, t), I32),
            jax.ShapeDtypeStruct((e, LANES), F32),
        ],
        scratch_shapes=[pltpu.VMEM((e, 1), F32)],
        compiler_params=_params(("arbitrary",)),
        name="route_topk",
    )(logits_t)


def _dispatch_kernel(pstart_ref, eid_ref, pos_ref, hp_hbm, xs_in, xs_hbm, sem):
    del xs_in
    ch = eid_ref.shape[2]
    base = pl.program_id(0) * ch
    for k in range(TOP_K):
        def body(t, carry):
            dst = pstart_ref[eid_ref[0, k, t]] + pos_ref[0, k, t]
            pltpu.make_async_copy(hp_hbm.at[pl.ds(base + t, 1)], xs_hbm.at[pl.ds(dst, 1)], sem).start()
            return carry
        lax.fori_loop(0, ch, body, 0)
    pltpu.make_async_copy(hp_hbm.at[pl.ds(0, TOP_K * ch)], xs_hbm.at[pl.ds(0, TOP_K * ch)], sem).wait()


def _dispatch_call(pstart, eid3, pos3, hp, rows):
    t, w = hp.shape
    nt, _, ch = eid3.shape
    smem = pl.BlockSpec((1, TOP_K, ch), lambda i, ps: (i, 0, 0), memory_space=pltpu.SMEM)
    grid_spec = pltpu.PrefetchScalarGridSpec(
        num_scalar_prefetch=1,
        grid=(nt,),
        in_specs=[smem, smem, pl.BlockSpec(memory_space=pl.ANY), pl.BlockSpec(memory_space=pl.ANY)],
        out_specs=pl.BlockSpec(memory_space=pl.ANY),
        scratch_shapes=[pltpu.SemaphoreType.DMA],
    )
    return pl.pallas_call(
        _dispatch_kernel,
        grid_spec=grid_spec,
        out_shape=jax.ShapeDtypeStruct((rows, w), U32),
        input_output_aliases={4: 0},
        compiler_params=_params(("arbitrary",)),
        name="moe_dispatch",
    )(pstart, eid3, pos3, hp, jnp.zeros((rows, w), U32))


def _expert_kernel(be_ref, nu_ref, xs_ref, w1_ref, b1_ref, w2_ref, b2_ref, ys_ref, w1s, w2s):
    b = pl.program_id(0)
    e = be_ref[b]
    prev = be_ref[jnp.maximum(b - 1, 0)]
    f = w2_ref.shape[1]

    @pl.when((b == 0) | (e != prev))
    def _():
        w1s[...] = w1_ref[0].astype(BF16)
        w2s[...] = w2_ref[0].astype(BF16)

    @pl.when(b < nu_ref[0])
    def _():
        x = _unpack_pairs(xs_ref[...]).astype(BF16)
        gu = jnp.dot(x, w1s[...], preferred_element_type=F32) + b1_ref[0]
        gate = jnp.minimum(gu[:, :f], SWIGLU_LIMIT)
        up = jnp.clip(gu[:, f:], -SWIGLU_LIMIT, SWIGLU_LIMIT)
        act = (up + 1.0) * (gate * _sigmoid(SWIGLU_ALPHA * gate))
        y = jnp.dot(act.astype(BF16), w2s[...], preferred_element_type=F32) + b2_ref[0]
        ys_ref[...] = _pack_pairs(y)

    @pl.when(b >= nu_ref[0])
    def _():
        ys_ref[...] = jnp.zeros(ys_ref.shape, U32)


def _expert_call(layer, block_e, n_used, xs, w_in, b_in, w_out, b_out):
    rows, w = xs.shape
    _, ne, d, f2 = w_in.shape
    f = f2 // 2
    bm = EXPERT_ROWS
    x_spec = pl.BlockSpec((bm, w), lambda b, be, nu: (b, 0))
    grid_spec = pltpu.PrefetchScalarGridSpec(
        num_scalar_prefetch=2,
        grid=(rows // bm,),
        in_specs=[
            x_spec,
            pl.BlockSpec((1, d, f2), lambda b, be, nu: (layer * ne + be[b], 0, 0)),
            pl.BlockSpec((1, 1, f2), lambda b, be, nu: (layer * ne + be[b], 0, 0)),
            pl.BlockSpec((1, f, d), lambda b, be, nu: (layer * ne + be[b], 0, 0)),
            pl.BlockSpec((1, 1, d), lambda b, be, nu: (layer * ne + be[b], 0, 0)),
        ],
        out_specs=x_spec,
        scratch_shapes=[pltpu.VMEM((d, f2), BF16), pltpu.VMEM((f, d), BF16)],
    )
    depth = w_in.shape[0]
    return pl.pallas_call(
        _expert_kernel,
        grid_spec=grid_spec,
        out_shape=jax.ShapeDtypeStruct((rows, w), U32),
        compiler_params=_params(("arbitrary",)),
        name="moe_experts",
    )(block_e, n_used, xs,
      w_in.reshape(depth * ne, d, f2), b_in.reshape(depth * ne, 1, f2),
      w_out.reshape(depth * ne, f, d), b_out.reshape(depth * ne, 1, d))


def _combine_kernel(final, pstart_ref, eid_ref, pos_ref, x_ref, gt_ref, g2, fg, ys_hbm, xo_ref, ybuf, sem):
    ts = x_ref.shape[1]
    for k in range(TOP_K):
        def body(t, carry):
            src = pstart_ref[eid_ref[0, k, t]] + pos_ref[0, k, t]
            pltpu.make_async_copy(ys_hbm.at[pl.ds(src, 1)], ybuf.at[k, pl.ds(t, 1)], sem).start()
            return carry
        lax.fori_loop(0, ts, body, 0)
    for k in range(TOP_K):
        pltpu.make_async_copy(ys_hbm.at[pl.ds(0, ts)], ybuf.at[k], sem).wait()
    gt = gt_ref[...]
    moe = gt[:, 0:1] * _unpack_pairs(ybuf[0])
    for k in range(1, TOP_K):
        moe = moe + gt[:, k:k + 1] * _unpack_pairs(ybuf[k])
    xn = x_ref[0] + g2[0] * moe
    if final:
        r = lax.rsqrt(jnp.mean(xn * xn, axis=-1, keepdims=True) + EPS)
        xn = (xn * r) * fg[...]
    xo_ref[0] = xn


def _combine_call(layer, final, pstart, eid3, pos3, x, gate_t, modr, fg, ys):
    batch, seq, d = x.shape
    _, w = ys.shape
    ts = DISPATCH_TILE
    ns = seq // ts
    smem = pl.BlockSpec((1, TOP_K, ts), lambda b, s, ps: (b * ns + s, 0, 0), memory_space=pltpu.SMEM)
    x_spec = pl.BlockSpec((1, ts, d), lambda b, s, ps: (b, s, 0))
    grid_spec = pltpu.PrefetchScalarGridSpec(
        num_scalar_prefetch=1,
        grid=(batch, ns),
        in_specs=[
            smem, smem, x_spec,
            pl.BlockSpec((ts, TOP_K), lambda b, s, ps: (b * ns + s, 0)),
            pl.BlockSpec((1, 1, d), lambda b, s, ps: ((layer * batch + b) * 6 + 5, 0, 0)),
            pl.BlockSpec((1, d), lambda b, s, ps: (0, 0)),
            pl.BlockSpec(memory_space=pl.ANY),
        ],
        out_specs=x_spec,
        scratch_shapes=[pltpu.VMEM((TOP_K, ts, w), U32), pltpu.SemaphoreType.DMA],
    )
    return pl.pallas_call(
        functools.partial(_combine_kernel, final),
        grid_spec=grid_spec,
        out_shape=jax.ShapeDtypeStruct((batch, seq, d), F32),
        compiler_params=_params(("arbitrary", "arbitrary")),
        name="moe_combine",
    )(pstart, eid3, pos3, x, gate_t, modr, fg, ys)


def _moe(layer, final, x, hp, logits_t, modr, fg, w_in, b_in, w_out, b_out):
    t = hp.shape[0]
    ne = logits_t.shape[0]
    bm = EXPERT_ROWS
    eid, gate, pos, cnt = _route_call(logits_t)
    counts = cnt[:, 0].astype(I32)
    nblk = (counts + bm - 1) // bm
    bend = jnp.cumsum(nblk)
    pstart = ((bend - nblk) * bm).astype(I32)
    n_blocks = (t * TOP_K) // bm + ne
    block_e = jnp.minimum(jnp.searchsorted(bend, jnp.arange(n_blocks, dtype=I32), side="right"),
                          ne - 1).astype(I32)
    n_used = bend[-1:].astype(I32)
    ch = DISPATCH_TILE
    eid3 = eid.reshape(TOP_K, t // ch, ch).transpose(1, 0, 2)
    pos3 = pos.reshape(TOP_K, t // ch, ch).transpose(1, 0, 2)
    xs = _dispatch_call(pstart, eid3, pos3, hp, n_blocks * bm)
    ys = _expert_call(layer, block_e, n_used, xs, w_in, b_in, w_out, b_out)
    return _combine_call(layer, final, pstart, eid3, pos3, x, gate.T, modr, fg, ys)


def kernel(x, c, norm_mix_g, norm_ffn_g, ada_w, ada_b, pool_w, pool_scale, fox_w_in, fox_b_f, fox_w_o,
           router_w, router_b, exp_w_in, exp_b_in, exp_w_out, exp_b_out, final_g):
    batch, seq, d = x.shape
    depth = ada_w.shape[0]
    mod = _ada_call(c, ada_w, ada_b)
    modr = mod.reshape(depth * batch * 6, 1, d)
    fg = final_g.reshape(1, d)
    for i in range(depth):
        gm = norm_mix_g[i].reshape(1, d)
        gf = norm_ffn_g[i].reshape(1, d)
        rwt = router_w[i].T
        rb = router_b[i].reshape(-1, 1)
        j = i // 2
        if i % 2 == 0:
            x, hp, lg = _pool_call(i, x, modr, gm, gf, pool_w[j], pool_scale[j].reshape(1, d), rwt, rb)
        else:
            w = fox_w_in[j]
            q, k, v, f2 = _qkv_call(i, x, modr, gm, w[:, :3 * d].astype(BF16), w[:, 3 * d:].T,
                                    fox_b_f[j].reshape(-1, 1))
            o = _attn_call(q, k, v, f2)
            x, hp, lg = _wo_call(i, x, o, modr, gf, fox_w_o[j].astype(BF16), rwt, rb)
        x = _moe(i, i == depth - 1, x, hp, lg, modr, fg, exp_w_in, exp_b_in, exp_w_out, exp_b_out)
    return x
```

```python
import functools

import jax
import jax.numpy as jnp
from jax import lax
from jax.experimental import pallas as pl
from jax.experimental.pallas import tpu as pltpu
from jax.experimental.pallas import tpu_sc as plsc

F32 = jnp.float32
BF16 = jnp.bfloat16
U32 = jnp.uint32
I32 = jnp.int32

POOL_WINDOWS = (2, 4, 8, 16)
POOL_HALO = 16
HEAD_DIM = 64
HEADS_PER_STEP = 2
TOP_K = 4
SWIGLU_LIMIT = 7.0
SWIGLU_ALPHA = 1.702
EPS = 1e-6
NEG_BIG = -1e30
LOG2E = 1.4426950408889634
BIAS_PARTS = 3

LANES = 128
SEQ_TILE = 512
ATTN_TILE = 512
ATTN_QUERY_SUBTILES = 4
ROUTE_TILE = 512
DEST_TILE = 4096
EXPERT_ROWS = 512
COMBINE_TILE = 512
ADA_COLS = 1536
VMEM_LIMIT_BYTES = 56 * 1024 * 1024

SC_CORES = 2
SC_SUBCORES = 16
SC_WORKERS = SC_CORES * SC_SUBCORES
SC_CHUNK = 64

NT_DIMS = (((1,), (1,)), ((), ()))


def _params(sem, vmem=VMEM_LIMIT_BYTES):
    return pltpu.CompilerParams(dimension_semantics=sem, vmem_limit_bytes=vmem)


def _sigmoid(z):
    return 1.0 / (1.0 + jnp.exp(-z))


def _norm_mod(x, g, scale, shift):
    r = lax.rsqrt(jnp.mean(x * x, axis=-1, keepdims=True) + EPS)
    return (x * r) * (g * (1.0 + scale)) + shift


def _pack_pairs(y):
    w = y.shape[1] // 2
    hi = lax.bitcast_convert_type(y[:, :w].astype(BF16).astype(F32), U32)
    lo = lax.bitcast_convert_type(y[:, w:].astype(BF16).astype(F32), U32)
    return hi | (lo >> 16)


def _unpack_pairs(w):
    hi = lax.bitcast_convert_type(w & jnp.uint32(0xFFFF0000), F32)
    lo = lax.bitcast_convert_type(w << 16, F32)
    return jnp.concatenate([hi, lo], axis=1)


def _split_bf16(x, parts):
    out = []
    r = x
    for _ in range(parts):
        p = r.astype(BF16)
        out.append(p)
        r = r - p.astype(F32)
    return out


def _ffn_prenorm(xn, gf, sc2, sh2, rwt, rb):
    h2 = _norm_mod(xn, gf, sc2, sh2)
    h_hi, h_lo = _split_bf16(h2, 2)
    w_hi, w_lo = _split_bf16(rwt, 2)
    lg = (lax.dot_general(w_hi, h_hi, NT_DIMS, preferred_element_type=F32)
          + lax.dot_general(w_lo, h_hi, NT_DIMS, preferred_element_type=F32)
          + lax.dot_general(w_hi, h_lo, NT_DIMS, preferred_element_type=F32)) + rb
    return _pack_pairs(h2), lg


def _mod_spec(layer, batch, j):
    def index(b, s):
        return ((layer * batch + b) * 6 + j, 0, 0)
    return index


def _ada_kernel(ct_ref, w_ref, b_ref, o_ref):
    ct = ct_ref[...]
    cond = ct * _sigmoid(ct)
    w = w_ref[0]
    rows = []
    for b in range(ct.shape[1]):
        rows.append(jnp.sum(w * cond[:, b:b + 1], axis=0, keepdims=True))
    o_ref[0] = jnp.concatenate(rows, axis=0) + b_ref[0]


def _ada_call(c, ada_w, ada_b):
    depth, d, n = ada_w.shape
    batch = c.shape[0]
    tn = ADA_COLS
    return pl.pallas_call(
        _ada_kernel,
        grid=(depth, n // tn),
        in_specs=[
            pl.BlockSpec((d, batch), lambda i, j: (0, 0)),
            pl.BlockSpec((1, d, tn), lambda i, j: (i, 0, j)),
            pl.BlockSpec((1, 1, tn), lambda i, j: (i, 0, j)),
        ],
        out_specs=pl.BlockSpec((1, batch, tn), lambda i, j: (i, 0, j)),
        out_shape=jax.ShapeDtypeStruct((depth, batch, n), F32),
        compiler_params=_params(("arbitrary", "arbitrary")),
        name="ada_mod",
    )(c.T, ada_w, ada_b.reshape(depth, 1, n))


def _pool_kernel(x_ref, sh1, sc1, g1, sh2, sc2, gm, gf, pw_ref, ps_ref, rwt_ref, rb_ref,
                 xo_ref, hp_ref, lg_ref, hbuf):
    s = pl.program_id(1)
    ts = x_ref.shape[1]
    d = x_ref.shape[2]
    pg = d // len(POOL_WINDOWS)
    x = x_ref[0]
    h = _norm_mod(x, gm[...], sc1[0], sh1[0])

    @pl.when(s == 0)
    def _():
        hbuf[0:POOL_HALO, :] = jnp.zeros((POOL_HALO, d), F32)

    @pl.when(s > 0)
    def _():
        hbuf[0:POOL_HALO, :] = hbuf[ts:ts + POOL_HALO, :]

    hbuf[POOL_HALO:POOL_HALO + ts, :] = h
    pos = s * ts + lax.broadcasted_iota(I32, (ts, 1), 0)
    ys = []
    for g, w in enumerate(POOL_WINDOWS):
        lo = g * pg
        hg = h[:, lo:lo + pg]
        acc = hg
        for k in range(1, w):
            acc = acc + hbuf[POOL_HALO - k:POOL_HALO - k + ts, lo:lo + pg]
        cnt = jnp.minimum(pos + 1, w).astype(F32)
        dlt = acc / cnt - hg
        ys.append(jnp.dot(dlt.astype(BF16), pw_ref[g].astype(BF16), preferred_element_type=F32))
    y = jnp.concatenate(ys, axis=1) * ps_ref[...]
    xn = x + g1[0] * y
    xo_ref[0] = xn
    hp, lg = _ffn_prenorm(xn, gf[...], sc2[0], sh2[0], rwt_ref[...], rb_ref[...])
    hp_ref[...] = hp
    lg_ref[...] = lg


def _row_layer_specs(layer, batch, ns, ts, d, e):
    row = lambda j: pl.BlockSpec((1, 1, d), _mod_spec(layer, batch, j))
    vec = pl.BlockSpec((1, d), lambda b, s: (0, 0))
    x_spec = pl.BlockSpec((1, ts, d), lambda b, s: (b, s, 0))
    out_specs = [
        x_spec,
        pl.BlockSpec((ts, d // 2), lambda b, s: (b * ns + s, 0)),
        pl.BlockSpec((e, ts), lambda b, s: (0, b * ns + s)),
    ]
    return row, vec, x_spec, out_specs


def _pool_call(layer, x, modr, gm, gf, pw, ps, rwt, rb):
    batch, seq, d = x.shape
    e = rwt.shape[0]
    ts = SEQ_TILE
    ns = seq // ts
    g = len(POOL_WINDOWS)
    pg = d // g
    row, vec, x_spec, out_specs = _row_layer_specs(layer, batch, ns, ts, d, e)
    return pl.pallas_call(
        _pool_kernel,
        grid=(batch, ns),
        in_specs=[
            x_spec, row(0), row(1), row(2), row(3), row(4), vec, vec,
            pl.BlockSpec((g, pg, pg), lambda b, s: (0, 0, 0)),
            vec,
            pl.BlockSpec((e, d), lambda b, s: (0, 0)),
            pl.BlockSpec((e, 1), lambda b, s: (0, 0)),
        ],
        out_specs=out_specs,
        out_shape=[
            jax.ShapeDtypeStruct((batch, seq, d), F32),
            jax.ShapeDtypeStruct((batch * seq, d // 2), U32),
            jax.ShapeDtypeStruct((e, batch * seq), F32),
        ],
        scratch_shapes=[pltpu.VMEM((ts + POOL_HALO, d), F32)],
        compiler_params=_params(("arbitrary", "arbitrary")),
        name="pool_layer",
    )(x, modr, modr, modr, modr, modr, gm, gf, pw, ps, rwt, rb)


def _qkv_kernel(x_ref, sh1, sc1, gm, w_ref, wf_ref, bf_ref, q_ref, k_ref, v_ref, carry):
    s = pl.program_id(1)
    ts = x_ref.shape[1]
    d = x_ref.shape[2]
    nh = wf_ref.shape[1]
    h = _norm_mod(x_ref[0], gm[...], sc1[0], sh1[0])
    h_hi, h_lo = _split_bf16(h, 2)
    qkv = jnp.dot(h_hi, w_ref[...], preferred_element_type=F32)

    w_hi, w_lo = _split_bf16(wf_ref[...], 2)
    fl = (jnp.dot(h_hi, w_hi, preferred_element_type=F32)
          + jnp.dot(h_hi, w_lo, preferred_element_type=F32)
          + jnp.dot(h_lo, w_hi, preferred_element_type=F32)) + bf_ref[...]
    logf = jnp.minimum(fl, 0.0) - jnp.log(1.0 + jnp.exp(-jnp.abs(fl)))

    @pl.when(s == 0)
    def _():
        carry[...] = jnp.zeros(carry.shape, F32)

    r = lax.broadcasted_iota(I32, (ts, ts), 0)
    c = lax.broadcasted_iota(I32, (ts, ts), 1)
    lower = jnp.where(r >= c, 1.0, 0.0).astype(BF16)
    parts = jnp.concatenate(_split_bf16(logf, 3), axis=1)
    cs = jnp.dot(lower, parts, preferred_element_type=F32)
    cum = cs[:, :nh] + cs[:, nh:2 * nh] + cs[:, 2 * nh:] + carry[...]
    carry[...] = cum[ts - 1:ts, :]
    bias = [p.astype(F32) for p in _split_bf16(cum * (-LOG2E), BIAS_PARTS)]

    lane = lax.broadcasted_iota(I32, (ts, LANES), 1)
    real = lane < HEAD_DIM
    q_aug = jnp.where(lane < HEAD_DIM + BIAS_PARTS, 1.0, 0.0)
    v_aug = jnp.where(lane == HEAD_DIM, 1.0, 0.0)
    q_scale = HEAD_DIM ** -0.5 * LOG2E
    for hd in range(nh):
        lo = (hd // 2) * LANES
        tiles = [qkv[:, t * d + lo:t * d + lo + LANES] for t in range(3)]
        if hd % 2:
            tiles = [pltpu.roll(t, HEAD_DIM, 1) for t in tiles]
        k_aug = jnp.zeros((ts, LANES), F32)
        for j in range(BIAS_PARTS):
            k_aug = jnp.where(lane == HEAD_DIM + j, bias[j][:, hd:hd + 1], k_aug)
        q_ref[0, hd] = jnp.where(real, tiles[0] * q_scale, q_aug).astype(BF16)
        k_ref[0, hd] = jnp.where(real, tiles[1], k_aug).astype(BF16)
        v_ref[0, hd] = jnp.where(real, tiles[2], v_aug).astype(BF16)


def _qkv_call(layer, x, modr, gm, wqkv, wf, bf):
    batch, seq, d = x.shape
    nh = wf.shape[1]
    ts = SEQ_TILE
    ns = seq // ts
    row = lambda j: pl.BlockSpec((1, 1, d), _mod_spec(layer, batch, j))
    head_spec = pl.BlockSpec((1, nh, ts, LANES), lambda b, s: (b, 0, s, 0))
    return pl.pallas_call(
        _qkv_kernel,
        grid=(batch, ns),
        in_specs=[
            pl.BlockSpec((1, ts, d), lambda b, s: (b, s, 0)), row(0), row(1),
            pl.BlockSpec((1, d), lambda b, s: (0, 0)),
            pl.BlockSpec((d, 3 * d), lambda b, s: (0, 0)),
            pl.BlockSpec((d, nh), lambda b, s: (0, 0)),
            pl.BlockSpec((1, nh), lambda b, s: (0, 0)),
        ],
        out_specs=[head_spec] * 3,
        out_shape=[jax.ShapeDtypeStruct((batch, nh, seq, LANES), BF16)] * 3,
        scratch_shapes=[pltpu.VMEM((1, nh), F32)],
        compiler_params=_params(("arbitrary", "arbitrary")),
        name="fox_qkv",
    )(x, modr, modr, gm, wqkv, wf, bf)


def _attn_kernel(q_ref, k_ref, v_ref, o_ref, m_s, acc_s):
    qi = pl.program_id(2)
    tk = ATTN_TILE
    tq = q_ref.shape[2]
    subs = tq // tk
    m_s[...] = jnp.full(m_s.shape, NEG_BIG, F32)
    acc_s[...] = jnp.zeros(acc_s.shape, F32)
    row = lax.broadcasted_iota(I32, (tk, tk), 0)
    col = lax.broadcasted_iota(I32, (tk, tk), 1)

    def step(kb, first, diagonal):
        start = pl.multiple_of(kb * tk, tk)
        rows = slice(first * tk, tq)
        for hh in range(HEADS_PER_STEP):
            kk = k_ref[0, hh, pl.ds(start, tk), :]
            vv = v_ref[0, hh, pl.ds(start, tk), :]
            z = lax.dot_general(q_ref[0, hh, rows, :], kk, NT_DIMS, preferred_element_type=F32)
            if diagonal:
                masked = jnp.where(row >= col, z[:tk], NEG_BIG)
                z = masked if first == subs - 1 else jnp.concatenate([masked, z[tk:]], axis=0)
            m_prev = m_s[hh, rows, :]
            m_new = jnp.maximum(m_prev, jnp.max(z, axis=1, keepdims=True))
            p = jnp.exp2(z - jnp.concatenate([m_new] * (tk // LANES), axis=1))
            acc_s[hh, rows, :] = (jnp.exp2(m_prev - m_new) * acc_s[hh, rows, :]
                                  + jnp.dot(p.astype(BF16), vv, preferred_element_type=F32))
            m_s[hh, rows, :] = m_new

    def body(kb, carry):
        step(kb, 0, False)
        return carry

    lax.fori_loop(0, subs * qi, body, 0)
    for j in range(subs):
        step(subs * qi + j, j, True)
    lane = lax.broadcasted_iota(I32, (tq, LANES), 1)
    outs = []
    for hh in range(HEADS_PER_STEP):
        a = acc_s[hh]
        outs.append(a / a[:, HEAD_DIM:HEAD_DIM + 1])
    o = jnp.where(lane < HEAD_DIM, outs[0], pltpu.roll(outs[1], HEAD_DIM, 1))
    o_ref[0] = o.astype(BF16)


def _attn_call(q, k, v):
    batch, nh, seq, _ = q.shape
    tq = ATTN_QUERY_SUBTILES * ATTN_TILE
    kv_spec = pl.BlockSpec((1, HEADS_PER_STEP, seq, LANES), lambda b, h, i: (b, h, 0, 0))
    return pl.pallas_call(
        _attn_kernel,
        grid=(batch, nh // HEADS_PER_STEP, seq // tq),
        in_specs=[pl.BlockSpec((1, HEADS_PER_STEP, tq, LANES), lambda b, h, i: (b, h, i, 0)),
                  kv_spec, kv_spec],
        out_specs=pl.BlockSpec((1, tq, LANES), lambda b, h, i: (b, i, h)),
        out_shape=jax.ShapeDtypeStruct((batch, seq, nh * HEAD_DIM), BF16),
        scratch_shapes=[pltpu.VMEM((HEADS_PER_STEP, tq, LANES), F32)] * 2,
        compiler_params=_params(("arbitrary", "arbitrary", "arbitrary")),
        name="fox_attention",
    )(q, k, v)


def _wo_kernel(x_ref, o_ref, g1, sh2, sc2, gf, wo_ref, rwt_ref, rb_ref, xo_ref, hp_ref, lg_ref):
    m = jnp.dot(o_ref[0], wo_ref[...], preferred_element_type=F32)
    xn = x_ref[0] + g1[0] * m
    xo_ref[0] = xn
    hp, lg = _ffn_prenorm(xn, gf[...], sc2[0], sh2[0], rwt_ref[...], rb_ref[...])
    hp_ref[...] = hp
    lg_ref[...] = lg


def _wo_call(layer, x, o, modr, gf, wo, rwt, rb):
    batch, seq, d = x.shape
    e = rwt.shape[0]
    ts = SEQ_TILE
    ns = seq // ts
    row, vec, x_spec, out_specs = _row_layer_specs(layer, batch, ns, ts, d, e)
    return pl.pallas_call(
        _wo_kernel,
        grid=(batch, ns),
        in_specs=[
            x_spec, x_spec, row(2), row(3), row(4), vec,
            pl.BlockSpec((d, d), lambda b, s: (0, 0)),
            pl.BlockSpec((e, d), lambda b, s: (0, 0)),
            pl.BlockSpec((e, 1), lambda b, s: (0, 0)),
        ],
        out_specs=out_specs,
        out_shape=[
            jax.ShapeDtypeStruct((batch, seq, d), F32),
            jax.ShapeDtypeStruct((batch * seq, d // 2), U32),
            jax.ShapeDtypeStruct((e, batch * seq), F32),
        ],
        compiler_params=_params(("arbitrary", "arbitrary")),
        name="fox_out",
    )(x, o, modr, modr, modr, gf, wo, rwt, rb)


def _route_kernel(lg_ref, eid_ref, gate_ref, pos_ref, cnt_ref, carry):
    i = pl.program_id(0)

    @pl.when(i == 0)
    def _():
        carry[...] = jnp.zeros(carry.shape, F32)

    l = lg_ref[...]
    e, ts = l.shape
    eidx = lax.broadcasted_iota(I32, (e, ts), 0)
    work = l
    top_v, top_i, hot = [], [], []
    for _ in range(TOP_K):
        m = jnp.max(work, axis=0, keepdims=True)
        sel = jnp.min(jnp.where(work == m, eidx, e), axis=0, keepdims=True)
        o = eidx == sel
        top_v.append(m)
        top_i.append(sel)
        hot.append(o)
        work = jnp.where(o, -jnp.inf, work)
    ex = [jnp.exp(v - top_v[0]) for v in top_v]
    den = ex[0] + ex[1] + ex[2] + ex[3]
    chosen = jnp.where(hot[0] | hot[1] | hot[2] | hot[3], 1.0, 0.0)
    r = lax.broadcasted_iota(I32, (ts, ts), 0)
    c = lax.broadcasted_iota(I32, (ts, ts), 1)
    before = jnp.where(r < c, 1.0, 0.0).astype(BF16)
    rank = jnp.dot(chosen.astype(BF16), before, preferred_element_type=F32) + carry[...]
    pos = [jnp.sum(jnp.where(o, rank, 0.0), axis=0, keepdims=True) for o in hot]
    carry[...] = carry[...] + jnp.sum(chosen, axis=1, keepdims=True)
    eid_ref[...] = jnp.concatenate(top_i, axis=0)
    gate_ref[...] = jnp.concatenate([x / den for x in ex], axis=0)
    pos_ref[...] = jnp.concatenate(pos, axis=0).astype(I32)
    cnt_ref[...] = jnp.broadcast_to(carry[...], cnt_ref.shape)


def _route_call(logits_t):
    e, t = logits_t.shape
    ts = ROUTE_TILE
    out = pl.BlockSpec((TOP_K, ts), lambda i: (0, i))
    return pl.pallas_call(
        _route_kernel,
        grid=(t // ts,),
        in_specs=[pl.BlockSpec((e, ts), lambda i: (0, i))],
        out_specs=[out, out, out, pl.BlockSpec((e, LANES), lambda i: (0, 0))],
        out_shape=[
            jax.ShapeDtypeStruct((TOP_K, t), I32),
            jax.ShapeDtypeStruct((TOP_K, t), F32),
            jax.ShapeDtypeStruct((TOP_K, t), I32),
            jax.ShapeDtypeStruct((e, LANES), F32),
        ],
        scratch_shapes=[pltpu.VMEM((e, 1), F32)],
        compiler_params=_params(("arbitrary",)),
        name="route_topk",
    )(logits_t)


def _dest_kernel(pstart_ref, eid_ref, pos_ref, dest_ref):
    eid = eid_ref[...]
    dest = pos_ref[...]
    for e in range(pstart_ref.shape[0]):
        dest = dest + jnp.where(eid == e, pstart_ref[e], 0)
    dest_ref[...] = dest


def _dest_call(pstart, eid, pos):
    k, t = eid.shape
    tt = min(DEST_TILE, t)
    spec = pl.BlockSpec((k, tt), lambda i, ps: (0, i))
    grid_spec = pltpu.PrefetchScalarGridSpec(
        num_scalar_prefetch=1, grid=(t // tt,), in_specs=[spec, spec], out_specs=spec)
    return pl.pallas_call(
        _dest_kernel,
        grid_spec=grid_spec,
        out_shape=jax.ShapeDtypeStruct((k, t), I32),
        compiler_params=_params(("arbitrary",)),
        name="route_dest",
    )(pstart, eid, pos)


def _sc_worker_chunks(total_chunks):
    worker = lax.axis_index("s") * SC_CORES + lax.axis_index("c")
    per_worker = total_chunks // SC_WORKERS
    return worker * per_worker, per_worker


def _sc_dispatch_call(hp, dest3, rows):
    t, w = hp.shape
    nchunk = dest3.shape[0]
    mesh = plsc.VectorSubcoreMesh(core_axis_name="c", subcore_axis_name="s")

    @functools.partial(
        pl.kernel, mesh=mesh, out_type=jax.ShapeDtypeStruct((rows, w), U32),
        scratch_types=[pltpu.VMEM((TOP_K, SC_CHUNK), I32), pltpu.VMEM((SC_CHUNK, w), U32)],
        name="moe_dispatch")
    def dispatch(hp_hbm, dest_hbm, xs_hbm, idx_v, rows_v):
        first, count = _sc_worker_chunks(nchunk)

        @pl.loop(0, count)
        def _(j):
            c = first + j
            pltpu.sync_copy(dest_hbm.at[c], idx_v)
            pltpu.sync_copy(hp_hbm.at[pl.ds(c * SC_CHUNK, SC_CHUNK)], rows_v)
            for k in range(TOP_K):
                pltpu.sync_copy(rows_v, xs_hbm.at[idx_v.at[k]])

    return dispatch(hp, dest3)


def _sc_gather_call(ys, idx):
    n = idx.shape[0]
    _, w = ys.shape
    nchunk = n // SC_CHUNK
    mesh = plsc.VectorSubcoreMesh(core_axis_name="c", subcore_axis_name="s")

    @functools.partial(
        pl.kernel, mesh=mesh, out_type=jax.ShapeDtypeStruct((n, w), U32),
        scratch_types=[pltpu.VMEM((SC_CHUNK,), I32), pltpu.VMEM((SC_CHUNK, w), U32),
                       pltpu.SemaphoreType.DMA],
        name="moe_gather")
    def gather(ys_hbm, idx_hbm, out_hbm, idx_v, rows_v, sem):
        first, count = _sc_worker_chunks(nchunk)

        @pl.loop(0, count)
        def _(j):
            c = first + j
            pltpu.sync_copy(idx_hbm.at[pl.ds(c * SC_CHUNK, SC_CHUNK)], idx_v)
            pltpu.async_copy(ys_hbm.at[idx_v], rows_v, sem).wait()
            pltpu.sync_copy(rows_v, out_hbm.at[pl.ds(c * SC_CHUNK, SC_CHUNK)])

    return gather(ys, idx)


def _expert_kernel(be_ref, nv_ref, xs_ref, w1_ref, b1_ref, w2_ref, b2_ref, ys_ref, w1s, w2s):
    b = pl.program_id(0)
    e = be_ref[b]
    prev = be_ref[jnp.maximum(b - 1, 0)]
    nvalid = nv_ref[b]
    f = w2_ref.shape[1]

    @pl.when((b == 0) | (e != prev))
    def _():
        w1s[...] = w1_ref[0].astype(BF16)
        w2s[...] = w2_ref[0].astype(BF16)

    @pl.when(nvalid > 0)
    def _():
        rows = lax.broadcasted_iota(I32, (xs_ref.shape[0], 1), 0)
        xw = jnp.where(rows < nvalid, xs_ref[...], jnp.uint32(0))
        x = _unpack_pairs(xw).astype(BF16)
        gu = jnp.dot(x, w1s[...], preferred_element_type=F32) + b1_ref[0]
        gate = jnp.minimum(gu[:, :f], SWIGLU_LIMIT)
        up = jnp.clip(gu[:, f:], -SWIGLU_LIMIT, SWIGLU_LIMIT)
        act = (up + 1.0) * (gate * _sigmoid(SWIGLU_ALPHA * gate))
        y = jnp.dot(act.astype(BF16), w2s[...], preferred_element_type=F32) + b2_ref[0]
        ys_ref[...] = _pack_pairs(y)

    @pl.when(nvalid <= 0)
    def _():
        ys_ref[...] = jnp.zeros(ys_ref.shape, U32)


def _expert_call(layer, block_e, nvalid, xs, w_in, b_in, w_out, b_out):
    rows, w = xs.shape
    depth, ne, d, f2 = w_in.shape
    f = f2 // 2
    bm = EXPERT_ROWS
    x_spec = pl.BlockSpec((bm, w), lambda b, be, nv: (b, 0))
    grid_spec = pltpu.PrefetchScalarGridSpec(
        num_scalar_prefetch=2,
        grid=(rows // bm,),
        in_specs=[
            x_spec,
            pl.BlockSpec((1, d, f2), lambda b, be, nv: (layer * ne + be[b], 0, 0)),
            pl.BlockSpec((1, 1, f2), lambda b, be, nv: (layer * ne + be[b], 0, 0)),
            pl.BlockSpec((1, f, d), lambda b, be, nv: (layer * ne + be[b], 0, 0)),
            pl.BlockSpec((1, 1, d), lambda b, be, nv: (layer * ne + be[b], 0, 0)),
        ],
        out_specs=x_spec,
        scratch_shapes=[pltpu.VMEM((d, f2), BF16), pltpu.VMEM((f, d), BF16)],
    )
    return pl.pallas_call(
        _expert_kernel,
        grid_spec=grid_spec,
        out_shape=jax.ShapeDtypeStruct((rows, w), U32),
        compiler_params=_params(("arbitrary",)),
        name="moe_experts",
    )(block_e, nvalid, xs,
      w_in.reshape(depth * ne, d, f2), b_in.reshape(depth * ne, 1, f2),
      w_out.reshape(depth * ne, f, d), b_out.reshape(depth * ne, 1, d))


def _combine_kernel(final, x_ref, yg_ref, gt_ref, g2, fg, xo_ref):
    w = yg_ref.shape[1] // TOP_K
    gt = gt_ref[...]
    moe = gt[:, 0:1] * _unpack_pairs(yg_ref[:, 0:w])
    for k in range(1, TOP_K):
        moe = moe + gt[:, k:k + 1] * _unpack_pairs(yg_ref[:, k * w:(k + 1) * w])
    xn = x_ref[0] + g2[0] * moe
    if final:
        r = lax.rsqrt(jnp.mean(xn * xn, axis=-1, keepdims=True) + EPS)
        xn = (xn * r) * fg[...]
    xo_ref[0] = xn


def _combine_call(layer, final, x, yg, gate_t, modr, fg):
    batch, seq, d = x.shape
    kw = yg.shape[1]
    ts = COMBINE_TILE
    ns = seq // ts
    x_spec = pl.BlockSpec((1, ts, d), lambda b, s: (b, s, 0))
    return pl.pallas_call(
        functools.partial(_combine_kernel, final),
        grid=(batch, ns),
        in_specs=[
            x_spec,
            pl.BlockSpec((ts, kw), lambda b, s: (b * ns + s, 0)),
            pl.BlockSpec((ts, TOP_K), lambda b, s: (b * ns + s, 0)),
            pl.BlockSpec((1, 1, d), _mod_spec(layer, batch, 5)),
            pl.BlockSpec((1, d), lambda b, s: (0, 0)),
        ],
        out_specs=x_spec,
        out_shape=jax.ShapeDtypeStruct((batch, seq, d), F32),
        compiler_params=_params(("arbitrary", "arbitrary")),
        name="moe_combine",
    )(x, yg, gate_t, modr, fg)


def _moe(layer, final, x, hp, logits_t, modr, fg, w_in, b_in, w_out, b_out):
    t, w = hp.shape
    ne = logits_t.shape[0]
    bm = EXPERT_ROWS
    eid, gate, pos, cnt = _route_call(logits_t)
    counts = cnt[:, 0].astype(I32)
    nblk = (counts + bm - 1) // bm
    bend = jnp.cumsum(nblk)
    bstart = bend - nblk
    pstart = (bstart * bm).astype(I32)
    n_blocks = (t * TOP_K) // bm + ne
    blocks = jnp.arange(n_blocks, dtype=I32)
    block_e = jnp.minimum(jnp.sum(blocks[:, None] >= bend[None, :], axis=1), ne - 1).astype(I32)
    mine = block_e[:, None] == jnp.arange(ne, dtype=I32)[None, :]
    left = jnp.sum(jnp.where(mine, counts[None, :] - (blocks[:, None] - bstart[None, :]) * bm, 0), axis=1)
    nvalid = jnp.where(blocks < bend[-1], jnp.clip(left, 0, bm), 0).astype(I32)
    dest = _dest_call(pstart, eid, pos)
    dest3 = dest.reshape(TOP_K, t // SC_CHUNK, SC_CHUNK).transpose(1, 0, 2)
    xs = _sc_dispatch_call(hp, dest3, n_blocks * bm)
    ys = _expert_call(layer, block_e, nvalid, xs, w_in, b_in, w_out, b_out)
    yg = _sc_gather_call(ys, dest.T.reshape(t * TOP_K))
    return _combine_call(layer, final, x, yg.reshape(t, TOP_K * w), gate.T, modr, fg)


def kernel(x, c, norm_mix_g, norm_ffn_g, ada_w, ada_b, pool_w, pool_scale, fox_w_in, fox_b_f, fox_w_o,
           router_w, router_b, exp_w_in, exp_b_in, exp_w_out, exp_b_out, final_g):
    batch, seq, d = x.shape
    depth = ada_w.shape[0]
    mod = _ada_call(c, ada_w, ada_b)
    modr = mod.reshape(depth * batch * 6, 1, d)
    fg = final_g.reshape(1, d)
    for i in range(depth):
        gm = norm_mix_g[i].reshape(1, d)
        gf = norm_ffn_g[i].reshape(1, d)
        rwt = router_w[i].T
        rb = router_b[i].reshape(-1, 1)
        j = i // 2
        if i % 2 == 0:
            x, hp, lg = _pool_call(i, x, modr, gm, gf, pool_w[j], pool_scale[j].reshape(1, d), rwt, rb)
        else:
            w = fox_w_in[j]
            q, k, v = _qkv_call(i, x, modr, gm, w[:, :3 * d].astype(BF16), w[:, 3 * d:],
                                fox_b_f[j].reshape(1, -1))
            o = _attn_call(q, k, v)
            x, hp, lg = _wo_call(i, x, o, modr, gf, fox_w_o[j].astype(BF16), rwt, rb)
        x = _moe(i, i == depth - 1, x, hp, lg, modr, fg, exp_w_in, exp_b_in, exp_w_out, exp_b_out)
    return x
```

```python
import functools

import jax
import jax.numpy as jnp
import numpy as np
from jax import lax
from jax.experimental import pallas as pl
from jax.experimental.pallas import tpu as pltpu
from jax.experimental.pallas import tpu_sc as plsc

F32 = jnp.float32
BF16 = jnp.bfloat16
U32 = jnp.uint32
I32 = jnp.int32

POOL_WINDOWS = (2, 4, 8, 16)
POOL_HALO = 16
HEAD_DIM = 64
HEADS_PER_STEP = 2
TOP_K = 4
SWIGLU_LIMIT = 7.0
SWIGLU_ALPHA = 1.702
EPS = 1e-6
NEG_BIG = -1e30
LOG2E = 1.4426950408889634
BIAS_PARTS = 3

LANES = 128
SEQ_TILE = 512
ATTN_TILE = 512
ATTN_QUERY_SUBTILES = 4
ROUTE_TILE = 512
DEST_TILE = 4096
EXPERT_ROWS = 512
COMBINE_TILE = 512
ADA_COLS = 1536
VMEM_LIMIT_BYTES = 56 * 1024 * 1024

SC_CORES = 2
SC_SUBCORES = 16
SC_WORKERS = SC_CORES * SC_SUBCORES
SC_CHUNK = 64
SC_BUFFERS = 2

NT_DIMS = (((1,), (1,)), ((), ()))


def _params(sem, vmem=VMEM_LIMIT_BYTES):
    return pltpu.CompilerParams(dimension_semantics=sem, vmem_limit_bytes=vmem)


def _sigmoid(z):
    return 1.0 / (1.0 + jnp.exp(-z))


def _norm_mod(x, g, scale, shift):
    r = lax.rsqrt(jnp.mean(x * x, axis=-1, keepdims=True) + EPS)
    return (x * r) * (g * (1.0 + scale)) + shift


def _pack_pairs(y):
    w = y.shape[1] // 2
    hi = lax.bitcast_convert_type(y[:, :w].astype(BF16).astype(F32), U32)
    lo = lax.bitcast_convert_type(y[:, w:].astype(BF16).astype(F32), U32)
    return hi | (lo >> 16)


def _unpack_pairs(w):
    hi = lax.bitcast_convert_type(w & jnp.uint32(0xFFFF0000), F32)
    lo = lax.bitcast_convert_type(w << 16, F32)
    return jnp.concatenate([hi, lo], axis=1)


def _split_bf16(x, parts):
    out = []
    r = x
    for _ in range(parts):
        p = r.astype(BF16)
        out.append(p)
        r = r - p.astype(F32)
    return out


def _ffn_prenorm(xn, gf, sc2, sh2, rwt, rb):
    h2 = _norm_mod(xn, gf, sc2, sh2)
    h_hi, h_lo = _split_bf16(h2, 2)
    w_hi, w_lo = _split_bf16(rwt, 2)
    lg = (lax.dot_general(w_hi, h_hi, NT_DIMS, preferred_element_type=F32)
          + lax.dot_general(w_lo, h_hi, NT_DIMS, preferred_element_type=F32)
          + lax.dot_general(w_hi, h_lo, NT_DIMS, preferred_element_type=F32)) + rb
    return _pack_pairs(h2), lg


def _mod_spec(layer, batch, j):
    def index(b, s):
        return ((layer * batch + b) * 6 + j, 0, 0)
    return index


def _ada_kernel(ct_ref, w_ref, b_ref, o_ref):
    ct = ct_ref[...]
    cond = ct * _sigmoid(ct)
    w = w_ref[0]
    rows = []
    for b in range(ct.shape[1]):
        rows.append(jnp.sum(w * cond[:, b:b + 1], axis=0, keepdims=True))
    o_ref[0] = jnp.concatenate(rows, axis=0) + b_ref[0]


def _ada_call(c, ada_w, ada_b):
    depth, d, n = ada_w.shape
    batch = c.shape[0]
    tn = ADA_COLS
    return pl.pallas_call(
        _ada_kernel,
        grid=(depth, n // tn),
        in_specs=[
            pl.BlockSpec((d, batch), lambda i, j: (0, 0)),
            pl.BlockSpec((1, d, tn), lambda i, j: (i, 0, j)),
            pl.BlockSpec((1, 1, tn), lambda i, j: (i, 0, j)),
        ],
        out_specs=pl.BlockSpec((1, batch, tn), lambda i, j: (i, 0, j)),
        out_shape=jax.ShapeDtypeStruct((depth, batch, n), F32),
        compiler_params=_params(("arbitrary", "arbitrary")),
        name="ada_mod",
    )(c.T, ada_w, ada_b.reshape(depth, 1, n))


def _pool_kernel(x_ref, sh1, sc1, g1, sh2, sc2, gm, gf, pw_ref, ps_ref, rwt_ref, rb_ref,
                 xo_ref, hp_ref, lg_ref, hbuf):
    s = pl.program_id(1)
    ts = x_ref.shape[1]
    d = x_ref.shape[2]
    pg = d // len(POOL_WINDOWS)
    x = x_ref[0]
    h = _norm_mod(x, gm[...], sc1[0], sh1[0])

    @pl.when(s == 0)
    def _():
        hbuf[0:POOL_HALO, :] = jnp.zeros((POOL_HALO, d), F32)

    @pl.when(s > 0)
    def _():
        hbuf[0:POOL_HALO, :] = hbuf[ts:ts + POOL_HALO, :]

    hbuf[POOL_HALO:POOL_HALO + ts, :] = h
    pos = s * ts + lax.broadcasted_iota(I32, (ts, 1), 0)
    ys = []
    for g, w in enumerate(POOL_WINDOWS):
        lo = g * pg
        hg = h[:, lo:lo + pg]
        acc = hg
        for k in range(1, w):
            acc = acc + hbuf[POOL_HALO - k:POOL_HALO - k + ts, lo:lo + pg]
        cnt = jnp.minimum(pos + 1, w).astype(F32)
        dlt = acc / cnt - hg
        ys.append(jnp.dot(dlt.astype(BF16), pw_ref[g].astype(BF16), preferred_element_type=F32))
    y = jnp.concatenate(ys, axis=1) * ps_ref[...]
    xn = x + g1[0] * y
    xo_ref[0] = xn
    hp, lg = _ffn_prenorm(xn, gf[...], sc2[0], sh2[0], rwt_ref[...], rb_ref[...])
    hp_ref[...] = hp
    lg_ref[...] = lg


def _row_layer_specs(layer, batch, ns, ts, d, e):
    row = lambda j: pl.BlockSpec((1, 1, d), _mod_spec(layer, batch, j))
    vec = pl.BlockSpec((1, d), lambda b, s: (0, 0))
    x_spec = pl.BlockSpec((1, ts, d), lambda b, s: (b, s, 0))
    out_specs = [
        x_spec,
        pl.BlockSpec((ts, d // 2), lambda b, s: (b * ns + s, 0)),
        pl.BlockSpec((e, ts), lambda b, s: (0, b * ns + s)),
    ]
    return row, vec, x_spec, out_specs


def _pool_call(layer, x, modr, gm, gf, pw, ps, rwt, rb):
    batch, seq, d = x.shape
    e = rwt.shape[0]
    ts = SEQ_TILE
    ns = seq // ts
    g = len(POOL_WINDOWS)
    pg = d // g
    row, vec, x_spec, out_specs = _row_layer_specs(layer, batch, ns, ts, d, e)
    return pl.pallas_call(
        _pool_kernel,
        grid=(batch, ns),
        in_specs=[
            x_spec, row(0), row(1), row(2), row(3), row(4), vec, vec,
            pl.BlockSpec((g, pg, pg), lambda b, s: (0, 0, 0)),
            vec,
            pl.BlockSpec((e, d), lambda b, s: (0, 0)),
            pl.BlockSpec((e, 1), lambda b, s: (0, 0)),
        ],
        out_specs=out_specs,
        out_shape=[
            jax.ShapeDtypeStruct((batch, seq, d), F32),
            jax.ShapeDtypeStruct((batch * seq, d // 2), U32),
            jax.ShapeDtypeStruct((e, batch * seq), F32),
        ],
        scratch_shapes=[pltpu.VMEM((ts + POOL_HALO, d), F32)],
        compiler_params=_params(("arbitrary", "arbitrary")),
        name="pool_layer",
    )(x, modr, modr, modr, modr, modr, gm, gf, pw, ps, rwt, rb)


def _aug_lane(head):
    return HEAD_DIM if head % 2 == 0 else 0


def _bias_placement(nh):
    place = np.zeros((BIAS_PARTS * nh, nh * LANES), np.float32)
    for j in range(BIAS_PARTS):
        for h in range(nh):
            place[j * nh + h, h * LANES + _aug_lane(h) + j] = 1.0
    return jnp.asarray(place, BF16)


def _qkv_kernel(x_ref, sh1, sc1, gm, w_ref, bf_ref, place_ref, q_ref, k_ref, v_ref, carry):
    s = pl.program_id(1)
    ts = x_ref.shape[1]
    d = x_ref.shape[2]
    nh = bf_ref.shape[1]
    h = _norm_mod(x_ref[0], gm[...], sc1[0], sh1[0]).astype(BF16)
    proj = jnp.dot(h, w_ref[...], preferred_element_type=F32)
    fl = proj[:, 3 * d:3 * d + nh] + bf_ref[...]
    logf = jnp.minimum(fl, 0.0) - jnp.log(1.0 + jnp.exp(-jnp.abs(fl)))

    @pl.when(s == 0)
    def _():
        carry[...] = jnp.zeros(carry.shape, F32)

    r = lax.broadcasted_iota(I32, (ts, ts), 0)
    c = lax.broadcasted_iota(I32, (ts, ts), 1)
    lower = jnp.where(r >= c, 1.0, 0.0).astype(BF16)
    parts = jnp.concatenate(_split_bf16(logf, 3), axis=1)
    cs = jnp.dot(lower, parts, preferred_element_type=F32)
    cum = cs[:, :nh] + cs[:, nh:2 * nh] + cs[:, 2 * nh:] + carry[...]
    carry[...] = cum[ts - 1:ts, :]
    bias = jnp.concatenate(_split_bf16(cum * (-LOG2E), BIAS_PARTS), axis=1)
    k_aug = jnp.dot(bias, place_ref[...], preferred_element_type=F32)

    lane = lax.broadcasted_iota(I32, (ts, LANES), 1)
    q_all = proj[:, :d] * (HEAD_DIM ** -0.5 * LOG2E)
    for hd in range(nh):
        lo = (hd // 2) * LANES
        a0 = _aug_lane(hd)
        real = (lane < HEAD_DIM) if hd % 2 == 0 else (lane >= HEAD_DIM)
        q_aug = jnp.where((lane >= a0) & (lane < a0 + BIAS_PARTS), 1.0, 0.0)
        v_aug = jnp.where(lane == a0, 1.0, 0.0)
        q_ref[0, hd] = jnp.where(real, q_all[:, lo:lo + LANES], q_aug).astype(BF16)
        k_ref[0, hd] = jnp.where(real, proj[:, d + lo:d + lo + LANES],
                                 k_aug[:, hd * LANES:(hd + 1) * LANES]).astype(BF16)
        v_ref[0, hd] = jnp.where(real, proj[:, 2 * d + lo:2 * d + lo + LANES], v_aug).astype(BF16)


def _qkv_call(layer, x, modr, gm, w_in, bf):
    batch, seq, d = x.shape
    nh = bf.shape[1]
    ts = SEQ_TILE
    ns = seq // ts
    w_all = jnp.concatenate([w_in, jnp.zeros((d, LANES - nh), w_in.dtype)], axis=1).astype(BF16)
    place = _bias_placement(nh)
    row = lambda j: pl.BlockSpec((1, 1, d), _mod_spec(layer, batch, j))
    head_spec = pl.BlockSpec((1, nh, ts, LANES), lambda b, s: (b, 0, s, 0))
    return pl.pallas_call(
        _qkv_kernel,
        grid=(batch, ns),
        in_specs=[
            pl.BlockSpec((1, ts, d), lambda b, s: (b, s, 0)), row(0), row(1),
            pl.BlockSpec((1, d), lambda b, s: (0, 0)),
            pl.BlockSpec(w_all.shape, lambda b, s: (0, 0)),
            pl.BlockSpec((1, nh), lambda b, s: (0, 0)),
            pl.BlockSpec(place.shape, lambda b, s: (0, 0)),
        ],
        out_specs=[head_spec] * 3,
        out_shape=[jax.ShapeDtypeStruct((batch, nh, seq, LANES), BF16)] * 3,
        scratch_shapes=[pltpu.VMEM((1, nh), F32)],
        compiler_params=_params(("arbitrary", "arbitrary")),
        name="fox_qkv",
    )(x, modr, modr, gm, w_all, bf, place)


def _attn_kernel(q_ref, k_ref, v_ref, o_ref, m_s, acc_s):
    qi = pl.program_id(2)
    tk = ATTN_TILE
    tq = q_ref.shape[2]
    subs = tq // tk
    m_s[...] = jnp.full(m_s.shape, NEG_BIG, F32)
    acc_s[...] = jnp.zeros(acc_s.shape, F32)
    row = lax.broadcasted_iota(I32, (tk, tk), 0)
    col = lax.broadcasted_iota(I32, (tk, tk), 1)

    def step(kb, first, diagonal):
        start = pl.multiple_of(kb * tk, tk)
        rows = slice(first * tk, tq)
        for hh in range(HEADS_PER_STEP):
            kk = k_ref[0, hh, pl.ds(start, tk), :]
            vv = v_ref[0, hh, pl.ds(start, tk), :]
            z = lax.dot_general(q_ref[0, hh, rows, :], kk, NT_DIMS, preferred_element_type=F32)
            if diagonal:
                masked = jnp.where(row >= col, z[:tk], NEG_BIG)
                z = masked if first == subs - 1 else jnp.concatenate([masked, z[tk:]], axis=0)
            m_prev = m_s[hh, rows, :]
            m_new = jnp.maximum(m_prev, jnp.max(z, axis=1, keepdims=True))
            p = jnp.exp2(z - jnp.concatenate([m_new] * (tk // LANES), axis=1))
            acc_s[hh, rows, :] = (jnp.exp2(m_prev - m_new) * acc_s[hh, rows, :]
                                  + jnp.dot(p.astype(BF16), vv, preferred_element_type=F32))
            m_s[hh, rows, :] = m_new

    def body(kb, carry):
        step(kb, 0, False)
        return carry

    lax.fori_loop(0, subs * qi, body, 0)
    for j in range(subs):
        step(subs * qi + j, j, True)
    lane = lax.broadcasted_iota(I32, (tq, LANES), 1)
    outs = []
    for hh in range(HEADS_PER_STEP):
        a = acc_s[hh]
        outs.append(a / a[:, _aug_lane(hh):_aug_lane(hh) + 1])
    o_ref[0] = jnp.where(lane < HEAD_DIM, outs[0], outs[1]).astype(BF16)


def _attn_call(q, k, v):
    batch, nh, seq, _ = q.shape
    tq = ATTN_QUERY_SUBTILES * ATTN_TILE
    kv_spec = pl.BlockSpec((1, HEADS_PER_STEP, seq, LANES), lambda b, h, i: (b, h, 0, 0))
    return pl.pallas_call(
        _attn_kernel,
        grid=(batch, nh // HEADS_PER_STEP, seq // tq),
        in_specs=[pl.BlockSpec((1, HEADS_PER_STEP, tq, LANES), lambda b, h, i: (b, h, i, 0)),
                  kv_spec, kv_spec],
        out_specs=pl.BlockSpec((1, tq, LANES), lambda b, h, i: (b, i, h)),
        out_shape=jax.ShapeDtypeStruct((batch, seq, nh * HEAD_DIM), BF16),
        scratch_shapes=[pltpu.VMEM((HEADS_PER_STEP, tq, LANES), F32)] * 2,
        compiler_params=_params(("arbitrary", "arbitrary", "arbitrary")),
        name="fox_attention",
    )(q, k, v)


def _wo_kernel(x_ref, o_ref, g1, sh2, sc2, gf, wo_ref, rwt_ref, rb_ref, xo_ref, hp_ref, lg_ref):
    m = jnp.dot(o_ref[0], wo_ref[...], preferred_element_type=F32)
    xn = x_ref[0] + g1[0] * m
    xo_ref[0] = xn
    hp, lg = _ffn_prenorm(xn, gf[...], sc2[0], sh2[0], rwt_ref[...], rb_ref[...])
    hp_ref[...] = hp
    lg_ref[...] = lg


def _wo_call(layer, x, o, modr, gf, wo, rwt, rb):
    batch, seq, d = x.shape
    e = rwt.shape[0]
    ts = SEQ_TILE
    ns = seq // ts
    row, vec, x_spec, out_specs = _row_layer_specs(layer, batch, ns, ts, d, e)
    return pl.pallas_call(
        _wo_kernel,
        grid=(batch, ns),
        in_specs=[
            x_spec, x_spec, row(2), row(3), row(4), vec,
            pl.BlockSpec((d, d), lambda b, s: (0, 0)),
            pl.BlockSpec((e, d), lambda b, s: (0, 0)),
            pl.BlockSpec((e, 1), lambda b, s: (0, 0)),
        ],
        out_specs=out_specs,
        out_shape=[
            jax.ShapeDtypeStruct((batch, seq, d), F32),
            jax.ShapeDtypeStruct((batch * seq, d // 2), U32),
            jax.ShapeDtypeStruct((e, batch * seq), F32),
        ],
        compiler_params=_params(("arbitrary", "arbitrary")),
        name="fox_out",
    )(x, o, modr, modr, modr, gf, wo, rwt, rb)


def _route_kernel(lg_ref, eid_ref, gate_ref, pos_ref, cnt_ref, carry):
    i = pl.program_id(0)

    @pl.when(i == 0)
    def _():
        carry[...] = jnp.zeros(carry.shape, F32)

    l = lg_ref[...]
    e, ts = l.shape
    eidx = lax.broadcasted_iota(I32, (e, ts), 0)
    work = l
    top_v, top_i, hot = [], [], []
    for _ in range(TOP_K):
        m = jnp.max(work, axis=0, keepdims=True)
        sel = jnp.min(jnp.where(work == m, eidx, e), axis=0, keepdims=True)
        o = eidx == sel
        top_v.append(m)
        top_i.append(sel)
        hot.append(o)
        work = jnp.where(o, -jnp.inf, work)
    ex = [jnp.exp(v - top_v[0]) for v in top_v]
    den = ex[0] + ex[1] + ex[2] + ex[3]
    chosen = jnp.where(hot[0] | hot[1] | hot[2] | hot[3], 1.0, 0.0)
    r = lax.broadcasted_iota(I32, (ts, ts), 0)
    c = lax.broadcasted_iota(I32, (ts, ts), 1)
    before = jnp.where(r < c, 1.0, 0.0).astype(BF16)
    rank = jnp.dot(chosen.astype(BF16), before, preferred_element_type=F32) + carry[...]
    pos = [jnp.sum(jnp.where(o, rank, 0.0), axis=0, keepdims=True) for o in hot]
    carry[...] = carry[...] + jnp.sum(chosen, axis=1, keepdims=True)
    eid_ref[...] = jnp.concatenate(top_i, axis=0)
    gate_ref[...] = jnp.concatenate([x / den for x in ex], axis=0)
    pos_ref[...] = jnp.concatenate(pos, axis=0).astype(I32)
    cnt_ref[...] = jnp.broadcast_to(carry[...], cnt_ref.shape)


def _route_call(logits_t):
    e, t = logits_t.shape
    ts = ROUTE_TILE
    out = pl.BlockSpec((TOP_K, ts), lambda i: (0, i))
    return pl.pallas_call(
        _route_kernel,
        grid=(t // ts,),
        in_specs=[pl.BlockSpec((e, ts), lambda i: (0, i))],
        out_specs=[out, out, out, pl.BlockSpec((e, LANES), lambda i: (0, 0))],
        out_shape=[
            jax.ShapeDtypeStruct((TOP_K, t), I32),
            jax.ShapeDtypeStruct((TOP_K, t), F32),
            jax.ShapeDtypeStruct((TOP_K, t), I32),
            jax.ShapeDtypeStruct((e, LANES), F32),
        ],
        scratch_shapes=[pltpu.VMEM((e, 1), F32)],
        compiler_params=_params(("arbitrary",)),
        name="route_topk",
    )(logits_t)


def _dest_kernel(pstart_ref, eid_ref, pos_ref, dest_ref):
    eid = eid_ref[...]
    dest = pos_ref[...]
    for e in range(pstart_ref.shape[0]):
        dest = dest + jnp.where(eid == e, pstart_ref[e], 0)
    dest_ref[...] = dest


def _dest_call(pstart, eid, pos):
    k, t = eid.shape
    tt = min(DEST_TILE, t)
    spec = pl.BlockSpec((k, tt), lambda i, ps: (0, i))
    grid_spec = pltpu.PrefetchScalarGridSpec(
        num_scalar_prefetch=1, grid=(t // tt,), in_specs=[spec, spec], out_specs=spec)
    return pl.pallas_call(
        _dest_kernel,
        grid_spec=grid_spec,
        out_shape=jax.ShapeDtypeStruct((k, t), I32),
        compiler_params=_params(("arbitrary",)),
        name="route_dest",
    )(pstart, eid, pos)


def _sc_worker_chunks(total_chunks):
    worker = lax.axis_index("s") * SC_CORES + lax.axis_index("c")
    per_worker = total_chunks // SC_WORKERS
    return worker * per_worker, per_worker


def _sc_dispatch_call(hp, dest3, rows):
    t, w = hp.shape
    nchunk = dest3.shape[0]
    mesh = plsc.VectorSubcoreMesh(core_axis_name="c", subcore_axis_name="s")

    assert nchunk % (SC_BUFFERS * SC_WORKERS) == 0

    @functools.partial(
        pl.kernel, mesh=mesh, out_type=jax.ShapeDtypeStruct((rows, w), U32),
        scratch_types=[pltpu.VMEM((SC_BUFFERS, TOP_K, SC_CHUNK), I32),
                       pltpu.VMEM((SC_BUFFERS, SC_CHUNK, w), U32),
                       pltpu.SemaphoreType.DMA((SC_BUFFERS,)), pltpu.SemaphoreType.DMA((SC_BUFFERS,))],
        name="moe_dispatch")
    def dispatch(hp_hbm, dest_hbm, xs_hbm, idx_v, rows_v, load_sem, scatter_sem):
        first, count = _sc_worker_chunks(nchunk)

        def load(c, slot):
            return pltpu.make_async_copy(hp_hbm.at[pl.ds(c * SC_CHUNK, SC_CHUNK)], rows_v.at[slot],
                                         load_sem.at[slot])

        def scatter(slot, k):
            return pltpu.make_async_copy(rows_v.at[slot], xs_hbm.at[idx_v.at[slot, k]],
                                         scatter_sem.at[slot])

        def start_load(c, slot):
            pltpu.sync_copy(dest_hbm.at[c], idx_v.at[slot])
            load(c, slot).start()

        start_load(first, 0)

        @pl.loop(0, count, step=SC_BUFFERS)
        def _(j):
            for b in range(SC_BUFFERS):
                c = first + j + b
                nxt = (b + 1) % SC_BUFFERS

                @pl.when(j + b + 1 < count)
                def _():
                    @pl.when(j + b >= 1)
                    def _():
                        for k in range(TOP_K):
                            scatter(nxt, k).wait()
                    start_load(c + 1, nxt)

                load(c, b).wait()
                for k in range(TOP_K):
                    scatter(b, k).start()

        for slot in range(SC_BUFFERS):
            for k in range(TOP_K):
                scatter(slot, k).wait()

    return dispatch(hp, dest3)


def _sc_gather_call(ys, idx):
    n = idx.shape[0]
    _, w = ys.shape
    nchunk = n // SC_CHUNK
    mesh = plsc.VectorSubcoreMesh(core_axis_name="c", subcore_axis_name="s")

    assert nchunk % (SC_BUFFERS * SC_WORKERS) == 0

    @functools.partial(
        pl.kernel, mesh=mesh, out_type=jax.ShapeDtypeStruct((n, w), U32),
        scratch_types=[pltpu.VMEM((SC_BUFFERS, SC_CHUNK), I32), pltpu.VMEM((SC_BUFFERS, SC_CHUNK, w), U32),
                       pltpu.SemaphoreType.DMA((SC_BUFFERS,)), pltpu.SemaphoreType.DMA((SC_BUFFERS,))],
        name="moe_gather")
    def gather(ys_hbm, idx_hbm, out_hbm, idx_v, rows_v, gather_sem, write_sem):
        first, count = _sc_worker_chunks(nchunk)

        def fetch(slot):
            return pltpu.make_async_copy(ys_hbm.at[idx_v.at[slot]], rows_v.at[slot], gather_sem.at[slot])

        def write(c, slot):
            return pltpu.make_async_copy(rows_v.at[slot], out_hbm.at[pl.ds(c * SC_CHUNK, SC_CHUNK)],
                                         write_sem.at[slot])

        def start_fetch(c, slot):
            pltpu.sync_copy(idx_hbm.at[pl.ds(c * SC_CHUNK, SC_CHUNK)], idx_v.at[slot])
            fetch(slot).start()

        start_fetch(first, 0)

        @pl.loop(0, count, step=SC_BUFFERS)
        def _(j):
            for b in range(SC_BUFFERS):
                c = first + j + b
                nxt = (b + 1) % SC_BUFFERS

                @pl.when(j + b + 1 < count)
                def _():
                    @pl.when(j + b >= 1)
                    def _():
                        write(c - 1, nxt).wait()
                    start_fetch(c + 1, nxt)

                fetch(b).wait()
                write(c, b).start()

        for b in range(SC_BUFFERS):
            write(first + count - SC_BUFFERS + b, b).wait()

    return gather(ys, idx)


def _expert_kernel(be_ref, nv_ref, xs_ref, w1_ref, b1_ref, w2_ref, b2_ref, ys_ref, w1s, w2s):
    b = pl.program_id(0)
    e = be_ref[b]
    prev = be_ref[jnp.maximum(b - 1, 0)]
    nvalid = nv_ref[b]
    f = w2_ref.shape[1]

    @pl.when((b == 0) | (e != prev))
    def _():
        w1s[...] = w1_ref[0].astype(BF16)
        w2s[...] = w2_ref[0].astype(BF16)

    @pl.when(nvalid > 0)
    def _():
        rows = lax.broadcasted_iota(I32, (xs_ref.shape[0], 1), 0)
        xw = jnp.where(rows < nvalid, xs_ref[...], jnp.uint32(0))
        x = _unpack_pairs(xw).astype(BF16)
        gu = jnp.dot(x, w1s[...], preferred_element_type=F32) + b1_ref[0]
        gate = jnp.minimum(gu[:, :f], SWIGLU_LIMIT)
        up = jnp.clip(gu[:, f:], -SWIGLU_LIMIT, SWIGLU_LIMIT)
        act = (up + 1.0) * (gate * _sigmoid(SWIGLU_ALPHA * gate))
        y = jnp.dot(act.astype(BF16), w2s[...], preferred_element_type=F32) + b2_ref[0]
        ys_ref[...] = _pack_pairs(y)

    @pl.when(nvalid <= 0)
    def _():
        ys_ref[...] = jnp.zeros(ys_ref.shape, U32)


def _expert_call(layer, block_e, nvalid, xs, w_in, b_in, w_out, b_out):
    rows, w = xs.shape
    depth, ne, d, f2 = w_in.shape
    f = f2 // 2
    bm = EXPERT_ROWS
    x_spec = pl.BlockSpec((bm, w), lambda b, be, nv: (b, 0))
    grid_spec = pltpu.PrefetchScalarGridSpec(
        num_scalar_prefetch=2,
        grid=(rows // bm,),
        in_specs=[
            x_spec,
            pl.BlockSpec((1, d, f2), lambda b, be, nv: (layer * ne + be[b], 0, 0)),
            pl.BlockSpec((1, 1, f2), lambda b, be, nv: (layer * ne + be[b], 0, 0)),
            pl.BlockSpec((1, f, d), lambda b, be, nv: (layer * ne + be[b], 0, 0)),
            pl.BlockSpec((1, 1, d), lambda b, be, nv: (layer * ne + be[b], 0, 0)),
        ],
        out_specs=x_spec,
        scratch_shapes=[pltpu.VMEM((d, f2), BF16), pltpu.VMEM((f, d), BF16)],
    )
    return pl.pallas_call(
        _expert_kernel,
        grid_spec=grid_spec,
        out_shape=jax.ShapeDtypeStruct((rows, w), U32),
        compiler_params=_params(("arbitrary",)),
        name="moe_experts",
    )(block_e, nvalid, xs,
      w_in.reshape(depth * ne, d, f2), b_in.reshape(depth * ne, 1, f2),
      w_out.reshape(depth * ne, f, d), b_out.reshape(depth * ne, 1, d))


def _combine_kernel(final, x_ref, yg_ref, gt_ref, g2, fg, xo_ref):
    gt = gt_ref[...]
    moe = gt[:, 0:1] * _unpack_pairs(yg_ref[0])
    for k in range(1, TOP_K):
        moe = moe + gt[:, k:k + 1] * _unpack_pairs(yg_ref[k])
    xn = x_ref[0] + g2[0] * moe
    if final:
        r = lax.rsqrt(jnp.mean(xn * xn, axis=-1, keepdims=True) + EPS)
        xn = (xn * r) * fg[...]
    xo_ref[0] = xn


def _combine_call(layer, final, x, yg, gate_t, modr, fg):
    batch, seq, d = x.shape
    w = yg.shape[2]
    ts = COMBINE_TILE
    ns = seq // ts
    x_spec = pl.BlockSpec((1, ts, d), lambda b, s: (b, s, 0))
    return pl.pallas_call(
        functools.partial(_combine_kernel, final),
        grid=(batch, ns),
        in_specs=[
            x_spec,
            pl.BlockSpec((TOP_K, ts, w), lambda b, s: (0, b * ns + s, 0)),
            pl.BlockSpec((ts, TOP_K), lambda b, s: (b * ns + s, 0)),
            pl.BlockSpec((1, 1, d), _mod_spec(layer, batch, 5)),
            pl.BlockSpec((1, d), lambda b, s: (0, 0)),
        ],
        out_specs=x_spec,
        out_shape=jax.ShapeDtypeStruct((batch, seq, d), F32),
        compiler_params=_params(("arbitrary", "arbitrary")),
        name="moe_combine",
    )(x, yg, gate_t, modr, fg)


def _moe(layer, final, x, hp, logits_t, modr, fg, w_in, b_in, w_out, b_out):
    t, w = hp.shape
    ne = logits_t.shape[0]
    bm = EXPERT_ROWS
    eid, gate, pos, cnt = _route_call(logits_t)
    counts = cnt[:, 0].astype(I32)
    nblk = (counts + bm - 1) // bm
    bend = jnp.cumsum(nblk)
    bstart = bend - nblk
    pstart = (bstart * bm).astype(I32)
    n_blocks = (t * TOP_K) // bm + ne
    blocks = jnp.arange(n_blocks, dtype=I32)
    block_e = jnp.minimum(jnp.sum(blocks[:, None] >= bend[None, :], axis=1), ne - 1).astype(I32)
    mine = block_e[:, None] == jnp.arange(ne, dtype=I32)[None, :]
    left = jnp.sum(jnp.where(mine, counts[None, :] - (blocks[:, None] - bstart[None, :]) * bm, 0), axis=1)
    nvalid = jnp.where(blocks < bend[-1], jnp.clip(left, 0, bm), 0).astype(I32)
    dest = _dest_call(pstart, eid, pos)
    dest3 = dest.reshape(TOP_K, t // SC_CHUNK, SC_CHUNK).transpose(1, 0, 2)
    xs = _sc_dispatch_call(hp, dest3, n_blocks * bm)
    ys = _expert_call(layer, block_e, nvalid, xs, w_in, b_in, w_out, b_out)
    yg = _sc_gather_call(ys, dest.reshape(TOP_K * t))
    return _combine_call(layer, final, x, yg.reshape(TOP_K, t, w), gate.T, modr, fg)


def kernel(x, c, norm_mix_g, norm_ffn_g, ada_w, ada_b, pool_w, pool_scale, fox_w_in, fox_b_f, fox_w_o,
           router_w, router_b, exp_w_in, exp_b_in, exp_w_out, exp_b_out, final_g):
    batch, seq, d = x.shape
    depth = ada_w.shape[0]
    mod = _ada_call(c, ada_w, ada_b)
    modr = mod.reshape(depth * batch * 6, 1, d)
    fg = final_g.reshape(1, d)
    for i in range(depth):
        gm = norm_mix_g[i].reshape(1, d)
        gf = norm_ffn_g[i].reshape(1, d)
        rwt = router_w[i].T
        rb = router_b[i].reshape(-1, 1)
        j = i // 2
        if i % 2 == 0:
            x, hp, lg = _pool_call(i, x, modr, gm, gf, pool_w[j], pool_scale[j].reshape(1, d), rwt, rb)
        else:
            q, k, v = _qkv_call(i, x, modr, gm, fox_w_in[j], fox_b_f[j].reshape(1, -1))
            o = _attn_call(q, k, v)
            x, hp, lg = _wo_call(i, x, o, modr, gf, fox_w_o[j].astype(BF16), rwt, rb)
        x = _moe(i, i == depth - 1, x, hp, lg, modr, fg, exp_w_in, exp_b_in, exp_w_out, exp_b_out)
    return x
```

```python
import functools

import jax
import jax.numpy as jnp
import numpy as np
from jax import lax
from jax.experimental import pallas as pl
from jax.experimental.pallas import tpu as pltpu
from jax.experimental.pallas import tpu_sc as plsc

F32 = jnp.float32
BF16 = jnp.bfloat16
U32 = jnp.uint32
I32 = jnp.int32

POOL_WINDOWS = (2, 4, 8, 16)
POOL_HALO = 16
HEAD_DIM = 64
HEADS_PER_STEP = 2
TOP_K = 4
SWIGLU_LIMIT = 7.0
SWIGLU_ALPHA = 1.702
EPS = 1e-6
NEG_BIG = -1e30
LOG2E = 1.4426950408889634
BIAS_PARTS = 3

LANES = 128
SEQ_TILE = 512
ATTN_TILE = 512
ATTN_QUERY_SUBTILES = 4
ROUTE_TILE = 512
DEST_TILE = 4096
EXPERT_ROWS = 512
COMBINE_TILE = 512
ADA_COLS = 1536
VMEM_LIMIT_BYTES = 56 * 1024 * 1024

SC_CORES = 2
SC_SUBCORES = 16
SC_WORKERS = SC_CORES * SC_SUBCORES
SC_CHUNK = 64
SC_BUFFERS = 2

NT_DIMS = (((1,), (1,)), ((), ()))


def _params(sem, vmem=VMEM_LIMIT_BYTES):
    return pltpu.CompilerParams(dimension_semantics=sem, vmem_limit_bytes=vmem)


def _sigmoid(z):
    return 1.0 / (1.0 + jnp.exp(-z))


def _norm_mod(x, g, scale, shift):
    r = lax.rsqrt(jnp.mean(x * x, axis=-1, keepdims=True) + EPS)
    return (x * r) * (g * (1.0 + scale)) + shift


def _pack_pairs(y):
    w = y.shape[1] // 2
    hi = lax.bitcast_convert_type(y[:, :w].astype(BF16).astype(F32), U32)
    lo = lax.bitcast_convert_type(y[:, w:].astype(BF16).astype(F32), U32)
    return hi | (lo >> 16)


def _unpack_pairs(w):
    hi = lax.bitcast_convert_type(w & jnp.uint32(0xFFFF0000), F32)
    lo = lax.bitcast_convert_type(w << 16, F32)
    return jnp.concatenate([hi, lo], axis=1)


def _split_bf16(x, parts):
    out = []
    r = x
    for _ in range(parts):
        p = r.astype(BF16)
        out.append(p)
        r = r - p.astype(F32)
    return out


def _ffn_prenorm(xn, gf, sc2, sh2, rwt, rb):
    h2 = _norm_mod(xn, gf, sc2, sh2)
    h_hi, h_lo = _split_bf16(h2, 2)
    w_hi, w_lo = _split_bf16(rwt, 2)
    lg = (lax.dot_general(w_hi, h_hi, NT_DIMS, preferred_element_type=F32)
          + lax.dot_general(w_lo, h_hi, NT_DIMS, preferred_element_type=F32)
          + lax.dot_general(w_hi, h_lo, NT_DIMS, preferred_element_type=F32)) + rb
    return _pack_pairs(h2), lg


def _mod_spec(layer, batch, j):
    def index(b, s):
        return ((layer * batch + b) * 6 + j, 0, 0)
    return index


def _moe_residual(x_ref, yg_ref, gt_ref, g2_ref):
    gt = gt_ref[...]
    moe = gt[:, 0:1] * _unpack_pairs(yg_ref[0])
    for k in range(1, TOP_K):
        moe = moe + gt[:, k:k + 1] * _unpack_pairs(yg_ref[k])
    return x_ref[0] + g2_ref[0] * moe


def _residual_in(refs, pending):
    if pending:
        return _moe_residual(*refs[:4]), refs[4:]
    return refs[0][0], refs[1:]


def _residual_specs(pending, batch, ns, ts, d):
    specs = [pl.BlockSpec((1, ts, d), lambda b, s: (b, s, 0))]
    if pending:
        _, t, w = pending["yg"].shape
        specs += [
            pl.BlockSpec((TOP_K, ts, w), lambda b, s: (0, b * ns + s, 0)),
            pl.BlockSpec((ts, TOP_K), lambda b, s: (b * ns + s, 0)),
            pl.BlockSpec((1, 1, d), _mod_spec(pending["layer"], batch, 5)),
        ]
    return specs


def _residual_args(x, pending, modr):
    return (x, pending["yg"], pending["gate_t"], modr) if pending else (x,)


def _ada_kernel(ct_ref, w_ref, b_ref, o_ref):
    ct = ct_ref[...]
    cond = ct * _sigmoid(ct)
    w = w_ref[0]
    rows = []
    for b in range(ct.shape[1]):
        rows.append(jnp.sum(w * cond[:, b:b + 1], axis=0, keepdims=True))
    o_ref[0] = jnp.concatenate(rows, axis=0) + b_ref[0]


def _ada_call(c, ada_w, ada_b):
    depth, d, n = ada_w.shape
    batch = c.shape[0]
    tn = ADA_COLS
    return pl.pallas_call(
        _ada_kernel,
        grid=(depth, n // tn),
        in_specs=[
            pl.BlockSpec((d, batch), lambda i, j: (0, 0)),
            pl.BlockSpec((1, d, tn), lambda i, j: (i, 0, j)),
            pl.BlockSpec((1, 1, tn), lambda i, j: (i, 0, j)),
        ],
        out_specs=pl.BlockSpec((1, batch, tn), lambda i, j: (i, 0, j)),
        out_shape=jax.ShapeDtypeStruct((depth, batch, n), F32),
        compiler_params=_params(("arbitrary", "arbitrary")),
        name="ada_mod",
    )(c.T, ada_w, ada_b.reshape(depth, 1, n))


def _pool_kernel(pending, *refs):
    x, refs = _residual_in(refs, pending)
    sh1, sc1, g1, sh2, sc2, gm, gf, pw_ref, ps_ref, rwt_ref, rb_ref, xo_ref, hp_ref, lg_ref, hbuf = refs
    s = pl.program_id(1)
    ts, d = x.shape
    pg = d // len(POOL_WINDOWS)
    h = _norm_mod(x, gm[...], sc1[0], sh1[0])

    @pl.when(s == 0)
    def _():
        hbuf[0:POOL_HALO, :] = jnp.zeros((POOL_HALO, d), F32)

    @pl.when(s > 0)
    def _():
        hbuf[0:POOL_HALO, :] = hbuf[ts:ts + POOL_HALO, :]

    hbuf[POOL_HALO:POOL_HALO + ts, :] = h
    pos = s * ts + lax.broadcasted_iota(I32, (ts, 1), 0)
    ys = []
    for g, w in enumerate(POOL_WINDOWS):
        lo = g * pg
        hg = h[:, lo:lo + pg]
        win = hbuf[:, lo:lo + pg]
        span = 1
        while span < w:
            win = win + pltpu.roll(win, span, 0)
            span *= 2
        acc = win[POOL_HALO:, :]
        cnt = jnp.minimum(pos + 1, w).astype(F32)
        dlt = acc / cnt - hg
        ys.append(jnp.dot(dlt.astype(BF16), pw_ref[g].astype(BF16), preferred_element_type=F32))
    y = jnp.concatenate(ys, axis=1) * ps_ref[...]
    xn = x + g1[0] * y
    xo_ref[0] = xn
    hp, lg = _ffn_prenorm(xn, gf[...], sc2[0], sh2[0], rwt_ref[...], rb_ref[...])
    hp_ref[...] = hp
    lg_ref[...] = lg


def _row_layer_specs(layer, batch, ns, ts, d, e):
    row = lambda j: pl.BlockSpec((1, 1, d), _mod_spec(layer, batch, j))
    vec = pl.BlockSpec((1, d), lambda b, s: (0, 0))
    x_spec = pl.BlockSpec((1, ts, d), lambda b, s: (b, s, 0))
    out_specs = [
        x_spec,
        pl.BlockSpec((ts, d // 2), lambda b, s: (b * ns + s, 0)),
        pl.BlockSpec((e, ts), lambda b, s: (0, b * ns + s)),
    ]
    return row, vec, x_spec, out_specs


def _pool_call(layer, x, pending, modr, gm, gf, pw, ps, rwt, rb):
    batch, seq, d = x.shape
    e = rwt.shape[0]
    ts = SEQ_TILE
    ns = seq // ts
    g = len(POOL_WINDOWS)
    pg = d // g
    row, vec, _, out_specs = _row_layer_specs(layer, batch, ns, ts, d, e)
    return pl.pallas_call(
        functools.partial(_pool_kernel, bool(pending)),
        grid=(batch, ns),
        in_specs=_residual_specs(pending, batch, ns, ts, d) + [
            row(0), row(1), row(2), row(3), row(4), vec, vec,
            pl.BlockSpec((g, pg, pg), lambda b, s: (0, 0, 0)),
            vec,
            pl.BlockSpec((e, d), lambda b, s: (0, 0)),
            pl.BlockSpec((e, 1), lambda b, s: (0, 0)),
        ],
        out_specs=out_specs,
        out_shape=[
            jax.ShapeDtypeStruct((batch, seq, d), F32),
            jax.ShapeDtypeStruct((batch * seq, d // 2), U32),
            jax.ShapeDtypeStruct((e, batch * seq), F32),
        ],
        scratch_shapes=[pltpu.VMEM((ts + POOL_HALO, d), F32)],
        compiler_params=_params(("arbitrary", "arbitrary")),
        name="pool_layer",
    )(*_residual_args(x, pending, modr), modr, modr, modr, modr, modr, gm, gf, pw, ps, rwt, rb)


def _aug_lane(head):
    return HEAD_DIM if head % 2 == 0 else 0


def _bias_placement(nh):
    place = np.zeros((BIAS_PARTS * nh, nh * LANES), np.float32)
    for j in range(BIAS_PARTS):
        for h in range(nh):
            place[j * nh + h, h * LANES + _aug_lane(h) + j] = 1.0
    return jnp.asarray(place, BF16)


def _qkv_kernel(pending, *refs):
    x, refs = _residual_in(refs, pending)
    sh1, sc1, gm, w_ref, bf_ref, place_ref, q_ref, k_ref, v_ref = refs[:9]
    carry = refs[-1]
    if pending:
        refs[9][0] = x
    s = pl.program_id(1)
    ts, d = x.shape
    nh = bf_ref.shape[1]
    h = _norm_mod(x, gm[...], sc1[0], sh1[0]).astype(BF16)
    proj = jnp.dot(h, w_ref[...], preferred_element_type=F32)
    fl = proj[:, 3 * d:3 * d + nh] + bf_ref[...]
    logf = jnp.minimum(fl, 0.0) - jnp.log(1.0 + jnp.exp(-jnp.abs(fl)))

    @pl.when(s == 0)
    def _():
        carry[...] = jnp.zeros(carry.shape, F32)

    r = lax.broadcasted_iota(I32, (ts, ts), 0)
    c = lax.broadcasted_iota(I32, (ts, ts), 1)
    lower = jnp.where(r >= c, 1.0, 0.0).astype(BF16)
    parts = jnp.concatenate(_split_bf16(logf, 3), axis=1)
    cs = jnp.dot(lower, parts, preferred_element_type=F32)
    cum = cs[:, :nh] + cs[:, nh:2 * nh] + cs[:, 2 * nh:] + carry[...]
    carry[...] = cum[ts - 1:ts, :]
    bias = jnp.concatenate(_split_bf16(cum * (-LOG2E), BIAS_PARTS), axis=1)
    k_aug = jnp.dot(bias, place_ref[...], preferred_element_type=F32)

    lane = lax.broadcasted_iota(I32, (ts, LANES), 1)
    q_all = proj[:, :d] * (HEAD_DIM ** -0.5 * LOG2E)
    for hd in range(nh):
        lo = (hd // 2) * LANES
        a0 = _aug_lane(hd)
        real = (lane < HEAD_DIM) if hd % 2 == 0 else (lane >= HEAD_DIM)
        q_aug = jnp.where((lane >= a0) & (lane < a0 + BIAS_PARTS), 1.0, 0.0)
        v_aug = jnp.where(lane == a0, 1.0, 0.0)
        q_ref[0, hd] = jnp.where(real, q_all[:, lo:lo + LANES], q_aug).astype(BF16)
        k_ref[0, hd] = jnp.where(real, proj[:, d + lo:d + lo + LANES],
                                 k_aug[:, hd * LANES:(hd + 1) * LANES]).astype(BF16)
        v_ref[0, hd] = jnp.where(real, proj[:, 2 * d + lo:2 * d + lo + LANES], v_aug).astype(BF16)


def _qkv_call(layer, x, pending, modr, gm, w_in, bf):
    batch, seq, d = x.shape
    nh = bf.shape[1]
    ts = SEQ_TILE
    ns = seq // ts
    w_all = jnp.concatenate([w_in, jnp.zeros((d, LANES - nh), w_in.dtype)], axis=1).astype(BF16)
    place = _bias_placement(nh)
    row = lambda j: pl.BlockSpec((1, 1, d), _mod_spec(layer, batch, j))
    head_spec = pl.BlockSpec((1, nh, ts, LANES), lambda b, s: (b, 0, s, 0))
    out_specs = [head_spec] * 3
    out_shape = [jax.ShapeDtypeStruct((batch, nh, seq, LANES), BF16)] * 3
    if pending:
        out_specs = out_specs + [pl.BlockSpec((1, ts, d), lambda b, s: (b, s, 0))]
        out_shape = out_shape + [jax.ShapeDtypeStruct((batch, seq, d), F32)]
    outs = pl.pallas_call(
        functools.partial(_qkv_kernel, bool(pending)),
        grid=(batch, ns),
        in_specs=_residual_specs(pending, batch, ns, ts, d) + [
            row(0), row(1),
            pl.BlockSpec((1, d), lambda b, s: (0, 0)),
            pl.BlockSpec(w_all.shape, lambda b, s: (0, 0)),
            pl.BlockSpec((1, nh), lambda b, s: (0, 0)),
            pl.BlockSpec(place.shape, lambda b, s: (0, 0)),
        ],
        out_specs=out_specs,
        out_shape=out_shape,
        scratch_shapes=[pltpu.VMEM((1, nh), F32)],
        compiler_params=_params(("arbitrary", "arbitrary")),
        name="fox_qkv",
    )(*_residual_args(x, pending, modr), modr, modr, gm, w_all, bf, place)
    return outs[0], outs[1], outs[2], (outs[3] if pending else x)


def _attn_kernel(q_ref, k_ref, v_ref, o_ref, m_s, acc_s):
    qi = pl.program_id(2)
    tk = ATTN_TILE
    tq = q_ref.shape[2]
    subs = tq // tk
    m_s[...] = jnp.full(m_s.shape, NEG_BIG, F32)
    acc_s[...] = jnp.zeros(acc_s.shape, F32)
    row = lax.broadcasted_iota(I32, (tk, tk), 0)
    col = lax.broadcasted_iota(I32, (tk, tk), 1)

    def step(kb, first, diagonal):
        start = pl.multiple_of(kb * tk, tk)
        rows = slice(first * tk, tq)
        for hh in range(HEADS_PER_STEP):
            kk = k_ref[0, hh, pl.ds(start, tk), :]
            vv = v_ref[0, hh, pl.ds(start, tk), :]
            z = lax.dot_general(q_ref[0, hh, rows, :], kk, NT_DIMS, preferred_element_type=F32)
            if diagonal:
                masked = jnp.where(row >= col, z[:tk], NEG_BIG)
                z = masked if first == subs - 1 else jnp.concatenate([masked, z[tk:]], axis=0)
            m_prev = m_s[hh, rows, :]
            m_new = jnp.maximum(m_prev, jnp.max(z, axis=1, keepdims=True))
            p = jnp.exp2(z - jnp.concatenate([m_new] * (tk // LANES), axis=1))
            acc_s[hh, rows, :] = (jnp.exp2(m_prev - m_new) * acc_s[hh, rows, :]
                                  + jnp.dot(p.astype(BF16), vv, preferred_element_type=F32))
            m_s[hh, rows, :] = m_new

    def body(kb2, carry):
        step(2 * kb2, 0, False)
        step(2 * kb2 + 1, 0, False)
        return carry

    lax.fori_loop(0, (subs // 2) * qi, body, 0)
    for j in range(subs):
        step(subs * qi + j, j, True)
    lane = lax.broadcasted_iota(I32, (tq, LANES), 1)
    outs = []
    for hh in range(HEADS_PER_STEP):
        a = acc_s[hh]
        outs.append(a / a[:, _aug_lane(hh):_aug_lane(hh) + 1])
    o_ref[0] = jnp.where(lane < HEAD_DIM, outs[0], outs[1]).astype(BF16)


def _attn_call(q, k, v):
    batch, nh, seq, _ = q.shape
    tq = ATTN_QUERY_SUBTILES * ATTN_TILE
    kv_spec = pl.BlockSpec((1, HEADS_PER_STEP, seq, LANES), lambda b, h, i: (b, h, 0, 0))
    return pl.pallas_call(
        _attn_kernel,
        grid=(batch, nh // HEADS_PER_STEP, seq // tq),
        in_specs=[pl.BlockSpec((1, HEADS_PER_STEP, tq, LANES), lambda b, h, i: (b, h, i, 0)),
                  kv_spec, kv_spec],
        out_specs=pl.BlockSpec((1, tq, LANES), lambda b, h, i: (b, i, h)),
        out_shape=jax.ShapeDtypeStruct((batch, seq, nh * HEAD_DIM), BF16),
        scratch_shapes=[pltpu.VMEM((HEADS_PER_STEP, tq, LANES), F32)] * 2,
        compiler_params=_params(("arbitrary", "arbitrary", "arbitrary")),
        name="fox_attention",
    )(q, k, v)


def _wo_kernel(x_ref, o_ref, g1, sh2, sc2, gf, wo_ref, rwt_ref, rb_ref, xo_ref, hp_ref, lg_ref):
    m = jnp.dot(o_ref[0], wo_ref[...], preferred_element_type=F32)
    xn = x_ref[0] + g1[0] * m
    xo_ref[0] = xn
    hp, lg = _ffn_prenorm(xn, gf[...], sc2[0], sh2[0], rwt_ref[...], rb_ref[...])
    hp_ref[...] = hp
    lg_ref[...] = lg


def _wo_call(layer, x, o, modr, gf, wo, rwt, rb):
    batch, seq, d = x.shape
    e = rwt.shape[0]
    ts = SEQ_TILE
    ns = seq // ts
    row, vec, x_spec, out_specs = _row_layer_specs(layer, batch, ns, ts, d, e)
    return pl.pallas_call(
        _wo_kernel,
        grid=(batch, ns),
        in_specs=[
            x_spec, x_spec, row(2), row(3), row(4), vec,
            pl.BlockSpec((d, d), lambda b, s: (0, 0)),
            pl.BlockSpec((e, d), lambda b, s: (0, 0)),
            pl.BlockSpec((e, 1), lambda b, s: (0, 0)),
        ],
        out_specs=out_specs,
        out_shape=[
            jax.ShapeDtypeStruct((batch, seq, d), F32),
            jax.ShapeDtypeStruct((batch * seq, d // 2), U32),
            jax.ShapeDtypeStruct((e, batch * seq), F32),
        ],
        compiler_params=_params(("arbitrary", "arbitrary")),
        name="fox_out",
    )(x, o, modr, modr, modr, gf, wo, rwt, rb)


def _route_kernel(lg_ref, eid_ref, gate_ref, pos_ref, cnt_ref, carry):
    i = pl.program_id(0)

    @pl.when(i == 0)
    def _():
        carry[...] = jnp.zeros(carry.shape, F32)

    l = lg_ref[...]
    e, ts = l.shape
    eidx = lax.broadcasted_iota(I32, (e, ts), 0)
    work = l
    top_v, top_i, hot = [], [], []
    for _ in range(TOP_K):
        m = jnp.max(work, axis=0, keepdims=True)
        sel = jnp.min(jnp.where(work == m, eidx, e), axis=0, keepdims=True)
        o = eidx == sel
        top_v.append(m)
        top_i.append(sel)
        hot.append(o)
        work = jnp.where(o, -jnp.inf, work)
    ex = [jnp.exp(v - top_v[0]) for v in top_v]
    den = ex[0] + ex[1] + ex[2] + ex[3]
    chosen = jnp.where(hot[0] | hot[1] | hot[2] | hot[3], 1.0, 0.0)
    r = lax.broadcasted_iota(I32, (ts, ts), 0)
    c = lax.broadcasted_iota(I32, (ts, ts), 1)
    before = jnp.where(r < c, 1.0, 0.0).astype(BF16)
    rank = jnp.dot(chosen.astype(BF16), before, preferred_element_type=F32) + carry[...]
    pos = [jnp.sum(jnp.where(o, rank, 0.0), axis=0, keepdims=True) for o in hot]
    carry[...] = carry[...] + jnp.sum(chosen, axis=1, keepdims=True)
    eid_ref[...] = jnp.concatenate(top_i, axis=0)
    gate_ref[...] = jnp.concatenate([x / den for x in ex], axis=0)
    pos_ref[...] = jnp.concatenate(pos, axis=0).astype(I32)
    cnt_ref[...] = jnp.broadcast_to(carry[...], cnt_ref.shape)


def _route_call(logits_t):
    e, t = logits_t.shape
    ts = ROUTE_TILE
    out = pl.BlockSpec((TOP_K, ts), lambda i: (0, i))
    return pl.pallas_call(
        _route_kernel,
        grid=(t // ts,),
        in_specs=[pl.BlockSpec((e, ts), lambda i: (0, i))],
        out_specs=[out, out, out, pl.BlockSpec((e, LANES), lambda i: (0, 0))],
        out_shape=[
            jax.ShapeDtypeStruct((TOP_K, t), I32),
            jax.ShapeDtypeStruct((TOP_K, t), F32),
            jax.ShapeDtypeStruct((TOP_K, t), I32),
            jax.ShapeDtypeStruct((e, LANES), F32),
        ],
        scratch_shapes=[pltpu.VMEM((e, 1), F32)],
        compiler_params=_params(("arbitrary",)),
        name="route_topk",
    )(logits_t)


def _dest_kernel(pstart_ref, eid_ref, pos_ref, dest_ref):
    eid = eid_ref[...]
    dest = pos_ref[...]
    for e in range(pstart_ref.shape[0]):
        dest = dest + jnp.where(eid == e, pstart_ref[e], 0)
    dest_ref[...] = dest


def _dest_call(pstart, eid, pos):
    k, t = eid.shape
    tt = min(DEST_TILE, t)
    spec = pl.BlockSpec((k, tt), lambda i, ps: (0, i))
    grid_spec = pltpu.PrefetchScalarGridSpec(
        num_scalar_prefetch=1, grid=(t // tt,), in_specs=[spec, spec], out_specs=spec)
    return pl.pallas_call(
        _dest_kernel,
        grid_spec=grid_spec,
        out_shape=jax.ShapeDtypeStruct((k, t), I32),
        compiler_params=_params(("arbitrary",)),
        name="route_dest",
    )(pstart, eid, pos)


def _sc_worker_chunks(total_chunks):
    worker = lax.axis_index("s") * SC_CORES + lax.axis_index("c")
    per_worker = total_chunks // SC_WORKERS
    return worker * per_worker, per_worker


def _sc_dispatch_call(hp, dest3, rows):
    t, w = hp.shape
    nchunk = dest3.shape[0]
    mesh = plsc.VectorSubcoreMesh(core_axis_name="c", subcore_axis_name="s")

    assert nchunk % (SC_BUFFERS * SC_WORKERS) == 0

    @functools.partial(
        pl.kernel, mesh=mesh, out_type=jax.ShapeDtypeStruct((rows, w), U32),
        scratch_types=[pltpu.VMEM((SC_BUFFERS, TOP_K, SC_CHUNK), I32),
                       pltpu.VMEM((SC_BUFFERS, SC_CHUNK, w), U32),
                       pltpu.SemaphoreType.DMA((SC_BUFFERS,)), pltpu.SemaphoreType.DMA((SC_BUFFERS,))],
        name="moe_dispatch")
    def dispatch(hp_hbm, dest_hbm, xs_hbm, idx_v, rows_v, load_sem, scatter_sem):
        first, count = _sc_worker_chunks(nchunk)

        def load(c, slot):
            return pltpu.make_async_copy(hp_hbm.at[pl.ds(c * SC_CHUNK, SC_CHUNK)], rows_v.at[slot],
                                         load_sem.at[slot])

        def scatter(slot, k):
            return pltpu.make_async_copy(rows_v.at[slot], xs_hbm.at[idx_v.at[slot, k]],
                                         scatter_sem.at[slot])

        def start_load(c, slot):
            pltpu.sync_copy(dest_hbm.at[c], idx_v.at[slot])
            load(c, slot).start()

        start_load(first, 0)

        @pl.loop(0, count, step=SC_BUFFERS)
        def _(j):
            for b in range(SC_BUFFERS):
                c = first + j + b
                nxt = (b + 1) % SC_BUFFERS

                @pl.when(j + b + 1 < count)
                def _():
                    @pl.when(j + b >= 1)
                    def _():
                        for k in range(TOP_K):
                            scatter(nxt, k).wait()
                    start_load(c + 1, nxt)

                load(c, b).wait()
                for k in range(TOP_K):
                    scatter(b, k).start()

        for slot in range(SC_BUFFERS):
            for k in range(TOP_K):
                scatter(slot, k).wait()

    return dispatch(hp, dest3)


def _sc_gather_call(ys, idx):
    n = idx.shape[0]
    _, w = ys.shape
    nchunk = n // SC_CHUNK
    mesh = plsc.VectorSubcoreMesh(core_axis_name="c", subcore_axis_name="s")

    assert nchunk % (SC_BUFFERS * SC_WORKERS) == 0

    @functools.partial(
        pl.kernel, mesh=mesh, out_type=jax.ShapeDtypeStruct((n, w), U32),
        scratch_types=[pltpu.VMEM((SC_BUFFERS, SC_CHUNK), I32), pltpu.VMEM((SC_BUFFERS, SC_CHUNK, w), U32),
                       pltpu.SemaphoreType.DMA((SC_BUFFERS,)), pltpu.SemaphoreType.DMA((SC_BUFFERS,))],
        name="moe_gather")
    def gather(ys_hbm, idx_hbm, out_hbm, idx_v, rows_v, gather_sem, write_sem):
        first, count = _sc_worker_chunks(nchunk)

        def fetch(slot):
            return pltpu.make_async_copy(ys_hbm.at[idx_v.at[slot]], rows_v.at[slot], gather_sem.at[slot])

        def write(c, slot):
            return pltpu.make_async_copy(rows_v.at[slot], out_hbm.at[pl.ds(c * SC_CHUNK, SC_CHUNK)],
                                         write_sem.at[slot])

        def start_fetch(c, slot):
            pltpu.sync_copy(idx_hbm.at[pl.ds(c * SC_CHUNK, SC_CHUNK)], idx_v.at[slot])
            fetch(slot).start()

        start_fetch(first, 0)

        @pl.loop(0, count, step=SC_BUFFERS)
        def _(j):
            for b in range(SC_BUFFERS):
                c = first + j + b
                nxt = (b + 1) % SC_BUFFERS

                @pl.when(j + b + 1 < count)
                def _():
                    @pl.when(j + b >= 1)
                    def _():
                        write(c - 1, nxt).wait()
                    start_fetch(c + 1, nxt)

                fetch(b).wait()
                write(c, b).start()

        for b in range(SC_BUFFERS):
            write(first + count - SC_BUFFERS + b, b).wait()

    return gather(ys, idx)


def _expert_kernel(be_ref, nv_ref, xs_ref, w1_ref, b1_ref, w2_ref, b2_ref, ys_ref, w1s, w2s):
    b = pl.program_id(0)
    e = be_ref[b]
    prev = be_ref[jnp.maximum(b - 1, 0)]
    nvalid = nv_ref[b]
    f = w2_ref.shape[1]

    @pl.when((b == 0) | (e != prev))
    def _():
        w1s[...] = w1_ref[0].astype(BF16)
        w2s[...] = w2_ref[0].astype(BF16)

    @pl.when(nvalid > 0)
    def _():
        rows = lax.broadcasted_iota(I32, (xs_ref.shape[0], 1), 0)
        xw = jnp.where(rows < nvalid, xs_ref[...], jnp.uint32(0))
        x = _unpack_pairs(xw).astype(BF16)
        gu = jnp.dot(x, w1s[...], preferred_element_type=F32) + b1_ref[0]
        gate = jnp.minimum(gu[:, :f], SWIGLU_LIMIT)
        up = jnp.clip(gu[:, f:], -SWIGLU_LIMIT, SWIGLU_LIMIT)
        act = (up + 1.0) * (gate * _sigmoid(SWIGLU_ALPHA * gate))
        y = jnp.dot(act.astype(BF16), w2s[...], preferred_element_type=F32) + b2_ref[0]
        ys_ref[...] = _pack_pairs(y)

    @pl.when(nvalid <= 0)
    def _():
        ys_ref[...] = jnp.zeros(ys_ref.shape, U32)


def _expert_call(layer, block_e, nvalid, xs, w_in, b_in, w_out, b_out):
    rows, w = xs.shape
    depth, ne, d, f2 = w_in.shape
    f = f2 // 2
    bm = EXPERT_ROWS
    x_spec = pl.BlockSpec((bm, w), lambda b, be, nv: (b, 0))
    grid_spec = pltpu.PrefetchScalarGridSpec(
        num_scalar_prefetch=2,
        grid=(rows // bm,),
        in_specs=[
            x_spec,
            pl.BlockSpec((1, d, f2), lambda b, be, nv: (layer * ne + be[b], 0, 0)),
            pl.BlockSpec((1, 1, f2), lambda b, be, nv: (layer * ne + be[b], 0, 0)),
            pl.BlockSpec((1, f, d), lambda b, be, nv: (layer * ne + be[b], 0, 0)),
            pl.BlockSpec((1, 1, d), lambda b, be, nv: (layer * ne + be[b], 0, 0)),
        ],
        out_specs=x_spec,
        scratch_shapes=[pltpu.VMEM((d, f2), BF16), pltpu.VMEM((f, d), BF16)],
    )
    return pl.pallas_call(
        _expert_kernel,
        grid_spec=grid_spec,
        out_shape=jax.ShapeDtypeStruct((rows, w), U32),
        compiler_params=_params(("arbitrary",)),
        name="moe_experts",
    )(block_e, nvalid, xs,
      w_in.reshape(depth * ne, d, f2), b_in.reshape(depth * ne, 1, f2),
      w_out.reshape(depth * ne, f, d), b_out.reshape(depth * ne, 1, d))


def _final_kernel(x_ref, yg_ref, gt_ref, g2, fg, xo_ref):
    xn = _moe_residual(x_ref, yg_ref, gt_ref, g2)
    r = lax.rsqrt(jnp.mean(xn * xn, axis=-1, keepdims=True) + EPS)
    xo_ref[0] = (xn * r) * fg[...]


def _final_call(x, pending, modr, fg):
    batch, seq, d = x.shape
    ts = COMBINE_TILE
    ns = seq // ts
    return pl.pallas_call(
        _final_kernel,
        grid=(batch, ns),
        in_specs=_residual_specs(pending, batch, ns, ts, d) + [pl.BlockSpec((1, d), lambda b, s: (0, 0))],
        out_specs=pl.BlockSpec((1, ts, d), lambda b, s: (b, s, 0)),
        out_shape=jax.ShapeDtypeStruct((batch, seq, d), F32),
        compiler_params=_params(("arbitrary", "arbitrary")),
        name="moe_combine_final",
    )(*_residual_args(x, pending, modr), fg)


def _moe(layer, hp, logits_t, w_in, b_in, w_out, b_out):
    t, w = hp.shape
    ne = logits_t.shape[0]
    bm = EXPERT_ROWS
    eid, gate, pos, cnt = _route_call(logits_t)
    counts = cnt[:, 0].astype(I32)
    nblk = (counts + bm - 1) // bm
    bend = jnp.cumsum(nblk)
    bstart = bend - nblk
    pstart = (bstart * bm).astype(I32)
    n_blocks = (t * TOP_K) // bm + ne
    blocks = jnp.arange(n_blocks, dtype=I32)
    block_e = jnp.minimum(jnp.sum(blocks[:, None] >= bend[None, :], axis=1), ne - 1).astype(I32)
    mine = block_e[:, None] == jnp.arange(ne, dtype=I32)[None, :]
    left = jnp.sum(jnp.where(mine, counts[None, :] - (blocks[:, None] - bstart[None, :]) * bm, 0), axis=1)
    nvalid = jnp.where(blocks < bend[-1], jnp.clip(left, 0, bm), 0).astype(I32)
    dest = _dest_call(pstart, eid, pos)
    dest3 = dest.reshape(TOP_K, t // SC_CHUNK, SC_CHUNK).transpose(1, 0, 2)
    xs = _sc_dispatch_call(hp, dest3, n_blocks * bm)
    ys = _expert_call(layer, block_e, nvalid, xs, w_in, b_in, w_out, b_out)
    yg = _sc_gather_call(ys, dest.reshape(TOP_K * t))
    return {"yg": yg.reshape(TOP_K, t, w), "gate_t": gate.T, "layer": layer}


def kernel(x, c, norm_mix_g, norm_ffn_g, ada_w, ada_b, pool_w, pool_scale, fox_w_in, fox_b_f, fox_w_o,
           router_w, router_b, exp_w_in, exp_b_in, exp_w_out, exp_b_out, final_g):
    batch, seq, d = x.shape
    depth = ada_w.shape[0]
    mod = _ada_call(c, ada_w, ada_b)
    modr = mod.reshape(depth * batch * 6, 1, d)
    fg = final_g.reshape(1, d)
    pending = None
    for i in range(depth):
        gm = norm_mix_g[i].reshape(1, d)
        gf = norm_ffn_g[i].reshape(1, d)
        rwt = router_w[i].T
        rb = router_b[i].reshape(-1, 1)
        j = i // 2
        if i % 2 == 0:
            x, hp, lg = _pool_call(i, x, pending, modr, gm, gf, pool_w[j], pool_scale[j].reshape(1, d),
                                   rwt, rb)
        else:
            q, k, v, x = _qkv_call(i, x, pending, modr, gm, fox_w_in[j], fox_b_f[j].reshape(1, -1))
            o = _attn_call(q, k, v)
            x, hp, lg = _wo_call(i, x, o, modr, gf, fox_w_o[j].astype(BF16), rwt, rb)
        pending = _moe(i, hp, lg, exp_w_in, exp_b_in, exp_w_out, exp_b_out)
    return _final_call(x, pending, modr, fg)
```

```python
import functools

import jax
import jax.numpy as jnp
import numpy as np
from jax import lax
from jax.experimental import pallas as pl
from jax.experimental.pallas import tpu as pltpu
from jax.experimental.pallas import tpu_sc as plsc

F32 = jnp.float32
BF16 = jnp.bfloat16
U32 = jnp.uint32
I32 = jnp.int32

POOL_WINDOWS = (2, 4, 8, 16)
POOL_HALO = 16
HEAD_DIM = 64
HEADS_PER_STEP = 2
TOP_K = 4
SWIGLU_LIMIT = 7.0
SWIGLU_ALPHA = 1.702
EPS = 1e-6
NEG_BIG = -1e30
LOG2E = 1.4426950408889634
BIAS_PARTS = 3

LANES = 128
SEQ_TILE = 512
QKV_SUBROWS = 256
ATTN_TILE = 512
ATTN_QUERY_SUBTILES = 4
ROUTE_TILE = 512
DEST_TILE = 4096
EXPERT_ROWS = 1024
EXPERT_SUBROWS = 512
COMBINE_TILE = 512
ADA_COLS = 1536
VMEM_LIMIT_BYTES = 56 * 1024 * 1024

SC_CORES = 2
SC_SUBCORES = 16
SC_WORKERS = SC_CORES * SC_SUBCORES
SC_CHUNK = 64
SC_BUFFERS = 2

NT_DIMS = (((1,), (1,)), ((), ()))


def _params(sem, vmem=VMEM_LIMIT_BYTES):
    return pltpu.CompilerParams(dimension_semantics=sem, vmem_limit_bytes=vmem)


def _sigmoid(z):
    return 1.0 / (1.0 + jnp.exp(-z))


def _norm_mod(x, g, scale, shift):
    r = lax.rsqrt(jnp.mean(x * x, axis=-1, keepdims=True) + EPS)
    return (x * r) * (g * (1.0 + scale)) + shift


def _pack_pairs(y):
    w = y.shape[1] // 2
    hi = lax.bitcast_convert_type(y[:, :w].astype(BF16).astype(F32), U32)
    lo = lax.bitcast_convert_type(y[:, w:].astype(BF16).astype(F32), U32)
    return hi | (lo >> 16)


def _unpack_pairs(w):
    hi = lax.bitcast_convert_type(w & jnp.uint32(0xFFFF0000), F32)
    lo = lax.bitcast_convert_type(w << 16, F32)
    return jnp.concatenate([hi, lo], axis=1)


def _split_bf16(x, parts):
    out = []
    r = x
    for _ in range(parts):
        p = r.astype(BF16)
        out.append(p)
        r = r - p.astype(F32)
    return out


def _ffn_prenorm(xn, gf, sc2, sh2, rwt, rb):
    h2 = _norm_mod(xn, gf, sc2, sh2)
    h_hi, h_lo = _split_bf16(h2, 2)
    w_hi, w_lo = _split_bf16(rwt, 2)
    lg = (lax.dot_general(w_hi, h_hi, NT_DIMS, preferred_element_type=F32)
          + lax.dot_general(w_lo, h_hi, NT_DIMS, preferred_element_type=F32)
          + lax.dot_general(w_hi, h_lo, NT_DIMS, preferred_element_type=F32)) + rb
    return _pack_pairs(h2), lg


def _mod_spec(layer, batch, j):
    def index(b, s):
        return ((layer * batch + b) * 6 + j, 0, 0)
    return index


def _moe_residual(x_ref, yg_ref, gt_ref, g2_ref, rows=slice(None)):
    gt = gt_ref[rows, :]
    moe = gt[:, 0:1] * _unpack_pairs(yg_ref[0, rows, :])
    for k in range(1, TOP_K):
        moe = moe + gt[:, k:k + 1] * _unpack_pairs(yg_ref[k, rows, :])
    return x_ref[0, rows, :] + g2_ref[0] * moe


def _residual_in(refs, pending, rows=slice(None)):
    if pending:
        return _moe_residual(*refs[:4], rows=rows)
    return refs[0][0, rows, :]


def _residual_rest(refs, pending):
    return refs[4:] if pending else refs[1:]


def _residual_specs(pending, batch, ns, ts, d):
    specs = [pl.BlockSpec((1, ts, d), lambda b, s: (b, s, 0))]
    if pending:
        _, t, w = pending["yg"].shape
        specs += [
            pl.BlockSpec((TOP_K, ts, w), lambda b, s: (0, b * ns + s, 0)),
            pl.BlockSpec((ts, TOP_K), lambda b, s: (b * ns + s, 0)),
            pl.BlockSpec((1, 1, d), _mod_spec(pending["layer"], batch, 5)),
        ]
    return specs


def _residual_args(x, pending, modr):
    return (x, pending["yg"], pending["gate_t"], modr) if pending else (x,)


def _ada_kernel(ct_ref, w_ref, b_ref, o_ref):
    ct = ct_ref[...]
    cond = ct * _sigmoid(ct)
    w = w_ref[0]
    rows = []
    for b in range(ct.shape[1]):
        rows.append(jnp.sum(w * cond[:, b:b + 1], axis=0, keepdims=True))
    o_ref[0] = jnp.concatenate(rows, axis=0) + b_ref[0]


def _ada_call(c, ada_w, ada_b):
    depth, d, n = ada_w.shape
    batch = c.shape[0]
    tn = ADA_COLS
    return pl.pallas_call(
        _ada_kernel,
        grid=(depth, n // tn),
        in_specs=[
            pl.BlockSpec((d, batch), lambda i, j: (0, 0)),
            pl.BlockSpec((1, d, tn), lambda i, j: (i, 0, j)),
            pl.BlockSpec((1, 1, tn), lambda i, j: (i, 0, j)),
        ],
        out_specs=pl.BlockSpec((1, batch, tn), lambda i, j: (i, 0, j)),
        out_shape=jax.ShapeDtypeStruct((depth, batch, n), F32),
        compiler_params=_params(("arbitrary", "arbitrary")),
        name="ada_mod",
    )(c.T, ada_w, ada_b.reshape(depth, 1, n))


def _pool_kernel(pending, *refs):
    x = _residual_in(refs, pending)
    refs = _residual_rest(refs, pending)
    sh1, sc1, g1, sh2, sc2, gm, gf, pw_ref, ps_ref, rwt_ref, rb_ref, xo_ref, hp_ref, lg_ref, hbuf = refs
    s = pl.program_id(1)
    ts, d = x.shape
    pg = d // len(POOL_WINDOWS)
    h = _norm_mod(x, gm[...], sc1[0], sh1[0])

    @pl.when(s == 0)
    def _():
        hbuf[0:POOL_HALO, :] = jnp.zeros((POOL_HALO, d), F32)

    @pl.when(s > 0)
    def _():
        hbuf[0:POOL_HALO, :] = hbuf[ts:ts + POOL_HALO, :]

    hbuf[POOL_HALO:POOL_HALO + ts, :] = h
    pos = s * ts + lax.broadcasted_iota(I32, (ts, 1), 0)
    ys = []
    for g, w in enumerate(POOL_WINDOWS):
        lo = g * pg
        hg = h[:, lo:lo + pg]
        win = hbuf[:, lo:lo + pg]
        span = 1
        while span < w:
            win = win + pltpu.roll(win, span, 0)
            span *= 2
        acc = win[POOL_HALO:, :]
        cnt = jnp.minimum(pos + 1, w).astype(F32)
        dlt = acc / cnt - hg
        ys.append(jnp.dot(dlt.astype(BF16), pw_ref[g].astype(BF16), preferred_element_type=F32))
    y = jnp.concatenate(ys, axis=1) * ps_ref[...]
    xn = x + g1[0] * y
    xo_ref[0] = xn
    hp, lg = _ffn_prenorm(xn, gf[...], sc2[0], sh2[0], rwt_ref[...], rb_ref[...])
    hp_ref[...] = hp
    lg_ref[...] = lg


def _row_layer_specs(layer, batch, ns, ts, d, e):
    row = lambda j: pl.BlockSpec((1, 1, d), _mod_spec(layer, batch, j))
    vec = pl.BlockSpec((1, d), lambda b, s: (0, 0))
    x_spec = pl.BlockSpec((1, ts, d), lambda b, s: (b, s, 0))
    out_specs = [
        x_spec,
        pl.BlockSpec((ts, d // 2), lambda b, s: (b * ns + s, 0)),
        pl.BlockSpec((e, ts), lambda b, s: (0, b * ns + s)),
    ]
    return row, vec, x_spec, out_specs


def _pool_call(layer, x, pending, modr, gm, gf, pw, ps, rwt, rb):
    batch, seq, d = x.shape
    e = rwt.shape[0]
    ts = SEQ_TILE
    ns = seq // ts
    g = len(POOL_WINDOWS)
    pg = d // g
    row, vec, _, out_specs = _row_layer_specs(layer, batch, ns, ts, d, e)
    return pl.pallas_call(
        functools.partial(_pool_kernel, bool(pending)),
        grid=(batch, ns),
        in_specs=_residual_specs(pending, batch, ns, ts, d) + [
            row(0), row(1), row(2), row(3), row(4), vec, vec,
            pl.BlockSpec((g, pg, pg), lambda b, s: (0, 0, 0)),
            vec,
            pl.BlockSpec((e, d), lambda b, s: (0, 0)),
            pl.BlockSpec((e, 1), lambda b, s: (0, 0)),
        ],
        out_specs=out_specs,
        out_shape=[
            jax.ShapeDtypeStruct((batch, seq, d), F32),
            jax.ShapeDtypeStruct((batch * seq, d // 2), U32),
            jax.ShapeDtypeStruct((e, batch * seq), F32),
        ],
        scratch_shapes=[pltpu.VMEM((ts + POOL_HALO, d), F32)],
        compiler_params=_params(("arbitrary", "arbitrary")),
        name="pool_layer",
    )(*_residual_args(x, pending, modr), modr, modr, modr, modr, modr, gm, gf, pw, ps, rwt, rb)


def _aug_lane(head):
    return HEAD_DIM if head % 2 == 0 else 0


def _bias_placement(nh):
    place = np.zeros((BIAS_PARTS * nh, nh * LANES), np.float32)
    for j in range(BIAS_PARTS):
        for h in range(nh):
            place[j * nh + h, h * LANES + _aug_lane(h) + j] = 1.0
    return jnp.asarray(place, BF16)


def _qkv_kernel(pending, *refs):
    sh1, sc1, gm, w_ref, bf_ref, place_ref, q_ref, k_ref, v_ref = _residual_rest(refs, pending)[:9]
    carry = refs[-1]
    s = pl.program_id(1)
    ts = q_ref.shape[2]
    d = gm.shape[1]
    nh = bf_ref.shape[1]
    sub = QKV_SUBROWS

    @pl.when(s == 0)
    def _():
        carry[...] = jnp.zeros(carry.shape, F32)

    r = lax.broadcasted_iota(I32, (sub, sub), 0)
    c = lax.broadcasted_iota(I32, (sub, sub), 1)
    lower = jnp.where(r >= c, 1.0, 0.0).astype(BF16)
    lane = lax.broadcasted_iota(I32, (sub, LANES), 1)
    for r0 in range(0, ts, sub):
        rows = slice(r0, r0 + sub)
        x = _residual_in(refs, pending, rows)
        if pending:
            refs[13][0, rows, :] = x
        h = _norm_mod(x, gm[...], sc1[0], sh1[0]).astype(BF16)
        fl = jnp.dot(h, w_ref[:, 3 * d:], preferred_element_type=F32)[:, :nh] + bf_ref[...]
        proj = jnp.dot(h, w_ref[:, :3 * d], preferred_element_type=F32)
        logf = jnp.minimum(fl, 0.0) - jnp.log(1.0 + jnp.exp(-jnp.abs(fl)))
        parts = jnp.concatenate(_split_bf16(logf, 3), axis=1)
        cs = jnp.dot(lower, parts, preferred_element_type=F32)
        cum = cs[:, :nh] + cs[:, nh:2 * nh] + cs[:, 2 * nh:] + carry[...]
        carry[...] = cum[sub - 1:sub, :]
        bias = jnp.concatenate(_split_bf16(cum * (-LOG2E), BIAS_PARTS), axis=1)
        k_aug = jnp.dot(bias, place_ref[...], preferred_element_type=F32)
        q_all = proj[:, :d] * (HEAD_DIM ** -0.5 * LOG2E)
        for hd in range(nh):
            lo = (hd // 2) * LANES
            a0 = _aug_lane(hd)
            real = (lane < HEAD_DIM) if hd % 2 == 0 else (lane >= HEAD_DIM)
            q_aug = jnp.where((lane >= a0) & (lane < a0 + BIAS_PARTS), 1.0, 0.0)
            v_aug = jnp.where(lane == a0, 1.0, 0.0)
            q_ref[0, hd, rows, :] = jnp.where(real, q_all[:, lo:lo + LANES], q_aug).astype(BF16)
            k_ref[0, hd, rows, :] = jnp.where(real, proj[:, d + lo:d + lo + LANES],
                                              k_aug[:, hd * LANES:(hd + 1) * LANES]).astype(BF16)
            v_ref[0, hd, rows, :] = jnp.where(real, proj[:, 2 * d + lo:2 * d + lo + LANES],
                                              v_aug).astype(BF16)


def _qkv_call(layer, x, pending, modr, gm, w_in, bf):
    batch, seq, d = x.shape
    nh = bf.shape[1]
    ts = SEQ_TILE
    ns = seq // ts
    w_all = jnp.concatenate([w_in, jnp.zeros((d, LANES - nh), w_in.dtype)], axis=1).astype(BF16)
    place = _bias_placement(nh)
    row = lambda j: pl.BlockSpec((1, 1, d), _mod_spec(layer, batch, j))
    head_spec = pl.BlockSpec((1, nh, ts, LANES), lambda b, s: (b, 0, s, 0))
    out_specs = [head_spec] * 3
    out_shape = [jax.ShapeDtypeStruct((batch, nh, seq, LANES), BF16)] * 3
    if pending:
        out_specs = out_specs + [pl.BlockSpec((1, ts, d), lambda b, s: (b, s, 0))]
        out_shape = out_shape + [jax.ShapeDtypeStruct((batch, seq, d), F32)]
    outs = pl.pallas_call(
        functools.partial(_qkv_kernel, bool(pending)),
        grid=(batch, ns),
        in_specs=_residual_specs(pending, batch, ns, ts, d) + [
            row(0), row(1),
            pl.BlockSpec((1, d), lambda b, s: (0, 0)),
            pl.BlockSpec(w_all.shape, lambda b, s: (0, 0)),
            pl.BlockSpec((1, nh), lambda b, s: (0, 0)),
            pl.BlockSpec(place.shape, lambda b, s: (0, 0)),
        ],
        out_specs=out_specs,
        out_shape=out_shape,
        scratch_shapes=[pltpu.VMEM((1, nh), F32)],
        compiler_params=_params(("arbitrary", "arbitrary")),
        name="fox_qkv",
    )(*_residual_args(x, pending, modr), modr, modr, gm, w_all, bf, place)
    return outs[0], outs[1], outs[2], (outs[3] if pending else x)


def _attn_kernel(q_ref, k_ref, v_ref, o_ref, m_s, acc_s):
    qi = pl.program_id(2)
    tk = ATTN_TILE
    tq = q_ref.shape[2]
    subs = tq // tk
    m_s[...] = jnp.full(m_s.shape, NEG_BIG, F32)
    acc_s[...] = jnp.zeros(acc_s.shape, F32)
    row = lax.broadcasted_iota(I32, (tk, tk), 0)
    col = lax.broadcasted_iota(I32, (tk, tk), 1)

    def step(kb, first, diagonal):
        start = pl.multiple_of(kb * tk, tk)
        rows = slice(first * tk, tq)
        for hh in range(HEADS_PER_STEP):
            kk = k_ref[0, hh, pl.ds(start, tk), :]
            vv = v_ref[0, hh, pl.ds(start, tk), :]
            z = lax.dot_general(q_ref[0, hh, rows, :], kk, NT_DIMS, preferred_element_type=F32)
            if diagonal:
                masked = jnp.where(row >= col, z[:tk], NEG_BIG)
                z = masked if first == subs - 1 else jnp.concatenate([masked, z[tk:]], axis=0)
            m_prev = m_s[hh, rows, :]
            m_new = jnp.maximum(m_prev, jnp.max(z, axis=1, keepdims=True))
            p = jnp.exp2(z - jnp.concatenate([m_new] * (tk // LANES), axis=1))
            acc_s[hh, rows, :] = (jnp.exp2(m_prev - m_new) * acc_s[hh, rows, :]
                                  + jnp.dot(p.astype(BF16), vv, preferred_element_type=F32))
            m_s[hh, rows, :] = m_new

    def body(kb2, carry):
        step(2 * kb2, 0, False)
        step(2 * kb2 + 1, 0, False)
        return carry

    lax.fori_loop(0, (subs // 2) * qi, body, 0)
    for j in range(subs):
        step(subs * qi + j, j, True)
    lane = lax.broadcasted_iota(I32, (tq, LANES), 1)
    outs = []
    for hh in range(HEADS_PER_STEP):
        a = acc_s[hh]
        outs.append(a / a[:, _aug_lane(hh):_aug_lane(hh) + 1])
    o_ref[0] = jnp.where(lane < HEAD_DIM, outs[0], outs[1]).astype(BF16)


def _attn_call(q, k, v):
    batch, nh, seq, _ = q.shape
    tq = ATTN_QUERY_SUBTILES * ATTN_TILE
    kv_spec = pl.BlockSpec((1, HEADS_PER_STEP, seq, LANES), lambda b, h, i: (b, h, 0, 0))
    return pl.pallas_call(
        _attn_kernel,
        grid=(batch, nh // HEADS_PER_STEP, seq // tq),
        in_specs=[pl.BlockSpec((1, HEADS_PER_STEP, tq, LANES), lambda b, h, i: (b, h, i, 0)),
                  kv_spec, kv_spec],
        out_specs=pl.BlockSpec((1, tq, LANES), lambda b, h, i: (b, i, h)),
        out_shape=jax.ShapeDtypeStruct((batch, seq, nh * HEAD_DIM), BF16),
        scratch_shapes=[pltpu.VMEM((HEADS_PER_STEP, tq, LANES), F32)] * 2,
        compiler_params=_params(("arbitrary", "arbitrary", "arbitrary")),
        name="fox_attention",
    )(q, k, v)


def _wo_kernel(x_ref, o_ref, g1, sh2, sc2, gf, wo_ref, rwt_ref, rb_ref, xo_ref, hp_ref, lg_ref):
    m = jnp.dot(o_ref[0], wo_ref[...], preferred_element_type=F32)
    xn = x_ref[0] + g1[0] * m
    xo_ref[0] = xn
    hp, lg = _ffn_prenorm(xn, gf[...], sc2[0], sh2[0], rwt_ref[...], rb_ref[...])
    hp_ref[...] = hp
    lg_ref[...] = lg


def _wo_call(layer, x, o, modr, gf, wo, rwt, rb):
    batch, seq, d = x.shape
    e = rwt.shape[0]
    ts = SEQ_TILE
    ns = seq // ts
    row, vec, x_spec, out_specs = _row_layer_specs(layer, batch, ns, ts, d, e)
    return pl.pallas_call(
        _wo_kernel,
        grid=(batch, ns),
        in_specs=[
            x_spec, x_spec, row(2), row(3), row(4), vec,
            pl.BlockSpec((d, d), lambda b, s: (0, 0)),
            pl.BlockSpec((e, d), lambda b, s: (0, 0)),
            pl.BlockSpec((e, 1), lambda b, s: (0, 0)),
        ],
        out_specs=out_specs,
        out_shape=[
            jax.ShapeDtypeStruct((batch, seq, d), F32),
            jax.ShapeDtypeStruct((batch * seq, d // 2), U32),
            jax.ShapeDtypeStruct((e, batch * seq), F32),
        ],
        compiler_params=_params(("arbitrary", "arbitrary")),
        name="fox_out",
    )(x, o, modr, modr, modr, gf, wo, rwt, rb)


def _route_kernel(lg_ref, eid_ref, gate_ref, pos_ref, cnt_ref, carry):
    i = pl.program_id(0)

    @pl.when(i == 0)
    def _():
        carry[...] = jnp.zeros(carry.shape, F32)

    l = lg_ref[...]
    e, ts = l.shape
    eidx = lax.broadcasted_iota(I32, (e, ts), 0)
    work = l
    top_v, top_i, hot = [], [], []
    for _ in range(TOP_K):
        m = jnp.max(work, axis=0, keepdims=True)
        sel = jnp.min(jnp.where(work == m, eidx, e), axis=0, keepdims=True)
        o = eidx == sel
        top_v.append(m)
        top_i.append(sel)
        hot.append(o)
        work = jnp.where(o, -jnp.inf, work)
    ex = [jnp.exp(v - top_v[0]) for v in top_v]
    den = ex[0] + ex[1] + ex[2] + ex[3]
    chosen = jnp.where(hot[0] | hot[1] | hot[2] | hot[3], 1.0, 0.0)
    r = lax.broadcasted_iota(I32, (ts, ts), 0)
    c = lax.broadcasted_iota(I32, (ts, ts), 1)
    before = jnp.where(r < c, 1.0, 0.0).astype(BF16)
    rank = jnp.dot(chosen.astype(BF16), before, preferred_element_type=F32) + carry[...]
    pos = [jnp.sum(jnp.where(o, rank, 0.0), axis=0, keepdims=True) for o in hot]
    carry[...] = carry[...] + jnp.sum(chosen, axis=1, keepdims=True)
    eid_ref[...] = jnp.concatenate(top_i, axis=0)
    gate_ref[...] = jnp.concatenate([x / den for x in ex], axis=0)
    pos_ref[...] = jnp.concatenate(pos, axis=0).astype(I32)
    cnt_ref[...] = jnp.broadcast_to(carry[...], cnt_ref.shape)


def _route_call(logits_t):
    e, t = logits_t.shape
    ts = ROUTE_TILE
    out = pl.BlockSpec((TOP_K, ts), lambda i: (0, i))
    return pl.pallas_call(
        _route_kernel,
        grid=(t // ts,),
        in_specs=[pl.BlockSpec((e, ts), lambda i: (0, i))],
        out_specs=[out, out, out, pl.BlockSpec((e, LANES), lambda i: (0, 0))],
        out_shape=[
            jax.ShapeDtypeStruct((TOP_K, t), I32),
            jax.ShapeDtypeStruct((TOP_K, t), F32),
            jax.ShapeDtypeStruct((TOP_K, t), I32),
            jax.ShapeDtypeStruct((e, LANES), F32),
        ],
        scratch_shapes=[pltpu.VMEM((e, 1), F32)],
        compiler_params=_params(("arbitrary",)),
        name="route_topk",
    )(logits_t)


def _dest_kernel(pstart_ref, eid_ref, pos_ref, dest_ref):
    eid = eid_ref[...]
    dest = pos_ref[...]
    for e in range(pstart_ref.shape[0]):
        dest = dest + jnp.where(eid == e, pstart_ref[e], 0)
    dest_ref[...] = dest


def _dest_call(pstart, eid, pos):
    k, t = eid.shape
    tt = min(DEST_TILE, t)
    spec = pl.BlockSpec((k, tt), lambda i, ps: (0, i))
    grid_spec = pltpu.PrefetchScalarGridSpec(
        num_scalar_prefetch=1, grid=(t // tt,), in_specs=[spec, spec], out_specs=spec)
    return pl.pallas_call(
        _dest_kernel,
        grid_spec=grid_spec,
        out_shape=jax.ShapeDtypeStruct((k, t), I32),
        compiler_params=_params(("arbitrary",)),
        name="route_dest",
    )(pstart, eid, pos)


def _sc_worker_chunks(total_chunks):
    worker = lax.axis_index("s") * SC_CORES + lax.axis_index("c")
    per_worker = total_chunks // SC_WORKERS
    return worker * per_worker, per_worker


def _sc_dispatch_call(hp, dest3, rows):
    t, w = hp.shape
    nchunk = dest3.shape[0]
    mesh = plsc.VectorSubcoreMesh(core_axis_name="c", subcore_axis_name="s")

    assert nchunk % (SC_BUFFERS * SC_WORKERS) == 0

    @functools.partial(
        pl.kernel, mesh=mesh, out_type=jax.ShapeDtypeStruct((rows, w), U32),
        scratch_types=[pltpu.VMEM((SC_BUFFERS, TOP_K, SC_CHUNK), I32),
                       pltpu.VMEM((SC_BUFFERS, SC_CHUNK, w), U32),
                       pltpu.SemaphoreType.DMA((SC_BUFFERS,)), pltpu.SemaphoreType.DMA((SC_BUFFERS,))],
        name="moe_dispatch")
    def dispatch(hp_hbm, dest_hbm, xs_hbm, idx_v, rows_v, load_sem, scatter_sem):
        first, count = _sc_worker_chunks(nchunk)

        def load(c, slot):
            return pltpu.make_async_copy(hp_hbm.at[pl.ds(c * SC_CHUNK, SC_CHUNK)], rows_v.at[slot],
                                         load_sem.at[slot])

        def scatter(slot, k):
            return pltpu.make_async_copy(rows_v.at[slot], xs_hbm.at[idx_v.at[slot, k]],
                                         scatter_sem.at[slot])

        def start_load(c, slot):
            pltpu.sync_copy(dest_hbm.at[c], idx_v.at[slot])
            load(c, slot).start()

        start_load(first, 0)

        @pl.loop(0, count, step=SC_BUFFERS)
        def _(j):
            for b in range(SC_BUFFERS):
                c = first + j + b
                nxt = (b + 1) % SC_BUFFERS

                @pl.when(j + b + 1 < count)
                def _():
                    @pl.when(j + b >= 1)
                    def _():
                        for k in range(TOP_K):
                            scatter(nxt, k).wait()
                    start_load(c + 1, nxt)

                load(c, b).wait()
                for k in range(TOP_K):
                    scatter(b, k).start()

        for slot in range(SC_BUFFERS):
            for k in range(TOP_K):
                scatter(slot, k).wait()

    return dispatch(hp, dest3)


def _sc_gather_call(ys, idx):
    n = idx.shape[0]
    _, w = ys.shape
    nchunk = n // SC_CHUNK
    mesh = plsc.VectorSubcoreMesh(core_axis_name="c", subcore_axis_name="s")

    assert nchunk % (SC_BUFFERS * SC_WORKERS) == 0

    @functools.partial(
        pl.kernel, mesh=mesh, out_type=jax.ShapeDtypeStruct((n, w), U32),
        scratch_types=[pltpu.VMEM((SC_BUFFERS, SC_CHUNK), I32), pltpu.VMEM((SC_BUFFERS, SC_CHUNK, w), U32),
                       pltpu.SemaphoreType.DMA((SC_BUFFERS,)), pltpu.SemaphoreType.DMA((SC_BUFFERS,))],
        name="moe_gather")
    def gather(ys_hbm, idx_hbm, out_hbm, idx_v, rows_v, gather_sem, write_sem):
        first, count = _sc_worker_chunks(nchunk)

        def fetch(slot):
            return pltpu.make_async_copy(ys_hbm.at[idx_v.at[slot]], rows_v.at[slot], gather_sem.at[slot])

        def write(c, slot):
            return pltpu.make_async_copy(rows_v.at[slot], out_hbm.at[pl.ds(c * SC_CHUNK, SC_CHUNK)],
                                         write_sem.at[slot])

        def start_fetch(c, slot):
            pltpu.sync_copy(idx_hbm.at[pl.ds(c * SC_CHUNK, SC_CHUNK)], idx_v.at[slot])
            fetch(slot).start()

        start_fetch(first, 0)

        @pl.loop(0, count, step=SC_BUFFERS)
        def _(j):
            for b in range(SC_BUFFERS):
                c = first + j + b
                nxt = (b + 1) % SC_BUFFERS

                @pl.when(j + b + 1 < count)
                def _():
                    @pl.when(j + b >= 1)
                    def _():
                        write(c - 1, nxt).wait()
                    start_fetch(c + 1, nxt)

                fetch(b).wait()
                write(c, b).start()

        for b in range(SC_BUFFERS):
            write(first + count - SC_BUFFERS + b, b).wait()

    return gather(ys, idx)


def _expert_kernel(be_ref, nv_ref, xs_ref, w1_ref, b1_ref, w2_ref, b2_ref, ys_ref, w1s, w2s):
    b = pl.program_id(0)
    e = be_ref[b]
    prev = be_ref[jnp.maximum(b - 1, 0)]
    nvalid = nv_ref[b]
    f = w2_ref.shape[1]

    @pl.when((b == 0) | (e != prev))
    def _():
        w1s[...] = w1_ref[0].astype(BF16)
        w2s[...] = w2_ref[0].astype(BF16)

    sub = EXPERT_SUBROWS
    for r0 in range(0, xs_ref.shape[0], sub):
        @pl.when(nvalid > r0)
        def _():
            rows = r0 + lax.broadcasted_iota(I32, (sub, 1), 0)
            xw = jnp.where(rows < nvalid, xs_ref[r0:r0 + sub, :], jnp.uint32(0))
            x = _unpack_pairs(xw).astype(BF16)
            gu = jnp.dot(x, w1s[...], preferred_element_type=F32) + b1_ref[0]
            gate = jnp.minimum(gu[:, :f], SWIGLU_LIMIT)
            up = jnp.clip(gu[:, f:], -SWIGLU_LIMIT, SWIGLU_LIMIT)
            act = (up + 1.0) * (gate * _sigmoid(SWIGLU_ALPHA * gate))
            y = jnp.dot(act.astype(BF16), w2s[...], preferred_element_type=F32) + b2_ref[0]
            ys_ref[r0:r0 + sub, :] = _pack_pairs(y)

        @pl.when(nvalid <= r0)
        def _():
            ys_ref[r0:r0 + sub, :] = jnp.zeros((sub, ys_ref.shape[1]), U32)


def _expert_call(layer, block_e, nvalid, xs, w_in, b_in, w_out, b_out):
    rows, w = xs.shape
    depth, ne, d, f2 = w_in.shape
    f = f2 // 2
    bm = EXPERT_ROWS
    x_spec = pl.BlockSpec((bm, w), lambda b, be, nv: (b, 0))
    grid_spec = pltpu.PrefetchScalarGridSpec(
        num_scalar_prefetch=2,
        grid=(rows // bm,),
        in_specs=[
            x_spec,
            pl.BlockSpec((1, d, f2), lambda b, be, nv: (layer * ne + be[b], 0, 0)),
            pl.BlockSpec((1, 1, f2), lambda b, be, nv: (layer * ne + be[b], 0, 0)),
            pl.BlockSpec((1, f, d), lambda b, be, nv: (layer * ne + be[b], 0, 0)),
            pl.BlockSpec((1, 1, d), lambda b, be, nv: (layer * ne + be[b], 0, 0)),
        ],
        out_specs=x_spec,
        scratch_shapes=[pltpu.VMEM((d, f2), BF16), pltpu.VMEM((f, d), BF16)],
    )
    return pl.pallas_call(
        _expert_kernel,
        grid_spec=grid_spec,
        out_shape=jax.ShapeDtypeStruct((rows, w), U32),
        compiler_params=_params(("arbitrary",)),
        name="moe_experts",
    )(block_e, nvalid, xs,
      w_in.reshape(depth * ne, d, f2), b_in.reshape(depth * ne, 1, f2),
      w_out.reshape(depth * ne, f, d), b_out.reshape(depth * ne, 1, d))


def _final_kernel(x_ref, yg_ref, gt_ref, g2, fg, xo_ref):
    xn = _moe_residual(x_ref, yg_ref, gt_ref, g2)
    r = lax.rsqrt(jnp.mean(xn * xn, axis=-1, keepdims=True) + EPS)
    xo_ref[0] = (xn * r) * fg[...]


def _final_call(x, pending, modr, fg):
    batch, seq, d = x.shape
    ts = COMBINE_TILE
    ns = seq // ts
    return pl.pallas_call(
        _final_kernel,
        grid=(batch, ns),
        in_specs=_residual_specs(pending, batch, ns, ts, d) + [pl.BlockSpec((1, d), lambda b, s: (0, 0))],
        out_specs=pl.BlockSpec((1, ts, d), lambda b, s: (b, s, 0)),
        out_shape=jax.ShapeDtypeStruct((batch, seq, d), F32),
        compiler_params=_params(("arbitrary", "arbitrary")),
        name="moe_combine_final",
    )(*_residual_args(x, pending, modr), fg)


def _moe(layer, hp, logits_t, w_in, b_in, w_out, b_out):
    t, w = hp.shape
    ne = logits_t.shape[0]
    bm = EXPERT_ROWS
    eid, gate, pos, cnt = _route_call(logits_t)
    counts = cnt[:, 0].astype(I32)
    nblk = (counts + bm - 1) // bm
    bend = jnp.cumsum(nblk)
    bstart = bend - nblk
    pstart = (bstart * bm).astype(I32)
    n_blocks = (t * TOP_K) // bm + ne
    blocks = jnp.arange(n_blocks, dtype=I32)
    block_e = jnp.minimum(jnp.sum(blocks[:, None] >= bend[None, :], axis=1), ne - 1).astype(I32)
    mine = block_e[:, None] == jnp.arange(ne, dtype=I32)[None, :]
    left = jnp.sum(jnp.where(mine, counts[None, :] - (blocks[:, None] - bstart[None, :]) * bm, 0), axis=1)
    nvalid = jnp.where(blocks < bend[-1], jnp.clip(left, 0, bm), 0).astype(I32)
    dest = _dest_call(pstart, eid, pos)
    dest3 = dest.reshape(TOP_K, t // SC_CHUNK, SC_CHUNK).transpose(1, 0, 2)
    xs = _sc_dispatch_call(hp, dest3, n_blocks * bm)
    ys = _expert_call(layer, block_e, nvalid, xs, w_in, b_in, w_out, b_out)
    yg = _sc_gather_call(ys, dest.reshape(TOP_K * t))
    return {"yg": yg.reshape(TOP_K, t, w), "gate_t": gate.T, "layer": layer}


def kernel(x, c, norm_mix_g, norm_ffn_g, ada_w, ada_b, pool_w, pool_scale, fox_w_in, fox_b_f, fox_w_o,
           router_w, router_b, exp_w_in, exp_b_in, exp_w_out, exp_b_out, final_g):
    batch, seq, d = x.shape
    depth = ada_w.shape[0]
    mod = _ada_call(c, ada_w, ada_b)
    modr = mod.reshape(depth * batch * 6, 1, d)
    fg = final_g.reshape(1, d)
    pending = None
    for i in range(depth):
        gm = norm_mix_g[i].reshape(1, d)
        gf = norm_ffn_g[i].reshape(1, d)
        rwt = router_w[i].T
        rb = router_b[i].reshape(-1, 1)
        j = i // 2
        if i % 2 == 0:
            x, hp, lg = _pool_call(i, x, pending, modr, gm, gf, pool_w[j], pool_scale[j].reshape(1, d),
                                   rwt, rb)
        else:
            q, k, v, x = _qkv_call(i, x, pending, modr, gm, fox_w_in[j], fox_b_f[j].reshape(1, -1))
            o = _attn_call(q, k, v)
            x, hp, lg = _wo_call(i, x, o, modr, gf, fox_w_o[j].astype(BF16), rwt, rb)
        pending = _moe(i, hp, lg, exp_w_in, exp_b_in, exp_w_out, exp_b_out)
    return _final_call(x, pending, modr, fg)
```

```python
import functools

import jax
import jax.numpy as jnp
import numpy as np
from jax import lax
from jax.experimental import pallas as pl
from jax.experimental.pallas import tpu as pltpu
from jax.experimental.pallas import tpu_sc as plsc

F32 = jnp.float32
BF16 = jnp.bfloat16
U32 = jnp.uint32
I32 = jnp.int32

POOL_WINDOWS = (2, 4, 8, 16)
POOL_HALO = 16
HEAD_DIM = 64
HEADS_PER_STEP = 2
TOP_K = 4
SWIGLU_LIMIT = 7.0
SWIGLU_ALPHA = 1.702
EPS = 1e-6
NEG_BIG = -1e30
LOG2E = 1.4426950408889634
BIAS_PARTS = 3
UNDERFLOW_LOG2 = 160.0
NORM_SLACK = 1.02

LANES = 128
STAT_ROWS = 8
SEQ_TILE = 512
QKV_SUBROWS = 256
ATTN_TILE = 512
ATTN_QUERY_SUBTILES = 4
ROUTE_TILE = 512
DEST_TILE = 4096
EXPERT_ROWS = 1024
EXPERT_SUBROWS = 512
COMBINE_TILE = 512
ADA_COLS = 1536
VMEM_LIMIT_BYTES = 56 * 1024 * 1024

SC_CORES = 2
SC_SUBCORES = 16
SC_WORKERS = SC_CORES * SC_SUBCORES
SC_CHUNK = 64
SC_BUFFERS = 2

NT_DIMS = (((1,), (1,)), ((), ()))


def _params(sem, vmem=VMEM_LIMIT_BYTES):
    return pltpu.CompilerParams(dimension_semantics=sem, vmem_limit_bytes=vmem)


def _sigmoid(z):
    return 1.0 / (1.0 + jnp.exp(-z))


def _norm_mod(x, g, scale, shift):
    r = lax.rsqrt(jnp.mean(x * x, axis=-1, keepdims=True) + EPS)
    return (x * r) * (g * (1.0 + scale)) + shift


def _pack_pairs(y):
    w = y.shape[1] // 2
    hi = lax.bitcast_convert_type(y[:, :w].astype(BF16).astype(F32), U32)
    lo = lax.bitcast_convert_type(y[:, w:].astype(BF16).astype(F32), U32)
    return hi | (lo >> 16)


def _unpack_pairs(w):
    hi = lax.bitcast_convert_type(w & jnp.uint32(0xFFFF0000), F32)
    lo = lax.bitcast_convert_type(w << 16, F32)
    return jnp.concatenate([hi, lo], axis=1)


def _split_bf16(x, parts):
    out = []
    r = x
    for _ in range(parts):
        p = r.astype(BF16)
        out.append(p)
        r = r - p.astype(F32)
    return out


def _ffn_prenorm(xn, gf, sc2, sh2, rwt, rb):
    h2 = _norm_mod(xn, gf, sc2, sh2)
    h_hi, h_lo = _split_bf16(h2, 2)
    w_hi, w_lo = _split_bf16(rwt, 2)
    lg = (lax.dot_general(w_hi, h_hi, NT_DIMS, preferred_element_type=F32)
          + lax.dot_general(w_lo, h_hi, NT_DIMS, preferred_element_type=F32)
          + lax.dot_general(w_hi, h_lo, NT_DIMS, preferred_element_type=F32)) + rb
    return _pack_pairs(h2), lg


def _mod_spec(layer, batch, j):
    def index(b, s):
        return ((layer * batch + b) * 6 + j, 0, 0)
    return index


def _moe_residual(x_ref, yg_ref, gt_ref, g2_ref, rows=slice(None)):
    gt = gt_ref[rows, :]
    moe = gt[:, 0:1] * _unpack_pairs(yg_ref[0, rows, :])
    for k in range(1, TOP_K):
        moe = moe + gt[:, k:k + 1] * _unpack_pairs(yg_ref[k, rows, :])
    return x_ref[0, rows, :] + g2_ref[0] * moe


def _residual_in(refs, pending, rows=slice(None)):
    if pending:
        return _moe_residual(*refs[:4], rows=rows)
    return refs[0][0, rows, :]


def _residual_rest(refs, pending):
    return refs[4:] if pending else refs[1:]


def _residual_specs(pending, batch, ns, ts, d):
    specs = [pl.BlockSpec((1, ts, d), lambda b, s: (b, s, 0))]
    if pending:
        _, t, w = pending["yg"].shape
        specs += [
            pl.BlockSpec((TOP_K, ts, w), lambda b, s: (0, b * ns + s, 0)),
            pl.BlockSpec((ts, TOP_K), lambda b, s: (b * ns + s, 0)),
            pl.BlockSpec((1, 1, d), _mod_spec(pending["layer"], batch, 5)),
        ]
    return specs


def _residual_args(x, pending, modr):
    return (x, pending["yg"], pending["gate_t"], modr) if pending else (x,)


def _ada_kernel(ct_ref, w_ref, b_ref, o_ref):
    ct = ct_ref[...]
    cond = ct * _sigmoid(ct)
    w = w_ref[0]
    rows = []
    for b in range(ct.shape[1]):
        rows.append(jnp.sum(w * cond[:, b:b + 1], axis=0, keepdims=True))
    o_ref[0] = jnp.concatenate(rows, axis=0) + b_ref[0]


def _ada_call(c, ada_w, ada_b):
    depth, d, n = ada_w.shape
    batch = c.shape[0]
    tn = ADA_COLS
    return pl.pallas_call(
        _ada_kernel,
        grid=(depth, n // tn),
        in_specs=[
            pl.BlockSpec((d, batch), lambda i, j: (0, 0)),
            pl.BlockSpec((1, d, tn), lambda i, j: (i, 0, j)),
            pl.BlockSpec((1, 1, tn), lambda i, j: (i, 0, j)),
        ],
        out_specs=pl.BlockSpec((1, batch, tn), lambda i, j: (i, 0, j)),
        out_shape=jax.ShapeDtypeStruct((depth, batch, n), F32),
        compiler_params=_params(("arbitrary", "arbitrary")),
        name="ada_mod",
    )(c.T, ada_w, ada_b.reshape(depth, 1, n))


def _pool_kernel(pending, *refs):
    x = _residual_in(refs, pending)
    refs = _residual_rest(refs, pending)
    sh1, sc1, g1, sh2, sc2, gm, gf, pw_ref, ps_ref, rwt_ref, rb_ref, xo_ref, hp_ref, lg_ref, hbuf = refs
    s = pl.program_id(1)
    ts, d = x.shape
    pg = d // len(POOL_WINDOWS)
    h = _norm_mod(x, gm[...], sc1[0], sh1[0])

    @pl.when(s == 0)
    def _():
        hbuf[0:POOL_HALO, :] = jnp.zeros((POOL_HALO, d), F32)

    @pl.when(s > 0)
    def _():
        hbuf[0:POOL_HALO, :] = hbuf[ts:ts + POOL_HALO, :]

    hbuf[POOL_HALO:POOL_HALO + ts, :] = h
    pos = s * ts + lax.broadcasted_iota(I32, (ts, 1), 0)
    ys = []
    for g, w in enumerate(POOL_WINDOWS):
        lo = g * pg
        hg = h[:, lo:lo + pg]
        win = hbuf[:, lo:lo + pg]
        span = 1
        while span < w:
            win = win + pltpu.roll(win, span, 0)
            span *= 2
        acc = win[POOL_HALO:, :]
        cnt = jnp.minimum(pos + 1, w).astype(F32)
        dlt = acc / cnt - hg
        ys.append(jnp.dot(dlt.astype(BF16), pw_ref[g].astype(BF16), preferred_element_type=F32))
    y = jnp.concatenate(ys, axis=1) * ps_ref[...]
    xn = x + g1[0] * y
    xo_ref[0] = xn
    hp, lg = _ffn_prenorm(xn, gf[...], sc2[0], sh2[0], rwt_ref[...], rb_ref[...])
    hp_ref[...] = hp
    lg_ref[...] = lg


def _row_layer_specs(layer, batch, ns, ts, d, e):
    row = lambda j: pl.BlockSpec((1, 1, d), _mod_spec(layer, batch, j))
    vec = pl.BlockSpec((1, d), lambda b, s: (0, 0))
    x_spec = pl.BlockSpec((1, ts, d), lambda b, s: (b, s, 0))
    out_specs = [
        x_spec,
        pl.BlockSpec((ts, d // 2), lambda b, s: (b * ns + s, 0)),
        pl.BlockSpec((e, ts), lambda b, s: (0, b * ns + s)),
    ]
    return row, vec, x_spec, out_specs


def _pool_call(layer, x, pending, modr, gm, gf, pw, ps, rwt, rb):
    batch, seq, d = x.shape
    e = rwt.shape[0]
    ts = SEQ_TILE
    ns = seq // ts
    g = len(POOL_WINDOWS)
    pg = d // g
    row, vec, _, out_specs = _row_layer_specs(layer, batch, ns, ts, d, e)
    return pl.pallas_call(
        functools.partial(_pool_kernel, bool(pending)),
        grid=(batch, ns),
        in_specs=_residual_specs(pending, batch, ns, ts, d) + [
            row(0), row(1), row(2), row(3), row(4), vec, vec,
            pl.BlockSpec((g, pg, pg), lambda b, s: (0, 0, 0)),
            vec,
            pl.BlockSpec((e, d), lambda b, s: (0, 0)),
            pl.BlockSpec((e, 1), lambda b, s: (0, 0)),
        ],
        out_specs=out_specs,
        out_shape=[
            jax.ShapeDtypeStruct((batch, seq, d), F32),
            jax.ShapeDtypeStruct((batch * seq, d // 2), U32),
            jax.ShapeDtypeStruct((e, batch * seq), F32),
        ],
        scratch_shapes=[pltpu.VMEM((ts + POOL_HALO, d), F32)],
        compiler_params=_params(("arbitrary", "arbitrary")),
        name="pool_layer",
    )(*_residual_args(x, pending, modr), modr, modr, modr, modr, modr, gm, gf, pw, ps, rwt, rb)


def _aug_lane(head):
    return HEAD_DIM if head % 2 == 0 else 0


def _bias_placement(nh):
    place = np.zeros((BIAS_PARTS * nh, nh * LANES), np.float32)
    for j in range(BIAS_PARTS):
        for h in range(nh):
            place[j * nh + h, h * LANES + _aug_lane(h) + j] = 1.0
    return jnp.asarray(place, BF16)


def _qkv_kernel(pending, *refs):
    rest = _residual_rest(refs, pending)
    sh1, sc1, gm, w_ref, bf_ref, place_ref, seg_ref, q_ref, k_ref, v_ref, st_ref = rest[:11]
    carry = refs[-1]
    s = pl.program_id(1)
    ts = q_ref.shape[2]
    d = gm.shape[1]
    nh = bf_ref.shape[1]
    sub = QKV_SUBROWS

    @pl.when(s == 0)
    def _():
        carry[...] = jnp.zeros(carry.shape, F32)

    r = lax.broadcasted_iota(I32, (sub, sub), 0)
    c = lax.broadcasted_iota(I32, (sub, sub), 1)
    lower = jnp.where(r >= c, 1.0, 0.0).astype(BF16)
    lane = lax.broadcasted_iota(I32, (sub, LANES), 1)
    q_top = k_top = jnp.zeros((1, LANES), F32)
    b_top = jnp.full((1, nh), NEG_BIG, F32)
    for r0 in range(0, ts, sub):
        rows = slice(r0, r0 + sub)
        x = _residual_in(refs, pending, rows)
        if pending:
            rest[11][0, rows, :] = x
        h = _norm_mod(x, gm[...], sc1[0], sh1[0]).astype(BF16)
        fl = jnp.dot(h, w_ref[:, 3 * d:], preferred_element_type=F32)[:, :nh] + bf_ref[...]
        proj = jnp.dot(h, w_ref[:, :3 * d], preferred_element_type=F32)
        logf = jnp.minimum(fl, 0.0) - jnp.log(1.0 + jnp.exp(-jnp.abs(fl)))
        parts = jnp.concatenate(_split_bf16(logf, 3), axis=1)
        cs = jnp.dot(lower, parts, preferred_element_type=F32)
        cum = cs[:, :nh] + cs[:, nh:2 * nh] + cs[:, 2 * nh:] + carry[...]
        carry[...] = cum[sub - 1:sub, :]
        bias = jnp.concatenate(_split_bf16(cum * (-LOG2E), BIAS_PARTS), axis=1)
        k_aug = jnp.dot(bias, place_ref[...], preferred_element_type=F32)
        q_all = proj[:, :d] * (HEAD_DIM ** -0.5 * LOG2E)
        k_all = proj[:, d:2 * d]
        q_norm = jnp.sqrt(jnp.dot((q_all * q_all).astype(BF16), seg_ref[...], preferred_element_type=F32))
        k_norm = jnp.sqrt(jnp.dot((k_all * k_all).astype(BF16), seg_ref[...], preferred_element_type=F32))
        q_top = jnp.maximum(q_top, jnp.max(q_norm, axis=0, keepdims=True))
        k_top = jnp.maximum(k_top, jnp.max(k_norm, axis=0, keepdims=True))
        b_top = jnp.maximum(b_top, jnp.max(cum * (-LOG2E), axis=0, keepdims=True))
        for hd in range(nh):
            lo = (hd // 2) * LANES
            a0 = _aug_lane(hd)
            real = (lane < HEAD_DIM) if hd % 2 == 0 else (lane >= HEAD_DIM)
            q_aug = jnp.where((lane >= a0) & (lane < a0 + BIAS_PARTS), 1.0, 0.0)
            v_aug = jnp.where(lane == a0, 1.0, 0.0)
            q_ref[0, hd, rows, :] = jnp.where(real, q_all[:, lo:lo + LANES], q_aug).astype(BF16)
            k_ref[0, hd, rows, :] = jnp.where(real, proj[:, d + lo:d + lo + LANES],
                                              k_aug[:, hd * LANES:(hd + 1) * LANES]).astype(BF16)
            v_ref[0, hd, rows, :] = jnp.where(real, proj[:, 2 * d + lo:2 * d + lo + LANES],
                                              v_aug).astype(BF16)
    b_row = jnp.concatenate([b_top, jnp.zeros((1, LANES - nh), F32)], axis=1)
    st_ref[0, 0] = jnp.concatenate([q_top, k_top, b_row, jnp.zeros((STAT_ROWS - 3, LANES), F32)], axis=0)


def _qkv_call(layer, x, pending, modr, gm, w_in, bf):
    batch, seq, d = x.shape
    nh = bf.shape[1]
    ts = SEQ_TILE
    ns = seq // ts
    w_all = jnp.concatenate([w_in, jnp.zeros((d, LANES - nh), w_in.dtype)], axis=1).astype(BF16)
    place = _bias_placement(nh)
    seg = jnp.asarray(np.arange(d)[:, None] // HEAD_DIM == np.arange(LANES)[None, :], BF16)
    row = lambda j: pl.BlockSpec((1, 1, d), _mod_spec(layer, batch, j))
    head_spec = pl.BlockSpec((1, nh, ts, LANES), lambda b, s: (b, 0, s, 0))
    out_specs = [head_spec] * 3 + [pl.BlockSpec((1, 1, STAT_ROWS, LANES), lambda b, s: (b, s, 0, 0))]
    out_shape = [jax.ShapeDtypeStruct((batch, nh, seq, LANES), BF16)] * 3 + [
        jax.ShapeDtypeStruct((batch, ns, STAT_ROWS, LANES), F32)]
    if pending:
        out_specs = out_specs + [pl.BlockSpec((1, ts, d), lambda b, s: (b, s, 0))]
        out_shape = out_shape + [jax.ShapeDtypeStruct((batch, seq, d), F32)]
    outs = pl.pallas_call(
        functools.partial(_qkv_kernel, bool(pending)),
        grid=(batch, ns),
        in_specs=_residual_specs(pending, batch, ns, ts, d) + [
            row(0), row(1),
            pl.BlockSpec((1, d), lambda b, s: (0, 0)),
            pl.BlockSpec(w_all.shape, lambda b, s: (0, 0)),
            pl.BlockSpec((1, nh), lambda b, s: (0, 0)),
            pl.BlockSpec(place.shape, lambda b, s: (0, 0)),
            pl.BlockSpec(seg.shape, lambda b, s: (0, 0)),
        ],
        out_specs=out_specs,
        out_shape=out_shape,
        scratch_shapes=[pltpu.VMEM((1, nh), F32)],
        compiler_params=_params(("arbitrary", "arbitrary")),
        name="fox_qkv",
    )(*_residual_args(x, pending, modr), modr, modr, gm, w_all, bf, place, seg)
    return outs[0], outs[1], outs[2], outs[3], (outs[4] if pending else x)


def _attn_kernel(q_ref, k_ref, v_ref, st_ref, o_ref, m_s, acc_s):
    hp = pl.program_id(1)
    qi = pl.program_id(2)
    tk = ATTN_TILE
    tq = q_ref.shape[2]
    subs = tq // tk
    m_s[...] = jnp.full(m_s.shape, NEG_BIG, F32)
    acc_s[...] = jnp.zeros(acc_s.shape, F32)
    row = lax.broadcasted_iota(I32, (tk, tk), 0)
    col = lax.broadcasted_iota(I32, (tk, tk), 1)

    def step(kb, first, diagonal):
        start = pl.multiple_of(kb * tk, tk)
        rows = slice(first * tk, tq)
        for hh in range(HEADS_PER_STEP):
            kk = k_ref[0, hh, pl.ds(start, tk), :]
            vv = v_ref[0, hh, pl.ds(start, tk), :]
            z = lax.dot_general(q_ref[0, hh, rows, :], kk, NT_DIMS, preferred_element_type=F32)
            if diagonal:
                masked = jnp.where(row >= col, z[:tk], NEG_BIG)
                z = masked if first == subs - 1 else jnp.concatenate([masked, z[tk:]], axis=0)
            m_prev = m_s[hh, rows, :]
            m_new = jnp.maximum(m_prev, jnp.max(z, axis=1, keepdims=True))
            p = jnp.exp2(z - jnp.concatenate([m_new] * (tk // LANES), axis=1))
            acc_s[hh, rows, :] = (jnp.exp2(m_prev - m_new) * acc_s[hh, rows, :]
                                  + jnp.dot(p.astype(BF16), vv, preferred_element_type=F32))
            m_s[hh, rows, :] = m_new

    for j in range(subs):
        step(subs * qi + j, j, True)

    st = st_ref[0]
    nt = st.shape[0]
    stat = lax.broadcasted_iota(I32, st.shape, 1)
    q_top = jnp.max(jnp.where(stat == 0, st, 0.0), axis=1)
    k_top = jnp.max(jnp.where(stat == 1, st, 0.0), axis=1)
    b_top = jnp.max(jnp.where(stat == 2, st, NEG_BIG), axis=1)
    tile = lax.broadcasted_iota(I32, (nt, LANES), 0)
    head = lax.broadcasted_iota(I32, (nt, LANES), 1)
    here = (tile >= subs * qi) & (tile < subs * (qi + 1))
    q_here = jnp.max(jnp.where(here, q_top, 0.0), axis=0, keepdims=True)
    upper = NORM_SLACK * q_here * k_top + b_top + 1.0
    first_needed = (subs * qi).astype(F32)
    for hh in range(HEADS_PER_STEP):
        floor = jnp.min(m_s[hh]) - UNDERFLOW_LOG2
        needed = (upper >= floor) & (head == HEADS_PER_STEP * hp + hh) & (tile < subs * qi)
        first_needed = jnp.minimum(first_needed, jnp.min(jnp.where(needed, tile.astype(F32), float(nt))))
    kb0 = first_needed.astype(I32)
    count = subs * qi - kb0

    def body(i, carry):
        step(kb0 + 2 * i, 0, False)
        step(kb0 + 2 * i + 1, 0, False)
        return carry

    lax.fori_loop(0, count // 2, body, 0)

    @pl.when(count % 2 == 1)
    def _():
        step(subs * qi - 1, 0, False)

    lane = lax.broadcasted_iota(I32, (tq, LANES), 1)
    outs = []
    for hh in range(HEADS_PER_STEP):
        a = acc_s[hh]
        outs.append(a / a[:, _aug_lane(hh):_aug_lane(hh) + 1])
    o_ref[0] = jnp.where(lane < HEAD_DIM, outs[0], outs[1]).astype(BF16)


def _attn_call(q, k, v, stats):
    batch, nh, seq, _ = q.shape
    tq = ATTN_QUERY_SUBTILES * ATTN_TILE
    kv_spec = pl.BlockSpec((1, HEADS_PER_STEP, seq, LANES), lambda b, h, i: (b, h, 0, 0))
    return pl.pallas_call(
        _attn_kernel,
        grid=(batch, nh // HEADS_PER_STEP, seq // tq),
        in_specs=[pl.BlockSpec((1, HEADS_PER_STEP, tq, LANES), lambda b, h, i: (b, h, i, 0)),
                  kv_spec, kv_spec,
                  pl.BlockSpec((1,) + stats.shape[1:], lambda b, h, i: (b, 0, 0, 0))],
        out_specs=pl.BlockSpec((1, tq, LANES), lambda b, h, i: (b, i, h)),
        out_shape=jax.ShapeDtypeStruct((batch, seq, nh * HEAD_DIM), BF16),
        scratch_shapes=[pltpu.VMEM((HEADS_PER_STEP, tq, LANES), F32)] * 2,
        compiler_params=_params(("arbitrary", "arbitrary", "arbitrary")),
        name="fox_attention",
    )(q, k, v, stats)


def _wo_kernel(x_ref, o_ref, g1, sh2, sc2, gf, wo_ref, rwt_ref, rb_ref, xo_ref, hp_ref, lg_ref):
    m = jnp.dot(o_ref[0], wo_ref[...], preferred_element_type=F32)
    xn = x_ref[0] + g1[0] * m
    xo_ref[0] = xn
    hp, lg = _ffn_prenorm(xn, gf[...], sc2[0], sh2[0], rwt_ref[...], rb_ref[...])
    hp_ref[...] = hp
    lg_ref[...] = lg


def _wo_call(layer, x, o, modr, gf, wo, rwt, rb):
    batch, seq, d = x.shape
    e = rwt.shape[0]
    ts = SEQ_TILE
    ns = seq // ts
    row, vec, x_spec, out_specs = _row_layer_specs(layer, batch, ns, ts, d, e)
    return pl.pallas_call(
        _wo_kernel,
        grid=(batch, ns),
        in_specs=[
            x_spec, x_spec, row(2), row(3), row(4), vec,
            pl.BlockSpec((d, d), lambda b, s: (0, 0)),
            pl.BlockSpec((e, d), lambda b, s: (0, 0)),
            pl.BlockSpec((e, 1), lambda b, s: (0, 0)),
        ],
        out_specs=out_specs,
        out_shape=[
            jax.ShapeDtypeStruct((batch, seq, d), F32),
            jax.ShapeDtypeStruct((batch * seq, d // 2), U32),
            jax.ShapeDtypeStruct((e, batch * seq), F32),
        ],
        compiler_params=_params(("arbitrary", "arbitrary")),
        name="fox_out",
    )(x, o, modr, modr, modr, gf, wo, rwt, rb)


def _route_kernel(lg_ref, eid_ref, gate_ref, pos_ref, cnt_ref, carry):
    i = pl.program_id(0)

    @pl.when(i == 0)
    def _():
        carry[...] = jnp.zeros(carry.shape, F32)

    l = lg_ref[...]
    e, ts = l.shape
    eidx = lax.broadcasted_iota(I32, (e, ts), 0)
    work = l
    top_v, top_i, hot = [], [], []
    for _ in range(TOP_K):
        m = jnp.max(work, axis=0, keepdims=True)
        sel = jnp.min(jnp.where(work == m, eidx, e), axis=0, keepdims=True)
        o = eidx == sel
        top_v.append(m)
        top_i.append(sel)
        hot.append(o)
        work = jnp.where(o, -jnp.inf, work)
    ex = [jnp.exp(v - top_v[0]) for v in top_v]
    den = ex[0] + ex[1] + ex[2] + ex[3]
    chosen = jnp.where(hot[0] | hot[1] | hot[2] | hot[3], 1.0, 0.0)
    r = lax.broadcasted_iota(I32, (ts, ts), 0)
    c = lax.broadcasted_iota(I32, (ts, ts), 1)
    before = jnp.where(r < c, 1.0, 0.0).astype(BF16)
    rank = jnp.dot(chosen.astype(BF16), before, preferred_element_type=F32) + carry[...]
    pos = [jnp.sum(jnp.where(o, rank, 0.0), axis=0, keepdims=True) for o in hot]
    carry[...] = carry[...] + jnp.sum(chosen, axis=1, keepdims=True)
    eid_ref[...] = jnp.concatenate(top_i, axis=0)
    gate_ref[...] = jnp.concatenate([x / den for x in ex], axis=0)
    pos_ref[...] = jnp.concatenate(pos, axis=0).astype(I32)
    cnt_ref[...] = jnp.broadcast_to(carry[...], cnt_ref.shape)


def _route_call(logits_t):
    e, t = logits_t.shape
    ts = ROUTE_TILE
    out = pl.BlockSpec((TOP_K, ts), lambda i: (0, i))
    return pl.pallas_call(
        _route_kernel,
        grid=(t // ts,),
        in_specs=[pl.BlockSpec((e, ts), lambda i: (0, i))],
        out_specs=[out, out, out, pl.BlockSpec((e, LANES), lambda i: (0, 0))],
        out_shape=[
            jax.ShapeDtypeStruct((TOP_K, t), I32),
            jax.ShapeDtypeStruct((TOP_K, t), F32),
            jax.ShapeDtypeStruct((TOP_K, t), I32),
            jax.ShapeDtypeStruct((e, LANES), F32),
        ],
        scratch_shapes=[pltpu.VMEM((e, 1), F32)],
        compiler_params=_params(("arbitrary",)),
        name="route_topk",
    )(logits_t)


def _dest_kernel(pstart_ref, eid_ref, pos_ref, dest_ref):
    eid = eid_ref[...]
    dest = pos_ref[...]
    for e in range(pstart_ref.shape[0]):
        dest = dest + jnp.where(eid == e, pstart_ref[e], 0)
    dest_ref[...] = dest


def _dest_call(pstart, eid, pos):
    k, t = eid.shape
    tt = min(DEST_TILE, t)
    spec = pl.BlockSpec((k, tt), lambda i, ps: (0, i))
    grid_spec = pltpu.PrefetchScalarGridSpec(
        num_scalar_prefetch=1, grid=(t // tt,), in_specs=[spec, spec], out_specs=spec)
    return pl.pallas_call(
        _dest_kernel,
        grid_spec=grid_spec,
        out_shape=jax.ShapeDtypeStruct((k, t), I32),
        compiler_params=_params(("arbitrary",)),
        name="route_dest",
    )(pstart, eid, pos)


def _sc_worker_chunks(total_chunks):
    worker = lax.axis_index("s") * SC_CORES + lax.axis_index("c")
    per_worker = total_chunks // SC_WORKERS
    return worker * per_worker, per_worker


def _sc_dispatch_call(hp, dest3, rows):
    t, w = hp.shape
    nchunk = dest3.shape[0]
    mesh = plsc.VectorSubcoreMesh(core_axis_name="c", subcore_axis_name="s")

    assert nchunk % (SC_BUFFERS * SC_WORKERS) == 0

    @functools.partial(
        pl.kernel, mesh=mesh, out_type=jax.ShapeDtypeStruct((rows, w), U32),
        scratch_types=[pltpu.VMEM((SC_BUFFERS, TOP_K, SC_CHUNK), I32),
                       pltpu.VMEM((SC_BUFFERS, SC_CHUNK, w), U32),
                       pltpu.SemaphoreType.DMA((SC_BUFFERS,)), pltpu.SemaphoreType.DMA((SC_BUFFERS,))],
        name="moe_dispatch")
    def dispatch(hp_hbm, dest_hbm, xs_hbm, idx_v, rows_v, load_sem, scatter_sem):
        first, count = _sc_worker_chunks(nchunk)

        def load(c, slot):
            return pltpu.make_async_copy(hp_hbm.at[pl.ds(c * SC_CHUNK, SC_CHUNK)], rows_v.at[slot],
                                         load_sem.at[slot])

        def scatter(slot, k):
            return pltpu.make_async_copy(rows_v.at[slot], xs_hbm.at[idx_v.at[slot, k]],
                                         scatter_sem.at[slot])

        def start_load(c, slot):
            pltpu.sync_copy(dest_hbm.at[c], idx_v.at[slot])
            load(c, slot).start()

        start_load(first, 0)

        @pl.loop(0, count, step=SC_BUFFERS)
        def _(j):
            for b in range(SC_BUFFERS):
                c = first + j + b
                nxt = (b + 1) % SC_BUFFERS

                @pl.when(j + b + 1 < count)
                def _():
                    @pl.when(j + b >= 1)
                    def _():
                        for k in range(TOP_K):
                            scatter(nxt, k).wait()
                    start_load(c + 1, nxt)

                load(c, b).wait()
                for k in range(TOP_K):
                    scatter(b, k).start()

        for slot in range(SC_BUFFERS):
            for k in range(TOP_K):
                scatter(slot, k).wait()

    return dispatch(hp, dest3)


def _sc_gather_call(ys, idx):
    n = idx.shape[0]
    _, w = ys.shape
    nchunk = n // SC_CHUNK
    mesh = plsc.VectorSubcoreMesh(core_axis_name="c", subcore_axis_name="s")

    assert nchunk % (SC_BUFFERS * SC_WORKERS) == 0

    @functools.partial(
        pl.kernel, mesh=mesh, out_type=jax.ShapeDtypeStruct((n, w), U32),
        scratch_types=[pltpu.VMEM((SC_BUFFERS, SC_CHUNK), I32), pltpu.VMEM((SC_BUFFERS, SC_CHUNK, w), U32),
                       pltpu.SemaphoreType.DMA((SC_BUFFERS,)), pltpu.SemaphoreType.DMA((SC_BUFFERS,))],
        name="moe_gather")
    def gather(ys_hbm, idx_hbm, out_hbm, idx_v, rows_v, gather_sem, write_sem):
        first, count = _sc_worker_chunks(nchunk)

        def fetch(slot):
            return pltpu.make_async_copy(ys_hbm.at[idx_v.at[slot]], rows_v.at[slot], gather_sem.at[slot])

        def write(c, slot):
            return pltpu.make_async_copy(rows_v.at[slot], out_hbm.at[pl.ds(c * SC_CHUNK, SC_CHUNK)],
                                         write_sem.at[slot])

        def start_fetch(c, slot):
            pltpu.sync_copy(idx_hbm.at[pl.ds(c * SC_CHUNK, SC_CHUNK)], idx_v.at[slot])
            fetch(slot).start()

        start_fetch(first, 0)

        @pl.loop(0, count, step=SC_BUFFERS)
        def _(j):
            for b in range(SC_BUFFERS):
                c = first + j + b
                nxt = (b + 1) % SC_BUFFERS

                @pl.when(j + b + 1 < count)
                def _():
                    @pl.when(j + b >= 1)
                    def _():
                        write(c - 1, nxt).wait()
                    start_fetch(c + 1, nxt)

                fetch(b).wait()
                write(c, b).start()

        for b in range(SC_BUFFERS):
            write(first + count - SC_BUFFERS + b, b).wait()

    return gather(ys, idx)


def _expert_kernel(be_ref, nv_ref, xs_ref, w1_ref, b1_ref, w2_ref, b2_ref, ys_ref, w1s, w2s):
    b = pl.program_id(0)
    e = be_ref[b]
    prev = be_ref[jnp.maximum(b - 1, 0)]
    nvalid = nv_ref[b]
    f = w2_ref.shape[1]

    @pl.when((b == 0) | (e != prev))
    def _():
        w1s[...] = w1_ref[0].astype(BF16)
        w2s[...] = w2_ref[0].astype(BF16)

    sub = EXPERT_SUBROWS
    for r0 in range(0, xs_ref.shape[0], sub):
        @pl.when(nvalid > r0)
        def _():
            rows = r0 + lax.broadcasted_iota(I32, (sub, 1), 0)
            xw = jnp.where(rows < nvalid, xs_ref[r0:r0 + sub, :], jnp.uint32(0))
            x = _unpack_pairs(xw).astype(BF16)
            gu = jnp.dot(x, w1s[...], preferred_element_type=F32) + b1_ref[0]
            gate = jnp.minimum(gu[:, :f], SWIGLU_LIMIT)
            up = jnp.clip(gu[:, f:], -SWIGLU_LIMIT, SWIGLU_LIMIT)
            act = (up + 1.0) * (gate * _sigmoid(SWIGLU_ALPHA * gate))
            y = jnp.dot(act.astype(BF16), w2s[...], preferred_element_type=F32) + b2_ref[0]
            ys_ref[r0:r0 + sub, :] = _pack_pairs(y)

        @pl.when(nvalid <= r0)
        def _():
            ys_ref[r0:r0 + sub, :] = jnp.zeros((sub, ys_ref.shape[1]), U32)


def _expert_call(layer, block_e, nvalid, xs, w_in, b_in, w_out, b_out):
    rows, w = xs.shape
    depth, ne, d, f2 = w_in.shape
    f = f2 // 2
    bm = EXPERT_ROWS
    x_spec = pl.BlockSpec((bm, w), lambda b, be, nv: (b, 0))
    grid_spec = pltpu.PrefetchScalarGridSpec(
        num_scalar_prefetch=2,
        grid=(rows // bm,),
        in_specs=[
            x_spec,
            pl.BlockSpec((1, d, f2), lambda b, be, nv: (layer * ne + be[b], 0, 0)),
            pl.BlockSpec((1, 1, f2), lambda b, be, nv: (layer * ne + be[b], 0, 0)),
            pl.BlockSpec((1, f, d), lambda b, be, nv: (layer * ne + be[b], 0, 0)),
            pl.BlockSpec((1, 1, d), lambda b, be, nv: (layer * ne + be[b], 0, 0)),
        ],
        out_specs=x_spec,
        scratch_shapes=[pltpu.VMEM((d, f2), BF16), pltpu.VMEM((f, d), BF16)],
    )
    return pl.pallas_call(
        _expert_kernel,
        grid_spec=grid_spec,
        out_shape=jax.ShapeDtypeStruct((rows, w), U32),
        compiler_params=_params(("arbitrary",)),
        name="moe_experts",
    )(block_e, nvalid, xs,
      w_in.reshape(depth * ne, d, f2), b_in.reshape(depth * ne, 1, f2),
      w_out.reshape(depth * ne, f, d), b_out.reshape(depth * ne, 1, d))


def _final_kernel(x_ref, yg_ref, gt_ref, g2, fg, xo_ref):
    xn = _moe_residual(x_ref, yg_ref, gt_ref, g2)
    r = lax.rsqrt(jnp.mean(xn * xn, axis=-1, keepdims=True) + EPS)
    xo_ref[0] = (xn * r) * fg[...]


def _final_call(x, pending, modr, fg):
    batch, seq, d = x.shape
    ts = COMBINE_TILE
    ns = seq // ts
    return pl.pallas_call(
        _final_kernel,
        grid=(batch, ns),
        in_specs=_residual_specs(pending, batch, ns, ts, d) + [pl.BlockSpec((1, d), lambda b, s: (0, 0))],
        out_specs=pl.BlockSpec((1, ts, d), lambda b, s: (b, s, 0)),
        out_shape=jax.ShapeDtypeStruct((batch, seq, d), F32),
        compiler_params=_params(("arbitrary", "arbitrary")),
        name="moe_combine_final",
    )(*_residual_args(x, pending, modr), fg)


def _moe(layer, hp, logits_t, w_in, b_in, w_out, b_out):
    t, w = hp.shape
    ne = logits_t.shape[0]
    bm = EXPERT_ROWS
    eid, gate, pos, cnt = _route_call(logits_t)
    counts = cnt[:, 0].astype(I32)
    nblk = (counts + bm - 1) // bm
    bend = jnp.cumsum(nblk)
    bstart = bend - nblk
    pstart = (bstart * bm).astype(I32)
    n_blocks = (t * TOP_K) // bm + ne
    blocks = jnp.arange(n_blocks, dtype=I32)
    block_e = jnp.minimum(jnp.sum(blocks[:, None] >= bend[None, :], axis=1), ne - 1).astype(I32)
    mine = block_e[:, None] == jnp.arange(ne, dtype=I32)[None, :]
    left = jnp.sum(jnp.where(mine, counts[None, :] - (blocks[:, None] - bstart[None, :]) * bm, 0), axis=1)
    nvalid = jnp.where(blocks < bend[-1], jnp.clip(left, 0, bm), 0).astype(I32)
    dest = _dest_call(pstart, eid, pos)
    dest3 = dest.reshape(TOP_K, t // SC_CHUNK, SC_CHUNK).transpose(1, 0, 2)
    xs = _sc_dispatch_call(hp, dest3, n_blocks * bm)
    ys = _expert_call(layer, block_e, nvalid, xs, w_in, b_in, w_out, b_out)
    yg = _sc_gather_call(ys, dest.reshape(TOP_K * t))
    return {"yg": yg.reshape(TOP_K, t, w), "gate_t": gate.T, "layer": layer}


def kernel(x, c, norm_mix_g, norm_ffn_g, ada_w, ada_b, pool_w, pool_scale, fox_w_in, fox_b_f, fox_w_o,
           router_w, router_b, exp_w_in, exp_b_in, exp_w_out, exp_b_out, final_g):
    batch, seq, d = x.shape
    depth = ada_w.shape[0]
    mod = _ada_call(c, ada_w, ada_b)
    modr = mod.reshape(depth * batch * 6, 1, d)
    fg = final_g.reshape(1, d)
    pending = None
    for i in range(depth):
        gm = norm_mix_g[i].reshape(1, d)
        gf = norm_ffn_g[i].reshape(1, d)
        rwt = router_w[i].T
        rb = router_b[i].reshape(-1, 1)
        j = i // 2
        if i % 2 == 0:
            x, hp, lg = _pool_call(i, x, pending, modr, gm, gf, pool_w[j], pool_scale[j].reshape(1, d),
                                   rwt, rb)
        else:
            q, k, v, stats, x = _qkv_call(i, x, pending, modr, gm, fox_w_in[j], fox_b_f[j].reshape(1, -1))
            o = _attn_call(q, k, v, stats)
            x, hp, lg = _wo_call(i, x, o, modr, gf, fox_w_o[j].astype(BF16), rwt, rb)
        pending = _moe(i, hp, lg, exp_w_in, exp_b_in, exp_w_out, exp_b_out)
    return _final_call(x, pending, modr, fg)
```

```python
import functools

import jax
import jax.numpy as jnp
import numpy as np
from jax import lax
from jax.experimental import pallas as pl
from jax.experimental.pallas import tpu as pltpu
from jax.experimental.pallas import tpu_sc as plsc

F32 = jnp.float32
BF16 = jnp.bfloat16
U32 = jnp.uint32
I32 = jnp.int32

POOL_WINDOWS = (2, 4, 8, 16)
POOL_HALO = 16
HEAD_DIM = 64
HEADS_PER_STEP = 2
TOP_K = 4
SWIGLU_LIMIT = 7.0
SWIGLU_ALPHA = 1.702
EPS = 1e-6
NEG_BIG = -1e30
LOG2E = 1.4426950408889634
BIAS_PARTS = 3
UNDERFLOW_LOG2 = 160.0
NORM_SLACK = 1.02

LANES = 128
STAT_ROWS = 8
SEQ_TILE = 512
QKV_SUBROWS = 256
ATTN_TILE = 512
ATTN_QUERY_SUBTILES = 4
ROUTE_TILE = 512
DEST_TILE = 4096
EXPERT_ROWS = 1024
EXPERT_SUBROWS = 512
COMBINE_TILE = 512
ADA_COLS = 1536
VMEM_LIMIT_BYTES = 56 * 1024 * 1024

SC_CORES = 2
SC_SUBCORES = 16
SC_WORKERS = SC_CORES * SC_SUBCORES
SC_CHUNK = 64
SC_BUFFERS = 2

NT_DIMS = (((1,), (1,)), ((), ()))


def _params(sem, vmem=VMEM_LIMIT_BYTES):
    return pltpu.CompilerParams(dimension_semantics=sem, vmem_limit_bytes=vmem)


def _sigmoid(z):
    return 1.0 / (1.0 + jnp.exp(-z))


def _norm_mod(x, g, scale, shift):
    r = lax.rsqrt(jnp.mean(x * x, axis=-1, keepdims=True) + EPS)
    return (x * r) * (g * (1.0 + scale)) + shift


def _pack_pairs(y):
    w = y.shape[1] // 2
    hi = lax.bitcast_convert_type(y[:, :w].astype(BF16).astype(F32), U32)
    lo = lax.bitcast_convert_type(y[:, w:].astype(BF16).astype(F32), U32)
    return hi | (lo >> 16)


def _unpack_pairs(w):
    hi = lax.bitcast_convert_type(w & jnp.uint32(0xFFFF0000), F32)
    lo = lax.bitcast_convert_type(w << 16, F32)
    return jnp.concatenate([hi, lo], axis=1)


def _split_bf16(x, parts):
    out = []
    r = x
    for _ in range(parts):
        p = r.astype(BF16)
        out.append(p)
        r = r - p.astype(F32)
    return out


def _ffn_prenorm(xn, gf, sc2, sh2, rwt, rb):
    h2 = _norm_mod(xn, gf, sc2, sh2)
    h_hi, h_lo = _split_bf16(h2, 2)
    w_hi, w_lo = _split_bf16(rwt, 2)
    lg = (lax.dot_general(w_hi, h_hi, NT_DIMS, preferred_element_type=F32)
          + lax.dot_general(w_lo, h_hi, NT_DIMS, preferred_element_type=F32)
          + lax.dot_general(w_hi, h_lo, NT_DIMS, preferred_element_type=F32)) + rb
    return _pack_pairs(h2), lg


def _mod_spec(layer, batch, j):
    def index(b, s):
        return ((layer * batch + b) * 6 + j, 0, 0)
    return index


def _moe_residual(x_ref, yg_ref, gt_ref, g2_ref, rows=slice(None)):
    gt = gt_ref[rows, :]
    moe = gt[:, 0:1] * _unpack_pairs(yg_ref[0, rows, :])
    for k in range(1, TOP_K):
        moe = moe + gt[:, k:k + 1] * _unpack_pairs(yg_ref[k, rows, :])
    return x_ref[0, rows, :] + g2_ref[0] * moe


def _residual_in(refs, pending, rows=slice(None)):
    if pending:
        return _moe_residual(*refs[:4], rows=rows)
    return refs[0][0, rows, :]


def _residual_rest(refs, pending):
    return refs[4:] if pending else refs[1:]


def _residual_specs(pending, batch, ns, ts, d):
    specs = [pl.BlockSpec((1, ts, d), lambda b, s: (b, s, 0))]
    if pending:
        _, t, w = pending["yg"].shape
        specs += [
            pl.BlockSpec((TOP_K, ts, w), lambda b, s: (0, b * ns + s, 0)),
            pl.BlockSpec((ts, TOP_K), lambda b, s: (b * ns + s, 0)),
            pl.BlockSpec((1, 1, d), _mod_spec(pending["layer"], batch, 5)),
        ]
    return specs


def _residual_args(x, pending, modr):
    return (x, pending["yg"], pending["gate_t"], modr) if pending else (x,)


def _ada_kernel(ct_ref, w_ref, b_ref, o_ref):
    ct = ct_ref[...]
    cond = ct * _sigmoid(ct)
    w = w_ref[0]
    rows = []
    for b in range(ct.shape[1]):
        rows.append(jnp.sum(w * cond[:, b:b + 1], axis=0, keepdims=True))
    o_ref[0] = jnp.concatenate(rows, axis=0) + b_ref[0]


def _ada_call(c, ada_w, ada_b):
    depth, d, n = ada_w.shape
    batch = c.shape[0]
    tn = ADA_COLS
    return pl.pallas_call(
        _ada_kernel,
        grid=(depth, n // tn),
        in_specs=[
            pl.BlockSpec((d, batch), lambda i, j: (0, 0)),
            pl.BlockSpec((1, d, tn), lambda i, j: (i, 0, j)),
            pl.BlockSpec((1, 1, tn), lambda i, j: (i, 0, j)),
        ],
        out_specs=pl.BlockSpec((1, batch, tn), lambda i, j: (i, 0, j)),
        out_shape=jax.ShapeDtypeStruct((depth, batch, n), F32),
        compiler_params=_params(("arbitrary", "arbitrary")),
        name="ada_mod",
    )(c.T, ada_w, ada_b.reshape(depth, 1, n))


def _pool_kernel(pending, *refs):
    x = _residual_in(refs, pending)
    refs = _residual_rest(refs, pending)
    sh1, sc1, g1, sh2, sc2, gm, gf, pw_ref, ps_ref, rwt_ref, rb_ref, xo_ref, hp_ref, lg_ref, hbuf = refs
    s = pl.program_id(1)
    ts, d = x.shape
    pg = d // len(POOL_WINDOWS)
    h = _norm_mod(x, gm[...], sc1[0], sh1[0])

    @pl.when(s == 0)
    def _():
        hbuf[0:POOL_HALO, :] = jnp.zeros((POOL_HALO, d), F32)

    @pl.when(s > 0)
    def _():
        hbuf[0:POOL_HALO, :] = hbuf[ts:ts + POOL_HALO, :]

    hbuf[POOL_HALO:POOL_HALO + ts, :] = h
    pos = s * ts + lax.broadcasted_iota(I32, (ts, 1), 0)
    ys = []
    for g, w in enumerate(POOL_WINDOWS):
        lo = g * pg
        hg = h[:, lo:lo + pg]
        win = hbuf[:, lo:lo + pg]
        span = 1
        while span < w:
            win = win + pltpu.roll(win, span, 0)
            span *= 2
        acc = win[POOL_HALO:, :]
        cnt = jnp.minimum(pos + 1, w).astype(F32)
        dlt = acc / cnt - hg
        ys.append(jnp.dot(dlt.astype(BF16), pw_ref[g].astype(BF16), preferred_element_type=F32))
    y = jnp.concatenate(ys, axis=1) * ps_ref[...]
    xn = x + g1[0] * y
    xo_ref[0] = xn
    hp, lg = _ffn_prenorm(xn, gf[...], sc2[0], sh2[0], rwt_ref[...], rb_ref[...])
    hp_ref[...] = hp
    lg_ref[...] = lg


def _row_layer_specs(layer, batch, ns, ts, d, e):
    row = lambda j: pl.BlockSpec((1, 1, d), _mod_spec(layer, batch, j))
    vec = pl.BlockSpec((1, d), lambda b, s: (0, 0))
    x_spec = pl.BlockSpec((1, ts, d), lambda b, s: (b, s, 0))
    out_specs = [
        x_spec,
        pl.BlockSpec((ts, d // 2), lambda b, s: (b * ns + s, 0)),
        pl.BlockSpec((e, ts), lambda b, s: (0, b * ns + s)),
    ]
    return row, vec, x_spec, out_specs


def _pool_call(layer, x, pending, modr, gm, gf, pw, ps, rwt, rb):
    batch, seq, d = x.shape
    e = rwt.shape[0]
    ts = SEQ_TILE
    ns = seq // ts
    g = len(POOL_WINDOWS)
    pg = d // g
    row, vec, _, out_specs = _row_layer_specs(layer, batch, ns, ts, d, e)
    return pl.pallas_call(
        functools.partial(_pool_kernel, bool(pending)),
        grid=(batch, ns),
        in_specs=_residual_specs(pending, batch, ns, ts, d) + [
            row(0), row(1), row(2), row(3), row(4), vec, vec,
            pl.BlockSpec((g, pg, pg), lambda b, s: (0, 0, 0)),
            vec,
            pl.BlockSpec((e, d), lambda b, s: (0, 0)),
            pl.BlockSpec((e, 1), lambda b, s: (0, 0)),
        ],
        out_specs=out_specs,
        out_shape=[
            jax.ShapeDtypeStruct((batch, seq, d), F32),
            jax.ShapeDtypeStruct((batch * seq, d // 2), U32),
            jax.ShapeDtypeStruct((e, batch * seq), F32),
        ],
        scratch_shapes=[pltpu.VMEM((ts + POOL_HALO, d), F32)],
        compiler_params=_params(("arbitrary", "arbitrary")),
        name="pool_layer",
    )(*_residual_args(x, pending, modr), modr, modr, modr, modr, modr, gm, gf, pw, ps, rwt, rb)


def _aug_lane(head):
    return HEAD_DIM if head % 2 == 0 else 0


def _bias_placement(nh):
    place = np.zeros((BIAS_PARTS * nh, nh * LANES), np.float32)
    for j in range(BIAS_PARTS):
        for h in range(nh):
            place[j * nh + h, h * LANES + _aug_lane(h) + j] = 1.0
    return jnp.asarray(place, BF16)


def _qkv_kernel(pending, *refs):
    rest = _residual_rest(refs, pending)
    sh1, sc1, gm, w_ref, bf_ref, place_ref, seg_ref, q_ref, k_ref, v_ref, st_ref = rest[:11]
    carry = refs[-1]
    s = pl.program_id(1)
    ts = q_ref.shape[2]
    d = gm.shape[1]
    nh = bf_ref.shape[1]
    sub = QKV_SUBROWS

    @pl.when(s == 0)
    def _():
        carry[...] = jnp.zeros(carry.shape, F32)

    r = lax.broadcasted_iota(I32, (sub, sub), 0)
    c = lax.broadcasted_iota(I32, (sub, sub), 1)
    lower = jnp.where(r >= c, 1.0, 0.0).astype(BF16)
    lane = lax.broadcasted_iota(I32, (sub, LANES), 1)
    q_top = k_top = jnp.zeros((1, LANES), F32)
    b_top = jnp.full((1, nh), NEG_BIG, F32)
    for r0 in range(0, ts, sub):
        rows = slice(r0, r0 + sub)
        x = _residual_in(refs, pending, rows)
        if pending:
            rest[11][0, rows, :] = x
        h = _norm_mod(x, gm[...], sc1[0], sh1[0]).astype(BF16)
        fl = jnp.dot(h, w_ref[:, 3 * d:], preferred_element_type=F32)[:, :nh] + bf_ref[...]
        proj = jnp.dot(h, w_ref[:, :3 * d], preferred_element_type=F32)
        logf = jnp.minimum(fl, 0.0) - jnp.log(1.0 + jnp.exp(-jnp.abs(fl)))
        parts = jnp.concatenate(_split_bf16(logf, 3), axis=1)
        cs = jnp.dot(lower, parts, preferred_element_type=F32)
        cum = cs[:, :nh] + cs[:, nh:2 * nh] + cs[:, 2 * nh:] + carry[...]
        carry[...] = cum[sub - 1:sub, :]
        bias = jnp.concatenate(_split_bf16(cum * (-LOG2E), BIAS_PARTS), axis=1)
        k_aug = jnp.dot(bias, place_ref[...], preferred_element_type=F32)
        q_all = proj[:, :d] * (HEAD_DIM ** -0.5 * LOG2E)
        k_all = proj[:, d:2 * d]
        q_norm = jnp.sqrt(jnp.dot((q_all * q_all).astype(BF16), seg_ref[...], preferred_element_type=F32))
        k_norm = jnp.sqrt(jnp.dot((k_all * k_all).astype(BF16), seg_ref[...], preferred_element_type=F32))
        q_top = jnp.maximum(q_top, jnp.max(q_norm, axis=0, keepdims=True))
        k_top = jnp.maximum(k_top, jnp.max(k_norm, axis=0, keepdims=True))
        b_top = jnp.maximum(b_top, jnp.max(cum * (-LOG2E), axis=0, keepdims=True))
        for hd in range(nh):
            lo = (hd // 2) * LANES
            a0 = _aug_lane(hd)
            real = (lane < HEAD_DIM) if hd % 2 == 0 else (lane >= HEAD_DIM)
            q_aug = jnp.where((lane >= a0) & (lane < a0 + BIAS_PARTS), 1.0, 0.0)
            v_aug = jnp.where(lane == a0, 1.0, 0.0)
            q_ref[0, hd, rows, :] = jnp.where(real, q_all[:, lo:lo + LANES], q_aug).astype(BF16)
            k_ref[0, hd, rows, :] = jnp.where(real, proj[:, d + lo:d + lo + LANES],
                                              k_aug[:, hd * LANES:(hd + 1) * LANES]).astype(BF16)
            v_ref[0, hd, rows, :] = jnp.where(real, proj[:, 2 * d + lo:2 * d + lo + LANES],
                                              v_aug).astype(BF16)
    b_row = jnp.concatenate([b_top, jnp.zeros((1, LANES - nh), F32)], axis=1)
    st_ref[0, 0] = jnp.concatenate([q_top, k_top, b_row, jnp.zeros((STAT_ROWS - 3, LANES), F32)], axis=0)


def _qkv_call(layer, x, pending, modr, gm, w_in, bf):
    batch, seq, d = x.shape
    nh = bf.shape[1]
    ts = SEQ_TILE
    ns = seq // ts
    w_all = jnp.concatenate([w_in, jnp.zeros((d, LANES - nh), w_in.dtype)], axis=1).astype(BF16)
    place = _bias_placement(nh)
    seg = jnp.asarray(np.arange(d)[:, None] // HEAD_DIM == np.arange(LANES)[None, :], BF16)
    row = lambda j: pl.BlockSpec((1, 1, d), _mod_spec(layer, batch, j))
    head_spec = pl.BlockSpec((1, nh, ts, LANES), lambda b, s: (b, 0, s, 0))
    out_specs = [head_spec] * 3 + [pl.BlockSpec((1, 1, STAT_ROWS, LANES), lambda b, s: (b, s, 0, 0))]
    out_shape = [jax.ShapeDtypeStruct((batch, nh, seq, LANES), BF16)] * 3 + [
        jax.ShapeDtypeStruct((batch, ns, STAT_ROWS, LANES), F32)]
    if pending:
        out_specs = out_specs + [pl.BlockSpec((1, ts, d), lambda b, s: (b, s, 0))]
        out_shape = out_shape + [jax.ShapeDtypeStruct((batch, seq, d), F32)]
    outs = pl.pallas_call(
        functools.partial(_qkv_kernel, bool(pending)),
        grid=(batch, ns),
        in_specs=_residual_specs(pending, batch, ns, ts, d) + [
            row(0), row(1),
            pl.BlockSpec((1, d), lambda b, s: (0, 0)),
            pl.BlockSpec(w_all.shape, lambda b, s: (0, 0)),
            pl.BlockSpec((1, nh), lambda b, s: (0, 0)),
            pl.BlockSpec(place.shape, lambda b, s: (0, 0)),
            pl.BlockSpec(seg.shape, lambda b, s: (0, 0)),
        ],
        out_specs=out_specs,
        out_shape=out_shape,
        scratch_shapes=[pltpu.VMEM((1, nh), F32)],
        compiler_params=_params(("arbitrary", "arbitrary")),
        name="fox_qkv",
    )(*_residual_args(x, pending, modr), modr, modr, gm, w_all, bf, place, seg)
    return outs[0], outs[1], outs[2], outs[3], (outs[4] if pending else x)


def _attn_kernel(q_ref, k_ref, v_ref, st_ref, o_ref, m_s, acc_s, level_s):
    hp = pl.program_id(1)
    qi = pl.program_id(2)
    tk = ATTN_TILE
    tq = q_ref.shape[2]
    subs = tq // tk
    m_s[...] = jnp.full(m_s.shape, NEG_BIG, F32)
    acc_s[...] = jnp.zeros(acc_s.shape, F32)
    row = lax.broadcasted_iota(I32, (tk, tk), 0)
    col = lax.broadcasted_iota(I32, (tk, tk), 1)

    def step(kb, lo, hi, diagonal):
        start = pl.multiple_of(kb * tk, tk)
        rows = slice(lo * tk, hi * tk)
        for hh in range(HEADS_PER_STEP):
            kk = k_ref[0, hh, pl.ds(start, tk), :]
            vv = v_ref[0, hh, pl.ds(start, tk), :]
            z = lax.dot_general(q_ref[0, hh, rows, :], kk, NT_DIMS, preferred_element_type=F32)
            if diagonal:
                masked = jnp.where(row >= col, z[:tk], NEG_BIG)
                z = masked if hi - lo == 1 else jnp.concatenate([masked, z[tk:]], axis=0)
            m_prev = m_s[hh, rows, :]
            m_new = jnp.maximum(m_prev, jnp.max(z, axis=1, keepdims=True))
            p = jnp.exp2(z - jnp.concatenate([m_new] * (tk // LANES), axis=1))
            acc_s[hh, rows, :] = (jnp.exp2(m_prev - m_new) * acc_s[hh, rows, :]
                                  + jnp.dot(p.astype(BF16), vv, preferred_element_type=F32))
            m_s[hh, rows, :] = m_new

    for j in range(subs):
        step(subs * qi + j, j, subs, True)

    st = st_ref[0]
    nt = st.shape[0]
    stat = lax.broadcasted_iota(I32, st.shape, 1)
    q_top = jnp.max(jnp.where(stat == 0, st, 0.0), axis=1)
    k_top = jnp.max(jnp.where(stat == 1, st, 0.0), axis=1)
    b_top = jnp.max(jnp.where(stat == 2, st, NEG_BIG), axis=1)
    tile = lax.broadcasted_iota(I32, (nt, LANES), 0)
    head = lax.broadcasted_iota(I32, (nt, LANES), 1)
    level = jnp.zeros((nt, LANES), F32)
    for hh in range(HEADS_PER_STEP):
        earlier = (head == HEADS_PER_STEP * hp + hh) & (tile < subs * qi)
        for j in range(subs):
            q_here = jnp.max(jnp.where(tile == subs * qi + j, q_top, 0.0), axis=0, keepdims=True)
            floor = jnp.min(m_s[hh, j * tk:(j + 1) * tk, :]) - UNDERFLOW_LOG2
            needed = (NORM_SLACK * q_here * k_top + b_top + 1.0 >= floor) & earlier
            level = jnp.maximum(level, jnp.where(needed, float(j + 1), 0.0))
    level = jnp.max(level, axis=1, keepdims=True).astype(I32)
    for t in range(nt):
        level_s[t] = level[t, 0]

    spans = sorted({1, max(subs // 2, 1), subs})

    def body(kb, carry):
        need = level_s[kb]
        below = 0
        for hi in spans:
            @pl.when((need > below) & (need <= hi))
            def _():
                step(kb, 0, hi, False)
            below = hi
        return carry

    lax.fori_loop(0, subs * qi, body, 0)

    lane = lax.broadcasted_iota(I32, (tq, LANES), 1)
    outs = []
    for hh in range(HEADS_PER_STEP):
        a = acc_s[hh]
        outs.append(a / a[:, _aug_lane(hh):_aug_lane(hh) + 1])
    o_ref[0] = jnp.where(lane < HEAD_DIM, outs[0], outs[1]).astype(BF16)


def _attn_call(q, k, v, stats):
    batch, nh, seq, _ = q.shape
    tq = ATTN_QUERY_SUBTILES * ATTN_TILE
    kv_spec = pl.BlockSpec((1, HEADS_PER_STEP, seq, LANES), lambda b, h, i: (b, h, 0, 0))
    return pl.pallas_call(
        _attn_kernel,
        grid=(batch, nh // HEADS_PER_STEP, seq // tq),
        in_specs=[pl.BlockSpec((1, HEADS_PER_STEP, tq, LANES), lambda b, h, i: (b, h, i, 0)),
                  kv_spec, kv_spec,
                  pl.BlockSpec((1,) + stats.shape[1:], lambda b, h, i: (b, 0, 0, 0))],
        out_specs=pl.BlockSpec((1, tq, LANES), lambda b, h, i: (b, i, h)),
        out_shape=jax.ShapeDtypeStruct((batch, seq, nh * HEAD_DIM), BF16),
        scratch_shapes=[pltpu.VMEM((HEADS_PER_STEP, tq, LANES), F32)] * 2 + [pltpu.SMEM((stats.shape[1],), I32)],
        compiler_params=_params(("arbitrary", "arbitrary", "arbitrary")),
        name="fox_attention",
    )(q, k, v, stats)


def _wo_kernel(x_ref, o_ref, g1, sh2, sc2, gf, wo_ref, rwt_ref, rb_ref, xo_ref, hp_ref, lg_ref):
    m = jnp.dot(o_ref[0], wo_ref[...], preferred_element_type=F32)
    xn = x_ref[0] + g1[0] * m
    xo_ref[0] = xn
    hp, lg = _ffn_prenorm(xn, gf[...], sc2[0], sh2[0], rwt_ref[...], rb_ref[...])
    hp_ref[...] = hp
    lg_ref[...] = lg


def _wo_call(layer, x, o, modr, gf, wo, rwt, rb):
    batch, seq, d = x.shape
    e = rwt.shape[0]
    ts = SEQ_TILE
    ns = seq // ts
    row, vec, x_spec, out_specs = _row_layer_specs(layer, batch, ns, ts, d, e)
    return pl.pallas_call(
        _wo_kernel,
        grid=(batch, ns),
        in_specs=[
            x_spec, x_spec, row(2), row(3), row(4), vec,
            pl.BlockSpec((d, d), lambda b, s: (0, 0)),
            pl.BlockSpec((e, d), lambda b, s: (0, 0)),
            pl.BlockSpec((e, 1), lambda b, s: (0, 0)),
        ],
        out_specs=out_specs,
        out_shape=[
            jax.ShapeDtypeStruct((batch, seq, d), F32),
            jax.ShapeDtypeStruct((batch * seq, d // 2), U32),
            jax.ShapeDtypeStruct((e, batch * seq), F32),
        ],
        compiler_params=_params(("arbitrary", "arbitrary")),
        name="fox_out",
    )(x, o, modr, modr, modr, gf, wo, rwt, rb)


def _route_kernel(lg_ref, eid_ref, gate_ref, pos_ref, cnt_ref, carry):
    i = pl.program_id(0)

    @pl.when(i == 0)
    def _():
        carry[...] = jnp.zeros(carry.shape, F32)

    l = lg_ref[...]
    e, ts = l.shape
    eidx = lax.broadcasted_iota(I32, (e, ts), 0)
    work = l
    top_v, top_i, hot = [], [], []
    for _ in range(TOP_K):
        m = jnp.max(work, axis=0, keepdims=True)
        sel = jnp.min(jnp.where(work == m, eidx, e), axis=0, keepdims=True)
        o = eidx == sel
        top_v.append(m)
        top_i.append(sel)
        hot.append(o)
        work = jnp.where(o, -jnp.inf, work)
    ex = [jnp.exp(v - top_v[0]) for v in top_v]
    den = ex[0] + ex[1] + ex[2] + ex[3]
    chosen = jnp.where(hot[0] | hot[1] | hot[2] | hot[3], 1.0, 0.0)
    r = lax.broadcasted_iota(I32, (ts, ts), 0)
    c = lax.broadcasted_iota(I32, (ts, ts), 1)
    before = jnp.where(r < c, 1.0, 0.0).astype(BF16)
    rank = jnp.dot(chosen.astype(BF16), before, preferred_element_type=F32) + carry[...]
    pos = [jnp.sum(jnp.where(o, rank, 0.0), axis=0, keepdims=True) for o in hot]
    carry[...] = carry[...] + jnp.sum(chosen, axis=1, keepdims=True)
    eid_ref[...] = jnp.concatenate(top_i, axis=0)
    gate_ref[...] = jnp.concatenate([x / den for x in ex], axis=0)
    pos_ref[...] = jnp.concatenate(pos, axis=0).astype(I32)
    cnt_ref[...] = jnp.broadcast_to(carry[...], cnt_ref.shape)


def _route_call(logits_t):
    e, t = logits_t.shape
    ts = ROUTE_TILE
    out = pl.BlockSpec((TOP_K, ts), lambda i: (0, i))
    return pl.pallas_call(
        _route_kernel,
        grid=(t // ts,),
        in_specs=[pl.BlockSpec((e, ts), lambda i: (0, i))],
        out_specs=[out, out, out, pl.BlockSpec((e, LANES), lambda i: (0, 0))],
        out_shape=[
            jax.ShapeDtypeStruct((TOP_K, t), I32),
            jax.ShapeDtypeStruct((TOP_K, t), F32),
            jax.ShapeDtypeStruct((TOP_K, t), I32),
            jax.ShapeDtypeStruct((e, LANES), F32),
        ],
        scratch_shapes=[pltpu.VMEM((e, 1), F32)],
        compiler_params=_params(("arbitrary",)),
        name="route_topk",
    )(logits_t)


def _dest_kernel(pstart_ref, eid_ref, pos_ref, dest_ref):
    eid = eid_ref[...]
    dest = pos_ref[...]
    for e in range(pstart_ref.shape[0]):
        dest = dest + jnp.where(eid == e, pstart_ref[e], 0)
    dest_ref[...] = dest


def _dest_call(pstart, eid, pos):
    k, t = eid.shape
    tt = min(DEST_TILE, t)
    spec = pl.BlockSpec((k, tt), lambda i, ps: (0, i))
    grid_spec = pltpu.PrefetchScalarGridSpec(
        num_scalar_prefetch=1, grid=(t // tt,), in_specs=[spec, spec], out_specs=spec)
    return pl.pallas_call(
        _dest_kernel,
        grid_spec=grid_spec,
        out_shape=jax.ShapeDtypeStruct((k, t), I32),
        compiler_params=_params(("arbitrary",)),
        name="route_dest",
    )(pstart, eid, pos)


def _sc_worker_chunks(total_chunks):
    worker = lax.axis_index("s") * SC_CORES + lax.axis_index("c")
    per_worker = total_chunks // SC_WORKERS
    return worker * per_worker, per_worker


def _sc_dispatch_call(hp, dest3, rows):
    t, w = hp.shape
    nchunk = dest3.shape[0]
    mesh = plsc.VectorSubcoreMesh(core_axis_name="c", subcore_axis_name="s")

    assert nchunk % (SC_BUFFERS * SC_WORKERS) == 0

    @functools.partial(
        pl.kernel, mesh=mesh, out_type=jax.ShapeDtypeStruct((rows, w), U32),
        scratch_types=[pltpu.VMEM((SC_BUFFERS, TOP_K, SC_CHUNK), I32),
                       pltpu.VMEM((SC_BUFFERS, SC_CHUNK, w), U32),
                       pltpu.SemaphoreType.DMA((SC_BUFFERS,)), pltpu.SemaphoreType.DMA((SC_BUFFERS,))],
        name="moe_dispatch")
    def dispatch(hp_hbm, dest_hbm, xs_hbm, idx_v, rows_v, load_sem, scatter_sem):
        first, count = _sc_worker_chunks(nchunk)

        def load(c, slot):
            return pltpu.make_async_copy(hp_hbm.at[pl.ds(c * SC_CHUNK, SC_CHUNK)], rows_v.at[slot],
                                         load_sem.at[slot])

        def scatter(slot, k):
            return pltpu.make_async_copy(rows_v.at[slot], xs_hbm.at[idx_v.at[slot, k]],
                                         scatter_sem.at[slot])

        def start_load(c, slot):
            pltpu.sync_copy(dest_hbm.at[c], idx_v.at[slot])
            load(c, slot).start()

        start_load(first, 0)

        @pl.loop(0, count, step=SC_BUFFERS)
        def _(j):
            for b in range(SC_BUFFERS):
                c = first + j + b
                nxt = (b + 1) % SC_BUFFERS

                @pl.when(j + b + 1 < count)
                def _():
                    @pl.when(j + b >= 1)
                    def _():
                        for k in range(TOP_K):
                            scatter(nxt, k).wait()
                    start_load(c + 1, nxt)

                load(c, b).wait()
                for k in range(TOP_K):
                    scatter(b, k).start()

        for slot in range(SC_BUFFERS):
            for k in range(TOP_K):
                scatter(slot, k).wait()

    return dispatch(hp, dest3)


def _sc_gather_call(ys, idx):
    n = idx.shape[0]
    _, w = ys.shape
    nchunk = n // SC_CHUNK
    mesh = plsc.VectorSubcoreMesh(core_axis_name="c", subcore_axis_name="s")

    assert nchunk % (SC_BUFFERS * SC_WORKERS) == 0

    @functools.partial(
        pl.kernel, mesh=mesh, out_type=jax.ShapeDtypeStruct((n, w), U32),
        scratch_types=[pltpu.VMEM((SC_BUFFERS, SC_CHUNK), I32), pltpu.VMEM((SC_BUFFERS, SC_CHUNK, w), U32),
                       pltpu.SemaphoreType.DMA((SC_BUFFERS,)), pltpu.SemaphoreType.DMA((SC_BUFFERS,))],
        name="moe_gather")
    def gather(ys_hbm, idx_hbm, out_hbm, idx_v, rows_v, gather_sem, write_sem):
        first, count = _sc_worker_chunks(nchunk)

        def fetch(slot):
            return pltpu.make_async_copy(ys_hbm.at[idx_v.at[slot]], rows_v.at[slot], gather_sem.at[slot])

        def write(c, slot):
            return pltpu.make_async_copy(rows_v.at[slot], out_hbm.at[pl.ds(c * SC_CHUNK, SC_CHUNK)],
                                         write_sem.at[slot])

        def start_fetch(c, slot):
            pltpu.sync_copy(idx_hbm.at[pl.ds(c * SC_CHUNK, SC_CHUNK)], idx_v.at[slot])
            fetch(slot).start()

        start_fetch(first, 0)

        @pl.loop(0, count, step=SC_BUFFERS)
        def _(j):
            for b in range(SC_BUFFERS):
                c = first + j + b
                nxt = (b + 1) % SC_BUFFERS

                @pl.when(j + b + 1 < count)
                def _():
                    @pl.when(j + b >= 1)
                    def _():
                        write(c - 1, nxt).wait()
                    start_fetch(c + 1, nxt)

                fetch(b).wait()
                write(c, b).start()

        for b in range(SC_BUFFERS):
            write(first + count - SC_BUFFERS + b, b).wait()

    return gather(ys, idx)


def _expert_kernel(be_ref, nv_ref, xs_ref, w1_ref, b1_ref, w2_ref, b2_ref, ys_ref, w1s, w2s):
    b = pl.program_id(0)
    e = be_ref[b]
    prev = be_ref[jnp.maximum(b - 1, 0)]
    nvalid = nv_ref[b]
    f = w2_ref.shape[1]

    @pl.when((b == 0) | (e != prev))
    def _():
        w1s[...] = w1_ref[0].astype(BF16)
        w2s[...] = w2_ref[0].astype(BF16)

    sub = EXPERT_SUBROWS
    for r0 in range(0, xs_ref.shape[0], sub):
        @pl.when(nvalid > r0)
        def _():
            rows = r0 + lax.broadcasted_iota(I32, (sub, 1), 0)
            xw = jnp.where(rows < nvalid, xs_ref[r0:r0 + sub, :], jnp.uint32(0))
            x = _unpack_pairs(xw).astype(BF16)
            gu = jnp.dot(x, w1s[...], preferred_element_type=F32) + b1_ref[0]
            gate = jnp.minimum(gu[:, :f], SWIGLU_LIMIT)
            up = jnp.clip(gu[:, f:], -SWIGLU_LIMIT, SWIGLU_LIMIT)
            act = (up + 1.0) * (gate * _sigmoid(SWIGLU_ALPHA * gate))
            y = jnp.dot(act.astype(BF16), w2s[...], preferred_element_type=F32) + b2_ref[0]
            ys_ref[r0:r0 + sub, :] = _pack_pairs(y)

        @pl.when(nvalid <= r0)
        def _():
            ys_ref[r0:r0 + sub, :] = jnp.zeros((sub, ys_ref.shape[1]), U32)


def _expert_call(layer, block_e, nvalid, xs, w_in, b_in, w_out, b_out):
    rows, w = xs.shape
    depth, ne, d, f2 = w_in.shape
    f = f2 // 2
    bm = EXPERT_ROWS
    x_spec = pl.BlockSpec((bm, w), lambda b, be, nv: (b, 0))
    grid_spec = pltpu.PrefetchScalarGridSpec(
        num_scalar_prefetch=2,
        grid=(rows // bm,),
        in_specs=[
            x_spec,
            pl.BlockSpec((1, d, f2), lambda b, be, nv: (layer * ne + be[b], 0, 0)),
            pl.BlockSpec((1, 1, f2), lambda b, be, nv: (layer * ne + be[b], 0, 0)),
            pl.BlockSpec((1, f, d), lambda b, be, nv: (layer * ne + be[b], 0, 0)),
            pl.BlockSpec((1, 1, d), lambda b, be, nv: (layer * ne + be[b], 0, 0)),
        ],
        out_specs=x_spec,
        scratch_shapes=[pltpu.VMEM((d, f2), BF16), pltpu.VMEM((f, d), BF16)],
    )
    return pl.pallas_call(
        _expert_kernel,
        grid_spec=grid_spec,
        out_shape=jax.ShapeDtypeStruct((rows, w), U32),
        compiler_params=_params(("arbitrary",)),
        name="moe_experts",
    )(block_e, nvalid, xs,
      w_in.reshape(depth * ne, d, f2), b_in.reshape(depth * ne, 1, f2),
      w_out.reshape(depth * ne, f, d), b_out.reshape(depth * ne, 1, d))


def _final_kernel(x_ref, yg_ref, gt_ref, g2, fg, xo_ref):
    xn = _moe_residual(x_ref, yg_ref, gt_ref, g2)
    r = lax.rsqrt(jnp.mean(xn * xn, axis=-1, keepdims=True) + EPS)
    xo_ref[0] = (xn * r) * fg[...]


def _final_call(x, pending, modr, fg):
    batch, seq, d = x.shape
    ts = COMBINE_TILE
    ns = seq // ts
    return pl.pallas_call(
        _final_kernel,
        grid=(batch, ns),
        in_specs=_residual_specs(pending, batch, ns, ts, d) + [pl.BlockSpec((1, d), lambda b, s: (0, 0))],
        out_specs=pl.BlockSpec((1, ts, d), lambda b, s: (b, s, 0)),
        out_shape=jax.ShapeDtypeStruct((batch, seq, d), F32),
        compiler_params=_params(("arbitrary", "arbitrary")),
        name="moe_combine_final",
    )(*_residual_args(x, pending, modr), fg)


def _moe(layer, hp, logits_t, w_in, b_in, w_out, b_out):
    t, w = hp.shape
    ne = logits_t.shape[0]
    bm = EXPERT_ROWS
    eid, gate, pos, cnt = _route_call(logits_t)
    counts = cnt[:, 0].astype(I32)
    nblk = (counts + bm - 1) // bm
    bend = jnp.cumsum(nblk)
    bstart = bend - nblk
    pstart = (bstart * bm).astype(I32)
    n_blocks = (t * TOP_K) // bm + ne
    blocks = jnp.arange(n_blocks, dtype=I32)
    block_e = jnp.minimum(jnp.sum(blocks[:, None] >= bend[None, :], axis=1), ne - 1).astype(I32)
    mine = block_e[:, None] == jnp.arange(ne, dtype=I32)[None, :]
    left = jnp.sum(jnp.where(mine, counts[None, :] - (blocks[:, None] - bstart[None, :]) * bm, 0), axis=1)
    nvalid = jnp.where(blocks < bend[-1], jnp.clip(left, 0, bm), 0).astype(I32)
    dest = _dest_call(pstart, eid, pos)
    dest3 = dest.reshape(TOP_K, t // SC_CHUNK, SC_CHUNK).transpose(1, 0, 2)
    xs = _sc_dispatch_call(hp, dest3, n_blocks * bm)
    ys = _expert_call(layer, block_e, nvalid, xs, w_in, b_in, w_out, b_out)
    yg = _sc_gather_call(ys, dest.reshape(TOP_K * t))
    return {"yg": yg.reshape(TOP_K, t, w), "gate_t": gate.T, "layer": layer}


def kernel(x, c, norm_mix_g, norm_ffn_g, ada_w, ada_b, pool_w, pool_scale, fox_w_in, fox_b_f, fox_w_o,
           router_w, router_b, exp_w_in, exp_b_in, exp_w_out, exp_b_out, final_g):
    batch, seq, d = x.shape
    depth = ada_w.shape[0]
    mod = _ada_call(c, ada_w, ada_b)
    modr = mod.reshape(depth * batch * 6, 1, d)
    fg = final_g.reshape(1, d)
    pending = None
    for i in range(depth):
        gm = norm_mix_g[i].reshape(1, d)
        gf = norm_ffn_g[i].reshape(1, d)
        rwt = router_w[i].T
        rb = router_b[i].reshape(-1, 1)
        j = i // 2
        if i % 2 == 0:
            x, hp, lg = _pool_call(i, x, pending, modr, gm, gf, pool_w[j], pool_scale[j].reshape(1, d),
                                   rwt, rb)
        else:
            q, k, v, stats, x = _qkv_call(i, x, pending, modr, gm, fox_w_in[j], fox_b_f[j].reshape(1, -1))
            o = _attn_call(q, k, v, stats)
            x, hp, lg = _wo_call(i, x, o, modr, gf, fox_w_o[j].astype(BF16), rwt, rb)
        pending = _moe(i, hp, lg, exp_w_in, exp_b_in, exp_w_out, exp_b_out)
    return _final_call(x, pending, modr, fg)
```

```python
import functools

import jax
import jax.numpy as jnp
import numpy as np
from jax import lax
from jax.experimental import pallas as pl
from jax.experimental.pallas import tpu as pltpu
from jax.experimental.pallas import tpu_sc as plsc

F32 = jnp.float32
BF16 = jnp.bfloat16
U32 = jnp.uint32
I32 = jnp.int32

POOL_WINDOWS = (2, 4, 8, 16)
POOL_HALO = 16
HEAD_DIM = 64
HEADS_PER_STEP = 2
TOP_K = 4
SWIGLU_LIMIT = 7.0
SWIGLU_ALPHA = 1.702
EPS = 1e-6
NEG_BIG = -1e30
LOG2E = 1.4426950408889634
BIAS_PARTS = 3
UNDERFLOW_LOG2 = 160.0
NORM_SLACK = 1.02

LANES = 128
STAT_ROWS = 8
SEQ_TILE = 512
QKV_SUBROWS = 256
ATTN_TILE = 512
ATTN_QUERY_SUBTILES = 4
ROUTE_TILE = 512
DEST_TILE = 4096
EXPERT_ROWS = 1024
EXPERT_SUBROWS = 512
COMBINE_TILE = 512
ADA_COLS = 1536
VMEM_LIMIT_BYTES = 56 * 1024 * 1024

SC_CORES = 2
SC_SUBCORES = 16
SC_WORKERS = SC_CORES * SC_SUBCORES
SC_CHUNK = 64
SC_BUFFERS = 2

NT_DIMS = (((1,), (1,)), ((), ()))


def _params(sem, vmem=VMEM_LIMIT_BYTES):
    return pltpu.CompilerParams(dimension_semantics=sem, vmem_limit_bytes=vmem)


def _sigmoid(z):
    return 1.0 / (1.0 + jnp.exp(-z))


def _norm_mod(x, g, scale, shift):
    r = lax.rsqrt(jnp.mean(x * x, axis=-1, keepdims=True) + EPS)
    return (x * r) * (g * (1.0 + scale)) + shift


def _pack_pairs(y):
    w = y.shape[1] // 2
    hi = lax.bitcast_convert_type(y[:, :w].astype(BF16).astype(F32), U32)
    lo = lax.bitcast_convert_type(y[:, w:].astype(BF16).astype(F32), U32)
    return hi | (lo >> 16)


def _unpack_pairs(w):
    hi = lax.bitcast_convert_type(w & jnp.uint32(0xFFFF0000), F32)
    lo = lax.bitcast_convert_type(w << 16, F32)
    return jnp.concatenate([hi, lo], axis=1)


def _split_bf16(x, parts):
    out = []
    r = x
    for _ in range(parts):
        p = r.astype(BF16)
        out.append(p)
        r = r - p.astype(F32)
    return out


def _ffn_prenorm(xn, gf, sc2, sh2, rwt, rb):
    h2 = _norm_mod(xn, gf, sc2, sh2)
    h_hi, h_lo = _split_bf16(h2, 2)
    w_hi, w_lo = _split_bf16(rwt, 2)
    lg = (lax.dot_general(w_hi, h_hi, NT_DIMS, preferred_element_type=F32)
          + lax.dot_general(w_lo, h_hi, NT_DIMS, preferred_element_type=F32)
          + lax.dot_general(w_hi, h_lo, NT_DIMS, preferred_element_type=F32)) + rb
    return _pack_pairs(h2), lg


def _mod_spec(layer, batch, j):
    def index(b, s):
        return ((layer * batch + b) * 6 + j, 0, 0)
    return index


def _moe_residual(x_ref, yg_ref, gt_ref, g2_ref, rows=slice(None)):
    gt = gt_ref[rows, :]
    moe = gt[:, 0:1] * _unpack_pairs(yg_ref[0, rows, :])
    for k in range(1, TOP_K):
        moe = moe + gt[:, k:k + 1] * _unpack_pairs(yg_ref[k, rows, :])
    return x_ref[0, rows, :] + g2_ref[0] * moe


def _residual_in(refs, pending, rows=slice(None)):
    if pending:
        return _moe_residual(*refs[:4], rows=rows)
    return refs[0][0, rows, :]


def _residual_rest(refs, pending):
    return refs[4:] if pending else refs[1:]


def _residual_specs(pending, batch, ns, ts, d):
    specs = [pl.BlockSpec((1, ts, d), lambda b, s: (b, s, 0))]
    if pending:
        _, t, w = pending["yg"].shape
        specs += [
            pl.BlockSpec((TOP_K, ts, w), lambda b, s: (0, b * ns + s, 0)),
            pl.BlockSpec((ts, TOP_K), lambda b, s: (b * ns + s, 0)),
            pl.BlockSpec((1, 1, d), _mod_spec(pending["layer"], batch, 5)),
        ]
    return specs


def _residual_args(x, pending, modr):
    return (x, pending["yg"], pending["gate_t"], modr) if pending else (x,)


def _ada_kernel(ct_ref, w_ref, b_ref, o_ref):
    ct = ct_ref[...]
    cond = ct * _sigmoid(ct)
    w = w_ref[0]
    rows = []
    for b in range(ct.shape[1]):
        rows.append(jnp.sum(w * cond[:, b:b + 1], axis=0, keepdims=True))
    o_ref[0] = jnp.concatenate(rows, axis=0) + b_ref[0]


def _ada_call(c, ada_w, ada_b):
    depth, d, n = ada_w.shape
    batch = c.shape[0]
    tn = ADA_COLS
    return pl.pallas_call(
        _ada_kernel,
        grid=(depth, n // tn),
        in_specs=[
            pl.BlockSpec((d, batch), lambda i, j: (0, 0)),
            pl.BlockSpec((1, d, tn), lambda i, j: (i, 0, j)),
            pl.BlockSpec((1, 1, tn), lambda i, j: (i, 0, j)),
        ],
        out_specs=pl.BlockSpec((1, batch, tn), lambda i, j: (i, 0, j)),
        out_shape=jax.ShapeDtypeStruct((depth, batch, n), F32),
        compiler_params=_params(("arbitrary", "arbitrary")),
        name="ada_mod",
    )(c.T, ada_w, ada_b.reshape(depth, 1, n))


def _pool_kernel(pending, *refs):
    x = _residual_in(refs, pending)
    refs = _residual_rest(refs, pending)
    sh1, sc1, g1, sh2, sc2, gm, gf, pw_ref, ps_ref, rwt_ref, rb_ref, xo_ref, hp_ref, lg_ref, hbuf = refs
    s = pl.program_id(1)
    ts, d = x.shape
    pg = d // len(POOL_WINDOWS)
    h = _norm_mod(x, gm[...], sc1[0], sh1[0])

    @pl.when(s == 0)
    def _():
        hbuf[0:POOL_HALO, :] = jnp.zeros((POOL_HALO, d), F32)

    @pl.when(s > 0)
    def _():
        hbuf[0:POOL_HALO, :] = hbuf[ts:ts + POOL_HALO, :]

    hbuf[POOL_HALO:POOL_HALO + ts, :] = h
    pos = s * ts + lax.broadcasted_iota(I32, (ts, 1), 0)
    ys = []
    for g, w in enumerate(POOL_WINDOWS):
        lo = g * pg
        hg = h[:, lo:lo + pg]
        win = hbuf[:, lo:lo + pg]
        span = 1
        while span < w:
            win = win + pltpu.roll(win, span, 0)
            span *= 2
        acc = win[POOL_HALO:, :]
        cnt = jnp.minimum(pos + 1, w).astype(F32)
        dlt = acc / cnt - hg
        ys.append(jnp.dot(dlt.astype(BF16), pw_ref[g].astype(BF16), preferred_element_type=F32))
    y = jnp.concatenate(ys, axis=1) * ps_ref[...]
    xn = x + g1[0] * y
    xo_ref[0] = xn
    hp, lg = _ffn_prenorm(xn, gf[...], sc2[0], sh2[0], rwt_ref[...], rb_ref[...])
    hp_ref[...] = hp
    lg_ref[...] = lg


def _row_layer_specs(layer, batch, ns, ts, d, e):
    row = lambda j: pl.BlockSpec((1, 1, d), _mod_spec(layer, batch, j))
    vec = pl.BlockSpec((1, d), lambda b, s: (0, 0))
    x_spec = pl.BlockSpec((1, ts, d), lambda b, s: (b, s, 0))
    out_specs = [
        x_spec,
        pl.BlockSpec((ts, d // 2), lambda b, s: (b * ns + s, 0)),
        pl.BlockSpec((e, ts), lambda b, s: (0, b * ns + s)),
    ]
    return row, vec, x_spec, out_specs


def _pool_call(layer, x, pending, modr, gm, gf, pw, ps, rwt, rb):
    batch, seq, d = x.shape
    e = rwt.shape[0]
    ts = SEQ_TILE
    ns = seq // ts
    g = len(POOL_WINDOWS)
    pg = d // g
    row, vec, _, out_specs = _row_layer_specs(layer, batch, ns, ts, d, e)
    return pl.pallas_call(
        functools.partial(_pool_kernel, bool(pending)),
        grid=(batch, ns),
        in_specs=_residual_specs(pending, batch, ns, ts, d) + [
            row(0), row(1), row(2), row(3), row(4), vec, vec,
            pl.BlockSpec((g, pg, pg), lambda b, s: (0, 0, 0)),
            vec,
            pl.BlockSpec((e, d), lambda b, s: (0, 0)),
            pl.BlockSpec((e, 1), lambda b, s: (0, 0)),
        ],
        out_specs=out_specs,
        out_shape=[
            jax.ShapeDtypeStruct((batch, seq, d), F32),
            jax.ShapeDtypeStruct((batch * seq, d // 2), U32),
            jax.ShapeDtypeStruct((e, batch * seq), F32),
        ],
        scratch_shapes=[pltpu.VMEM((ts + POOL_HALO, d), F32)],
        compiler_params=_params(("arbitrary", "arbitrary")),
        name="pool_layer",
    )(*_residual_args(x, pending, modr), modr, modr, modr, modr, modr, gm, gf, pw, ps, rwt, rb)


def _aug_lane(head):
    return HEAD_DIM if head % 2 == 0 else 0


def _bias_placement(nh):
    place = np.zeros((BIAS_PARTS * nh, nh * LANES), np.float32)
    for j in range(BIAS_PARTS):
        for h in range(nh):
            place[j * nh + h, h * LANES + _aug_lane(h) + j] = 1.0
    return jnp.asarray(place, BF16)


def _qkv_kernel(pending, *refs):
    rest = _residual_rest(refs, pending)
    sh1, sc1, gm, w_ref, bf_ref, place_ref, seg_ref, q_ref, k_ref, v_ref, st_ref = rest[:11]
    carry = refs[-1]
    s = pl.program_id(1)
    ts = q_ref.shape[2]
    d = gm.shape[1]
    nh = bf_ref.shape[1]
    sub = QKV_SUBROWS

    @pl.when(s == 0)
    def _():
        carry[...] = jnp.zeros(carry.shape, F32)

    r = lax.broadcasted_iota(I32, (sub, sub), 0)
    c = lax.broadcasted_iota(I32, (sub, sub), 1)
    lower = jnp.where(r >= c, 1.0, 0.0).astype(BF16)
    lane = lax.broadcasted_iota(I32, (sub, LANES), 1)
    q_top = k_top = jnp.zeros((1, LANES), F32)
    b_top = jnp.full((1, nh), NEG_BIG, F32)
    for r0 in range(0, ts, sub):
        rows = slice(r0, r0 + sub)
        x = _residual_in(refs, pending, rows)
        if pending:
            rest[11][0, rows, :] = x
        h = _norm_mod(x, gm[...], sc1[0], sh1[0]).astype(BF16)
        fl = jnp.dot(h, w_ref[:, 3 * d:], preferred_element_type=F32)[:, :nh] + bf_ref[...]
        proj = jnp.dot(h, w_ref[:, :3 * d], preferred_element_type=F32)
        logf = jnp.minimum(fl, 0.0) - jnp.log(1.0 + jnp.exp(-jnp.abs(fl)))
        parts = jnp.concatenate(_split_bf16(logf, 3), axis=1)
        cs = jnp.dot(lower, parts, preferred_element_type=F32)
        cum = cs[:, :nh] + cs[:, nh:2 * nh] + cs[:, 2 * nh:] + carry[...]
        carry[...] = cum[sub - 1:sub, :]
        bias = jnp.concatenate(_split_bf16(cum * (-LOG2E), BIAS_PARTS), axis=1)
        k_aug = jnp.dot(bias, place_ref[...], preferred_element_type=F32)
        q_all = proj[:, :d] * (HEAD_DIM ** -0.5 * LOG2E)
        k_all = proj[:, d:2 * d]
        q_norm = jnp.sqrt(jnp.dot((q_all * q_all).astype(BF16), seg_ref[...], preferred_element_type=F32))
        k_norm = jnp.sqrt(jnp.dot((k_all * k_all).astype(BF16), seg_ref[...], preferred_element_type=F32))
        q_top = jnp.maximum(q_top, jnp.max(q_norm, axis=0, keepdims=True))
        k_top = jnp.maximum(k_top, jnp.max(k_norm, axis=0, keepdims=True))
        b_top = jnp.maximum(b_top, jnp.max(cum * (-LOG2E), axis=0, keepdims=True))
        for hd in range(nh):
            lo = (hd // 2) * LANES
            a0 = _aug_lane(hd)
            real = (lane < HEAD_DIM) if hd % 2 == 0 else (lane >= HEAD_DIM)
            q_aug = jnp.where((lane >= a0) & (lane < a0 + BIAS_PARTS), 1.0, 0.0)
            v_aug = jnp.where(lane == a0, 1.0, 0.0)
            q_ref[0, hd, rows, :] = jnp.where(real, q_all[:, lo:lo + LANES], q_aug).astype(BF16)
            k_ref[0, hd, rows, :] = jnp.where(real, proj[:, d + lo:d + lo + LANES],
                                              k_aug[:, hd * LANES:(hd + 1) * LANES]).astype(BF16)
            v_ref[0, hd, rows, :] = jnp.where(real, proj[:, 2 * d + lo:2 * d + lo + LANES],
                                              v_aug).astype(BF16)
    b_row = jnp.concatenate([b_top, jnp.zeros((1, LANES - nh), F32)], axis=1)
    st_ref[0, 0] = jnp.concatenate([q_top, k_top, b_row, jnp.zeros((STAT_ROWS - 3, LANES), F32)], axis=0)


def _qkv_call(layer, x, pending, modr, gm, w_in, bf):
    batch, seq, d = x.shape
    nh = bf.shape[1]
    ts = SEQ_TILE
    ns = seq // ts
    w_all = jnp.concatenate([w_in, jnp.zeros((d, LANES - nh), w_in.dtype)], axis=1).astype(BF16)
    place = _bias_placement(nh)
    seg = jnp.asarray(np.arange(d)[:, None] // HEAD_DIM == np.arange(LANES)[None, :], BF16)
    row = lambda j: pl.BlockSpec((1, 1, d), _mod_spec(layer, batch, j))
    head_spec = pl.BlockSpec((1, nh, ts, LANES), lambda b, s: (b, 0, s, 0))
    out_specs = [head_spec] * 3 + [pl.BlockSpec((1, 1, STAT_ROWS, LANES), lambda b, s: (b, s, 0, 0))]
    out_shape = [jax.ShapeDtypeStruct((batch, nh, seq, LANES), BF16)] * 3 + [
        jax.ShapeDtypeStruct((batch, ns, STAT_ROWS, LANES), F32)]
    if pending:
        out_specs = out_specs + [pl.BlockSpec((1, ts, d), lambda b, s: (b, s, 0))]
        out_shape = out_shape + [jax.ShapeDtypeStruct((batch, seq, d), F32)]
    outs = pl.pallas_call(
        functools.partial(_qkv_kernel, bool(pending)),
        grid=(batch, ns),
        in_specs=_residual_specs(pending, batch, ns, ts, d) + [
            row(0), row(1),
            pl.BlockSpec((1, d), lambda b, s: (0, 0)),
            pl.BlockSpec(w_all.shape, lambda b, s: (0, 0)),
            pl.BlockSpec((1, nh), lambda b, s: (0, 0)),
            pl.BlockSpec(place.shape, lambda b, s: (0, 0)),
            pl.BlockSpec(seg.shape, lambda b, s: (0, 0)),
        ],
        out_specs=out_specs,
        out_shape=out_shape,
        scratch_shapes=[pltpu.VMEM((1, nh), F32)],
        compiler_params=_params(("arbitrary", "arbitrary")),
        name="fox_qkv",
    )(*_residual_args(x, pending, modr), modr, modr, gm, w_all, bf, place, seg)
    return outs[0], outs[1], outs[2], outs[3], (outs[4] if pending else x)


def _attn_kernel(q_ref, k_ref, v_ref, st_ref, o_ref, m_s, acc_s, level_s):
    hp = pl.program_id(1)
    qi = pl.program_id(2)
    tk = ATTN_TILE
    tq = q_ref.shape[2]
    subs = tq // tk
    m_s[...] = jnp.full(m_s.shape, NEG_BIG, F32)
    acc_s[...] = jnp.zeros(acc_s.shape, F32)
    row = lax.broadcasted_iota(I32, (tk, tk), 0)
    col = lax.broadcasted_iota(I32, (tk, tk), 1)

    def step(kb, lo, hi, diagonal):
        start = pl.multiple_of(kb * tk, tk)
        rows = slice(lo * tk, hi * tk)
        for hh in range(HEADS_PER_STEP):
            kk = k_ref[0, hh, pl.ds(start, tk), :]
            vv = v_ref[0, hh, pl.ds(start, tk), :]
            z = lax.dot_general(q_ref[0, hh, rows, :], kk, NT_DIMS, preferred_element_type=F32)
            if diagonal:
                masked = jnp.where(row >= col, z[:tk], NEG_BIG)
                z = masked if hi - lo == 1 else jnp.concatenate([masked, z[tk:]], axis=0)
            m_prev = m_s[hh, rows, :]
            m_new = jnp.maximum(m_prev, jnp.max(z, axis=1, keepdims=True))
            p = jnp.exp2(z - jnp.concatenate([m_new] * (tk // LANES), axis=1))
            acc_s[hh, rows, :] = (jnp.exp2(m_prev - m_new) * acc_s[hh, rows, :]
                                  + jnp.dot(p.astype(BF16), vv, preferred_element_type=F32))
            m_s[hh, rows, :] = m_new

    for j in range(subs):
        step(subs * qi + j, j, subs, True)

    st = st_ref[0]
    nt = st.shape[0]
    stat = lax.broadcasted_iota(I32, st.shape, 1)
    q_top = jnp.max(jnp.where(stat == 0, st, 0.0), axis=1)
    k_top = jnp.max(jnp.where(stat == 1, st, 0.0), axis=1)
    b_top = jnp.max(jnp.where(stat == 2, st, NEG_BIG), axis=1)
    tile = lax.broadcasted_iota(I32, (nt, LANES), 0)
    head = lax.broadcasted_iota(I32, (nt, LANES), 1)
    level = jnp.zeros((nt, LANES), F32)
    for hh in range(HEADS_PER_STEP):
        earlier = (head == HEADS_PER_STEP * hp + hh) & (tile < subs * qi)
        for j in range(subs):
            q_here = jnp.max(jnp.where(tile == subs * qi + j, q_top, 0.0), axis=0, keepdims=True)
            floor = jnp.min(m_s[hh, j * tk:(j + 1) * tk, :]) - UNDERFLOW_LOG2
            needed = (NORM_SLACK * q_here * k_top + b_top + 1.0 >= floor) & earlier
            level = jnp.maximum(level, jnp.where(needed, float(j + 1), 0.0))
    level = jnp.max(level, axis=1, keepdims=True).astype(I32)
    for t in range(nt):
        level_s[t] = level[t, 0]

    spans = sorted({1, max(subs // 2, 1), subs})

    def body(kb, carry):
        need = level_s[kb]
        below = 0
        for hi in spans:
            @pl.when((need > below) & (need <= hi))
            def _():
                step(kb, 0, hi, False)
            below = hi
        return carry

    lax.fori_loop(0, subs * qi, body, 0)

    lane = lax.broadcasted_iota(I32, (tq, LANES), 1)
    outs = []
    for hh in range(HEADS_PER_STEP):
        a = acc_s[hh]
        outs.append(a / a[:, _aug_lane(hh):_aug_lane(hh) + 1])
    o_ref[0] = jnp.where(lane < HEAD_DIM, outs[0], outs[1]).astype(BF16)


def _attn_call(q, k, v, stats):
    batch, nh, seq, _ = q.shape
    tq = ATTN_QUERY_SUBTILES * ATTN_TILE
    kv_spec = pl.BlockSpec((1, HEADS_PER_STEP, seq, LANES), lambda b, h, i: (b, h, 0, 0))
    return pl.pallas_call(
        _attn_kernel,
        grid=(batch, nh // HEADS_PER_STEP, seq // tq),
        in_specs=[pl.BlockSpec((1, HEADS_PER_STEP, tq, LANES), lambda b, h, i: (b, h, i, 0)),
                  kv_spec, kv_spec,
                  pl.BlockSpec((1,) + stats.shape[1:], lambda b, h, i: (b, 0, 0, 0))],
        out_specs=pl.BlockSpec((1, tq, LANES), lambda b, h, i: (b, i, h)),
        out_shape=jax.ShapeDtypeStruct((batch, seq, nh * HEAD_DIM), BF16),
        scratch_shapes=[pltpu.VMEM((HEADS_PER_STEP, tq, LANES), F32)] * 2 + [pltpu.SMEM((stats.shape[1],), I32)],
        compiler_params=_params(("arbitrary", "arbitrary", "arbitrary")),
        name="fox_attention",
    )(q, k, v, stats)


def _wo_kernel(x_ref, o_ref, g1, sh2, sc2, gf, wo_ref, rwt_ref, rb_ref, xo_ref, hp_ref, lg_ref):
    m = jnp.dot(o_ref[0], wo_ref[...], preferred_element_type=F32)
    xn = x_ref[0] + g1[0] * m
    xo_ref[0] = xn
    hp, lg = _ffn_prenorm(xn, gf[...], sc2[0], sh2[0], rwt_ref[...], rb_ref[...])
    hp_ref[...] = hp
    lg_ref[...] = lg


def _wo_call(layer, x, o, modr, gf, wo, rwt, rb):
    batch, seq, d = x.shape
    e = rwt.shape[0]
    ts = SEQ_TILE
    ns = seq // ts
    row, vec, x_spec, out_specs = _row_layer_specs(layer, batch, ns, ts, d, e)
    return pl.pallas_call(
        _wo_kernel,
        grid=(batch, ns),
        in_specs=[
            x_spec, x_spec, row(2), row(3), row(4), vec,
            pl.BlockSpec((d, d), lambda b, s: (0, 0)),
            pl.BlockSpec((e, d), lambda b, s: (0, 0)),
            pl.BlockSpec((e, 1), lambda b, s: (0, 0)),
        ],
        out_specs=out_specs,
        out_shape=[
            jax.ShapeDtypeStruct((batch, seq, d), F32),
            jax.ShapeDtypeStruct((batch * seq, d // 2), U32),
            jax.ShapeDtypeStruct((e, batch * seq), F32),
        ],
        compiler_params=_params(("arbitrary", "arbitrary")),
        name="fox_out",
    )(x, o, modr, modr, modr, gf, wo, rwt, rb)


def _route_kernel(lg_ref, eid_ref, gate_ref, pos_ref, cnt_ref, carry):
    i = pl.program_id(0)

    @pl.when(i == 0)
    def _():
        carry[...] = jnp.zeros(carry.shape, F32)

    l = lg_ref[...]
    e, ts = l.shape
    eidx = lax.broadcasted_iota(I32, (e, ts), 0)
    work = l
    top_v, top_i, hot = [], [], []
    for _ in range(TOP_K):
        m = jnp.max(work, axis=0, keepdims=True)
        sel = jnp.min(jnp.where(work == m, eidx, e), axis=0, keepdims=True)
        o = eidx == sel
        top_v.append(m)
        top_i.append(sel)
        hot.append(o)
        work = jnp.where(o, -jnp.inf, work)
    ex = [jnp.exp(v - top_v[0]) for v in top_v]
    den = ex[0] + ex[1] + ex[2] + ex[3]
    chosen = jnp.where(hot[0] | hot[1] | hot[2] | hot[3], 1.0, 0.0)
    r = lax.broadcasted_iota(I32, (ts, ts), 0)
    c = lax.broadcasted_iota(I32, (ts, ts), 1)
    before = jnp.where(r < c, 1.0, 0.0).astype(BF16)
    rank = jnp.dot(chosen.astype(BF16), before, preferred_element_type=F32) + carry[...]
    pos = [jnp.sum(jnp.where(o, rank, 0.0), axis=0, keepdims=True) for o in hot]
    carry[...] = carry[...] + jnp.sum(chosen, axis=1, keepdims=True)
    eid_ref[...] = jnp.concatenate(top_i, axis=0)
    gate_ref[...] = jnp.concatenate([x / den for x in ex], axis=0)
    pos_ref[...] = jnp.concatenate(pos, axis=0).astype(I32)
    cnt_ref[...] = jnp.broadcast_to(carry[...], cnt_ref.shape)


def _route_call(logits_t):
    e, t = logits_t.shape
    ts = ROUTE_TILE
    out = pl.BlockSpec((TOP_K, ts), lambda i: (0, i))
    return pl.pallas_call(
        _route_kernel,
        grid=(t // ts,),
        in_specs=[pl.BlockSpec((e, ts), lambda i: (0, i))],
        out_specs=[out, out, out, pl.BlockSpec((e, LANES), lambda i: (0, 0))],
        out_shape=[
            jax.ShapeDtypeStruct((TOP_K, t), I32),
            jax.ShapeDtypeStruct((TOP_K, t), F32),
            jax.ShapeDtypeStruct((TOP_K, t), I32),
            jax.ShapeDtypeStruct((e, LANES), F32),
        ],
        scratch_shapes=[pltpu.VMEM((e, 1), F32)],
        compiler_params=_params(("arbitrary",)),
        name="route_topk",
    )(logits_t)


def _dest_kernel(pstart_ref, eid_ref, pos_ref, dest_ref):
    eid = eid_ref[...]
    dest = pos_ref[...]
    for e in range(pstart_ref.shape[0]):
        dest = dest + jnp.where(eid == e, pstart_ref[e], 0)
    dest_ref[...] = dest


def _dest_call(pstart, eid, pos):
    k, t = eid.shape
    tt = min(DEST_TILE, t)
    spec = pl.BlockSpec((k, tt), lambda i, ps: (0, i))
    grid_spec = pltpu.PrefetchScalarGridSpec(
        num_scalar_prefetch=1, grid=(t // tt,), in_specs=[spec, spec], out_specs=spec)
    return pl.pallas_call(
        _dest_kernel,
        grid_spec=grid_spec,
        out_shape=jax.ShapeDtypeStruct((k, t), I32),
        compiler_params=_params(("arbitrary",)),
        name="route_dest",
    )(pstart, eid, pos)


def _sc_worker_chunks(total_chunks):
    worker = lax.axis_index("s") * SC_CORES + lax.axis_index("c")
    per_worker = total_chunks // SC_WORKERS
    return worker * per_worker, per_worker


def _sc_dispatch_call(hp, dest3, rows):
    t, w = hp.shape
    nchunk = dest3.shape[0]
    mesh = plsc.VectorSubcoreMesh(core_axis_name="c", subcore_axis_name="s")

    assert nchunk % (SC_BUFFERS * SC_WORKERS) == 0

    @functools.partial(
        pl.kernel, mesh=mesh, out_type=jax.ShapeDtypeStruct((rows, w), U32),
        scratch_types=[pltpu.VMEM((SC_BUFFERS, TOP_K, SC_CHUNK), I32),
                       pltpu.VMEM((SC_BUFFERS, SC_CHUNK, w), U32),
                       pltpu.SemaphoreType.DMA((SC_BUFFERS,)), pltpu.SemaphoreType.DMA((SC_BUFFERS,))],
        name="moe_dispatch")
    def dispatch(hp_hbm, dest_hbm, xs_hbm, idx_v, rows_v, load_sem, scatter_sem):
        first, count = _sc_worker_chunks(nchunk)

        def load(c, slot):
            return pltpu.make_async_copy(hp_hbm.at[pl.ds(c * SC_CHUNK, SC_CHUNK)], rows_v.at[slot],
                                         load_sem.at[slot])

        def scatter(slot, k):
            return pltpu.make_async_copy(rows_v.at[slot], xs_hbm.at[idx_v.at[slot, k]],
                                         scatter_sem.at[slot])

        def start_load(c, slot):
            pltpu.sync_copy(dest_hbm.at[c], idx_v.at[slot])
            load(c, slot).start()

        start_load(first, 0)

        @pl.loop(0, count, step=SC_BUFFERS)
        def _(j):
            for b in range(SC_BUFFERS):
                c = first + j + b
                nxt = (b + 1) % SC_BUFFERS

                @pl.when(j + b + 1 < count)
                def _():
                    @pl.when(j + b >= 1)
                    def _():
                        for k in range(TOP_K):
                            scatter(nxt, k).wait()
                    start_load(c + 1, nxt)

                load(c, b).wait()
                for k in range(TOP_K):
                    scatter(b, k).start()

        for slot in range(SC_BUFFERS):
            for k in range(TOP_K):
                scatter(slot, k).wait()

    return dispatch(hp, dest3)


def _sc_gather_call(ys, idx):
    n = idx.shape[0]
    _, w = ys.shape
    nchunk = n // SC_CHUNK
    mesh = plsc.VectorSubcoreMesh(core_axis_name="c", subcore_axis_name="s")

    assert nchunk % (SC_BUFFERS * SC_WORKERS) == 0

    @functools.partial(
        pl.kernel, mesh=mesh, out_type=jax.ShapeDtypeStruct((n, w), U32),
        scratch_types=[pltpu.VMEM((SC_BUFFERS, SC_CHUNK), I32), pltpu.VMEM((SC_BUFFERS, SC_CHUNK, w), U32),
                       pltpu.SemaphoreType.DMA((SC_BUFFERS,)), pltpu.SemaphoreType.DMA((SC_BUFFERS,))],
        name="moe_gather")
    def gather(ys_hbm, idx_hbm, out_hbm, idx_v, rows_v, gather_sem, write_sem):
        first, count = _sc_worker_chunks(nchunk)

        def fetch(slot):
            return pltpu.make_async_copy(ys_hbm.at[idx_v.at[slot]], rows_v.at[slot], gather_sem.at[slot])

        def write(c, slot):
            return pltpu.make_async_copy(rows_v.at[slot], out_hbm.at[pl.ds(c * SC_CHUNK, SC_CHUNK)],
                                         write_sem.at[slot])

        def start_fetch(c, slot):
            pltpu.sync_copy(idx_hbm.at[pl.ds(c * SC_CHUNK, SC_CHUNK)], idx_v.at[slot])
            fetch(slot).start()

        start_fetch(first, 0)

        @pl.loop(0, count, step=SC_BUFFERS)
        def _(j):
            for b in range(SC_BUFFERS):
                c = first + j + b
                nxt = (b + 1) % SC_BUFFERS

                @pl.when(j + b + 1 < count)
                def _():
                    @pl.when(j + b >= 1)
                    def _():
                        write(c - 1, nxt).wait()
                    start_fetch(c + 1, nxt)

                fetch(b).wait()
                write(c, b).start()

        for b in range(SC_BUFFERS):
            write(first + count - SC_BUFFERS + b, b).wait()

    return gather(ys, idx)


def _expert_kernel(be_ref, nv_ref, xs_ref, w1_ref, b1_ref, w2_ref, b2_ref, ys_ref, w1s, w2s):
    b = pl.program_id(0)
    e = be_ref[b]
    prev = be_ref[jnp.maximum(b - 1, 0)]
    nvalid = nv_ref[b]
    f = w2_ref.shape[1]

    @pl.when((b == 0) | (e != prev))
    def _():
        w1s[...] = w1_ref[0].astype(BF16)
        w2s[...] = w2_ref[0].astype(BF16)

    def ffn(r0, n):
        rows = r0 + lax.broadcasted_iota(I32, (n, 1), 0)
        xw = jnp.where(rows < nvalid, xs_ref[r0:r0 + n, :], jnp.uint32(0))
        x = _unpack_pairs(xw).astype(BF16)
        gu = jnp.dot(x, w1s[...], preferred_element_type=F32) + b1_ref[0]
        gate = jnp.minimum(gu[:, :f], SWIGLU_LIMIT)
        up = jnp.clip(gu[:, f:], -SWIGLU_LIMIT, SWIGLU_LIMIT)
        act = (up + 1.0) * (gate * _sigmoid(SWIGLU_ALPHA * gate))
        y = jnp.dot(act.astype(BF16), w2s[...], preferred_element_type=F32) + b2_ref[0]
        ys_ref[r0:r0 + n, :] = _pack_pairs(y)

    def blank(r0, n):
        ys_ref[r0:r0 + n, :] = jnp.zeros((n, ys_ref.shape[1]), U32)

    sub = EXPERT_SUBROWS
    half = sub // 2
    for r0 in range(0, xs_ref.shape[0], sub):
        @pl.when(nvalid > r0 + half)
        def _():
            ffn(r0, sub)

        @pl.when((nvalid > r0) & (nvalid <= r0 + half))
        def _():
            ffn(r0, half)
            blank(r0 + half, half)

        @pl.when(nvalid <= r0)
        def _():
            blank(r0, sub)


def _expert_call(layer, block_e, nvalid, xs, w_in, b_in, w_out, b_out):
    rows, w = xs.shape
    depth, ne, d, f2 = w_in.shape
    f = f2 // 2
    bm = EXPERT_ROWS
    x_spec = pl.BlockSpec((bm, w), lambda b, be, nv: (b, 0))
    grid_spec = pltpu.PrefetchScalarGridSpec(
        num_scalar_prefetch=2,
        grid=(rows // bm,),
        in_specs=[
            x_spec,
            pl.BlockSpec((1, d, f2), lambda b, be, nv: (layer * ne + be[b], 0, 0)),
            pl.BlockSpec((1, 1, f2), lambda b, be, nv: (layer * ne + be[b], 0, 0)),
            pl.BlockSpec((1, f, d), lambda b, be, nv: (layer * ne + be[b], 0, 0)),
            pl.BlockSpec((1, 1, d), lambda b, be, nv: (layer * ne + be[b], 0, 0)),
        ],
        out_specs=x_spec,
        scratch_shapes=[pltpu.VMEM((d, f2), BF16), pltpu.VMEM((f, d), BF16)],
    )
    return pl.pallas_call(
        _expert_kernel,
        grid_spec=grid_spec,
        out_shape=jax.ShapeDtypeStruct((rows, w), U32),
        compiler_params=_params(("arbitrary",)),
        name="moe_experts",
    )(block_e, nvalid, xs,
      w_in.reshape(depth * ne, d, f2), b_in.reshape(depth * ne, 1, f2),
      w_out.reshape(depth * ne, f, d), b_out.reshape(depth * ne, 1, d))


def _final_kernel(x_ref, yg_ref, gt_ref, g2, fg, xo_ref):
    xn = _moe_residual(x_ref, yg_ref, gt_ref, g2)
    r = lax.rsqrt(jnp.mean(xn * xn, axis=-1, keepdims=True) + EPS)
    xo_ref[0] = (xn * r) * fg[...]


def _final_call(x, pending, modr, fg):
    batch, seq, d = x.shape
    ts = COMBINE_TILE
    ns = seq // ts
    return pl.pallas_call(
        _final_kernel,
        grid=(batch, ns),
        in_specs=_residual_specs(pending, batch, ns, ts, d) + [pl.BlockSpec((1, d), lambda b, s: (0, 0))],
        out_specs=pl.BlockSpec((1, ts, d), lambda b, s: (b, s, 0)),
        out_shape=jax.ShapeDtypeStruct((batch, seq, d), F32),
        compiler_params=_params(("arbitrary", "arbitrary")),
        name="moe_combine_final",
    )(*_residual_args(x, pending, modr), fg)


def _moe(layer, hp, logits_t, w_in, b_in, w_out, b_out):
    t, w = hp.shape
    ne = logits_t.shape[0]
    bm = EXPERT_ROWS
    eid, gate, pos, cnt = _route_call(logits_t)
    counts = cnt[:, 0].astype(I32)
    nblk = (counts + bm - 1) // bm
    bend = jnp.cumsum(nblk)
    bstart = bend - nblk
    pstart = (bstart * bm).astype(I32)
    n_blocks = (t * TOP_K) // bm + ne
    blocks = jnp.arange(n_blocks, dtype=I32)
    block_e = jnp.minimum(jnp.sum(blocks[:, None] >= bend[None, :], axis=1), ne - 1).astype(I32)
    mine = block_e[:, None] == jnp.arange(ne, dtype=I32)[None, :]
    left = jnp.sum(jnp.where(mine, counts[None, :] - (blocks[:, None] - bstart[None, :]) * bm, 0), axis=1)
    nvalid = jnp.where(blocks < bend[-1], jnp.clip(left, 0, bm), 0).astype(I32)
    dest = _dest_call(pstart, eid, pos)
    dest3 = dest.reshape(TOP_K, t // SC_CHUNK, SC_CHUNK).transpose(1, 0, 2)
    xs = _sc_dispatch_call(hp, dest3, n_blocks * bm)
    ys = _expert_call(layer, block_e, nvalid, xs, w_in, b_in, w_out, b_out)
    yg = _sc_gather_call(ys, dest.reshape(TOP_K * t))
    return {"yg": yg.reshape(TOP_K, t, w), "gate_t": gate.T, "layer": layer}


def kernel(x, c, norm_mix_g, norm_ffn_g, ada_w, ada_b, pool_w, pool_scale, fox_w_in, fox_b_f, fox_w_o,
           router_w, router_b, exp_w_in, exp_b_in, exp_w_out, exp_b_out, final_g):
    batch, seq, d = x.shape
    depth = ada_w.shape[0]
    mod = _ada_call(c, ada_w, ada_b)
    modr = mod.reshape(depth * batch * 6, 1, d)
    fg = final_g.reshape(1, d)
    pending = None
    for i in range(depth):
        gm = norm_mix_g[i].reshape(1, d)
        gf = norm_ffn_g[i].reshape(1, d)
        rwt = router_w[i].T
        rb = router_b[i].reshape(-1, 1)
        j = i // 2
        if i % 2 == 0:
            x, hp, lg = _pool_call(i, x, pending, modr, gm, gf, pool_w[j], pool_scale[j].reshape(1, d),
                                   rwt, rb)
        else:
            q, k, v, stats, x = _qkv_call(i, x, pending, modr, gm, fox_w_in[j], fox_b_f[j].reshape(1, -1))
            o = _attn_call(q, k, v, stats)
            x, hp, lg = _wo_call(i, x, o, modr, gf, fox_w_o[j].astype(BF16), rwt, rb)
        pending = _moe(i, hp, lg, exp_w_in, exp_b_in, exp_w_out, exp_b_out)
    return _final_call(x, pending, modr, fg)
```

```python
import functools

import jax
import jax.numpy as jnp
import numpy as np
from jax import lax
from jax.experimental import pallas as pl
from jax.experimental.pallas import tpu as pltpu
from jax.experimental.pallas import tpu_sc as plsc

F32 = jnp.float32
BF16 = jnp.bfloat16
U32 = jnp.uint32
I32 = jnp.int32

POOL_WINDOWS = (2, 4, 8, 16)
POOL_HALO = 16
HEAD_DIM = 64
HEADS_PER_STEP = 2
TOP_K = 4
GATE_COLS = 8
SWIGLU_LIMIT = 7.0
SWIGLU_ALPHA = 1.702
EPS = 1e-6
NEG_BIG = -1e30
LOG2E = 1.4426950408889634
BIAS_PARTS = 3
UNDERFLOW_LOG2 = 160.0
NORM_SLACK = 1.02

LANES = 128
STAT_ROWS = 8
SEQ_TILE = 512
QKV_SUBROWS = 256
ATTN_TILE = 512
ATTN_QUERY_SUBTILES = 4
ROUTE_TILE = 512
DEST_TILE = 4096
EXPERT_ROWS = 1024
EXPERT_SUBROWS = 512
COMBINE_TILE = 512
ADA_COLS = 1536
VMEM_LIMIT_BYTES = 56 * 1024 * 1024

SC_CORES = 2
SC_SUBCORES = 16
SC_WORKERS = SC_CORES * SC_SUBCORES
SC_CHUNK = 64
SC_BUFFERS = 2

NT_DIMS = (((1,), (1,)), ((), ()))


def _params(sem, vmem=VMEM_LIMIT_BYTES):
    return pltpu.CompilerParams(dimension_semantics=sem, vmem_limit_bytes=vmem)


def _sigmoid(z):
    return 1.0 / (1.0 + jnp.exp(-z))


def _norm_mod(x, g, scale, shift):
    r = lax.rsqrt(jnp.mean(x * x, axis=-1, keepdims=True) + EPS)
    return (x * r) * (g * (1.0 + scale)) + shift


def _pack_pairs(y):
    w = y.shape[1] // 2
    hi = lax.bitcast_convert_type(y[:, :w].astype(BF16).astype(F32), U32)
    lo = lax.bitcast_convert_type(y[:, w:].astype(BF16).astype(F32), U32)
    return hi | (lo >> 16)


def _unpack_pairs(w):
    hi = lax.bitcast_convert_type(w & jnp.uint32(0xFFFF0000), F32)
    lo = lax.bitcast_convert_type(w << 16, F32)
    return jnp.concatenate([hi, lo], axis=1)


def _split_bf16(x, parts):
    out = []
    r = x
    for _ in range(parts):
        p = r.astype(BF16)
        out.append(p)
        r = r - p.astype(F32)
    return out


def _ffn_prenorm(xn, gf, sc2, sh2, rwt, rb):
    h2 = _norm_mod(xn, gf, sc2, sh2)
    h_hi, h_lo = _split_bf16(h2, 2)
    w_hi, w_lo = _split_bf16(rwt, 2)
    lg = (lax.dot_general(w_hi, h_hi, NT_DIMS, preferred_element_type=F32)
          + lax.dot_general(w_lo, h_hi, NT_DIMS, preferred_element_type=F32)
          + lax.dot_general(w_hi, h_lo, NT_DIMS, preferred_element_type=F32)) + rb
    return _pack_pairs(h2), lg


def _mod_spec(layer, batch, j):
    def index(b, s):
        return ((layer * batch + b) * 6 + j, 0, 0)
    return index


def _moe_residual(x_ref, yg_ref, gt_ref, g2_ref, rows=slice(None)):
    gt = gt_ref[rows, :]
    moe = gt[:, 0:1] * _unpack_pairs(yg_ref[0, rows, :])
    for k in range(1, TOP_K):
        moe = moe + gt[:, k:k + 1] * _unpack_pairs(yg_ref[k, rows, :])
    return x_ref[0, rows, :] + g2_ref[0] * moe


def _residual_in(refs, pending, rows=slice(None)):
    if pending:
        return _moe_residual(*refs[:4], rows=rows)
    return refs[0][0, rows, :]


def _residual_rest(refs, pending):
    return refs[4:] if pending else refs[1:]


def _residual_specs(pending, batch, ns, ts, d):
    specs = [pl.BlockSpec((1, ts, d), lambda b, s: (b, s, 0))]
    if pending:
        _, t, w = pending["yg"].shape
        specs += [
            pl.BlockSpec((TOP_K, ts, w), lambda b, s: (0, b * ns + s, 0)),
            pl.BlockSpec((ts, GATE_COLS), lambda b, s: (b * ns + s, 0)),
            pl.BlockSpec((1, 1, d), _mod_spec(pending["layer"], batch, 5)),
        ]
    return specs


def _residual_args(x, pending, modr):
    return (x, pending["yg"], pending["gate_t"], modr) if pending else (x,)


def _ada_kernel(ct_ref, w_ref, b_ref, o_ref):
    ct = ct_ref[...]
    cond = ct * _sigmoid(ct)
    w = w_ref[0]
    rows = []
    for b in range(ct.shape[1]):
        rows.append(jnp.sum(w * cond[:, b:b + 1], axis=0, keepdims=True))
    o_ref[0] = jnp.concatenate(rows, axis=0) + b_ref[0]


def _ada_call(c, ada_w, ada_b):
    depth, d, n = ada_w.shape
    batch = c.shape[0]
    tn = ADA_COLS
    return pl.pallas_call(
        _ada_kernel,
        grid=(depth, n // tn),
        in_specs=[
            pl.BlockSpec((d, batch), lambda i, j: (0, 0)),
            pl.BlockSpec((1, d, tn), lambda i, j: (i, 0, j)),
            pl.BlockSpec((1, 1, tn), lambda i, j: (i, 0, j)),
        ],
        out_specs=pl.BlockSpec((1, batch, tn), lambda i, j: (i, 0, j)),
        out_shape=jax.ShapeDtypeStruct((depth, batch, n), F32),
        compiler_params=_params(("arbitrary", "arbitrary")),
        name="ada_mod",
    )(c.T, ada_w, ada_b.reshape(depth, 1, n))


def _pool_kernel(pending, *refs):
    x = _residual_in(refs, pending)
    refs = _residual_rest(refs, pending)
    sh1, sc1, g1, sh2, sc2, gm, gf, pw_ref, ps_ref, rwt_ref, rb_ref, xo_ref, hp_ref, lg_ref, hbuf = refs
    s = pl.program_id(1)
    ts, d = x.shape
    pg = d // len(POOL_WINDOWS)
    h = _norm_mod(x, gm[...], sc1[0], sh1[0])

    @pl.when(s == 0)
    def _():
        hbuf[0:POOL_HALO, :] = jnp.zeros((POOL_HALO, d), F32)

    @pl.when(s > 0)
    def _():
        hbuf[0:POOL_HALO, :] = hbuf[ts:ts + POOL_HALO, :]

    hbuf[POOL_HALO:POOL_HALO + ts, :] = h
    pos = s * ts + lax.broadcasted_iota(I32, (ts, 1), 0)
    ys = []
    for g, w in enumerate(POOL_WINDOWS):
        lo = g * pg
        hg = h[:, lo:lo + pg]
        win = hbuf[:, lo:lo + pg]
        span = 1
        while span < w:
            win = win + pltpu.roll(win, span, 0)
            span *= 2
        acc = win[POOL_HALO:, :]
        cnt = jnp.minimum(pos + 1, w).astype(F32)
        dlt = acc / cnt - hg
        ys.append(jnp.dot(dlt.astype(BF16), pw_ref[g].astype(BF16), preferred_element_type=F32))
    y = jnp.concatenate(ys, axis=1) * ps_ref[...]
    xn = x + g1[0] * y
    xo_ref[0] = xn
    hp, lg = _ffn_prenorm(xn, gf[...], sc2[0], sh2[0], rwt_ref[...], rb_ref[...])
    hp_ref[...] = hp
    lg_ref[...] = lg


def _row_layer_specs(layer, batch, ns, ts, d, e):
    row = lambda j: pl.BlockSpec((1, 1, d), _mod_spec(layer, batch, j))
    vec = pl.BlockSpec((1, d), lambda b, s: (0, 0))
    x_spec = pl.BlockSpec((1, ts, d), lambda b, s: (b, s, 0))
    out_specs = [
        x_spec,
        pl.BlockSpec((ts, d // 2), lambda b, s: (b * ns + s, 0)),
        pl.BlockSpec((e, ts), lambda b, s: (0, b * ns + s)),
    ]
    return row, vec, x_spec, out_specs


def _pool_call(layer, x, pending, modr, gm, gf, pw, ps, rwt, rb):
    batch, seq, d = x.shape
    e = rwt.shape[0]
    ts = SEQ_TILE
    ns = seq // ts
    g = len(POOL_WINDOWS)
    pg = d // g
    row, vec, _, out_specs = _row_layer_specs(layer, batch, ns, ts, d, e)
    return pl.pallas_call(
        functools.partial(_pool_kernel, bool(pending)),
        grid=(batch, ns),
        in_specs=_residual_specs(pending, batch, ns, ts, d) + [
            row(0), row(1), row(2), row(3), row(4), vec, vec,
            pl.BlockSpec((g, pg, pg), lambda b, s: (0, 0, 0)),
            vec,
            pl.BlockSpec((e, d), lambda b, s: (0, 0)),
            pl.BlockSpec((e, 1), lambda b, s: (0, 0)),
        ],
        out_specs=out_specs,
        out_shape=[
            jax.ShapeDtypeStruct((batch, seq, d), F32),
            jax.ShapeDtypeStruct((batch * seq, d // 2), U32),
            jax.ShapeDtypeStruct((e, batch * seq), F32),
        ],
        scratch_shapes=[pltpu.VMEM((ts + POOL_HALO, d), F32)],
        compiler_params=_params(("arbitrary", "arbitrary")),
        name="pool_layer",
    )(*_residual_args(x, pending, modr), modr, modr, modr, modr, modr, gm, gf, pw, ps, rwt, rb)


def _aug_lane(head):
    return HEAD_DIM if head % 2 == 0 else 0


def _bias_placement(nh):
    place = np.zeros((BIAS_PARTS * nh, nh * LANES), np.float32)
    for j in range(BIAS_PARTS):
        for h in range(nh):
            place[j * nh + h, h * LANES + _aug_lane(h) + j] = 1.0
    return jnp.asarray(place, BF16)


def _qkv_kernel(pending, *refs):
    rest = _residual_rest(refs, pending)
    sh1, sc1, gm, w_ref, wf_ref, bf_ref, place_ref, seg_ref, q_ref, k_ref, v_ref, st_ref = rest[:12]
    carry = refs[-1]
    s = pl.program_id(1)
    ts = q_ref.shape[2]
    d = gm.shape[1]
    nh = bf_ref.shape[1]
    sub = QKV_SUBROWS

    @pl.when(s == 0)
    def _():
        carry[...] = jnp.zeros(carry.shape, F32)

    r = lax.broadcasted_iota(I32, (sub, sub), 0)
    c = lax.broadcasted_iota(I32, (sub, sub), 1)
    lower = jnp.where(r >= c, 1.0, 0.0).astype(BF16)
    lane = lax.broadcasted_iota(I32, (sub, LANES), 1)
    q_top = k_top = jnp.zeros((1, LANES), F32)
    b_top = jnp.full((1, nh), NEG_BIG, F32)
    for r0 in range(0, ts, sub):
        rows = slice(r0, r0 + sub)
        x = _residual_in(refs, pending, rows)
        if pending:
            rest[12][0, rows, :] = x
        h = _norm_mod(x, gm[...], sc1[0], sh1[0]).astype(BF16)
        fl = jnp.dot(h, wf_ref[...], preferred_element_type=F32)[:, :nh] + bf_ref[...]
        proj = jnp.dot(h, w_ref[...], preferred_element_type=F32)
        logf = jnp.minimum(fl, 0.0) - jnp.log(1.0 + jnp.exp(-jnp.abs(fl)))
        parts = jnp.concatenate(_split_bf16(logf, 3), axis=1)
        cs = jnp.dot(lower, parts, preferred_element_type=F32)
        cum = cs[:, :nh] + cs[:, nh:2 * nh] + cs[:, 2 * nh:] + carry[...]
        carry[...] = cum[sub - 1:sub, :]
        bias = jnp.concatenate(_split_bf16(cum * (-LOG2E), BIAS_PARTS), axis=1)
        k_aug = jnp.dot(bias, place_ref[...], preferred_element_type=F32)
        q_all = proj[:, :d] * (HEAD_DIM ** -0.5 * LOG2E)
        k_all = proj[:, d:2 * d]
        q_norm = jnp.sqrt(jnp.dot((q_all * q_all).astype(BF16), seg_ref[...], preferred_element_type=F32))
        k_norm = jnp.sqrt(jnp.dot((k_all * k_all).astype(BF16), seg_ref[...], preferred_element_type=F32))
        q_top = jnp.maximum(q_top, jnp.max(q_norm, axis=0, keepdims=True))
        k_top = jnp.maximum(k_top, jnp.max(k_norm, axis=0, keepdims=True))
        b_top = jnp.maximum(b_top, jnp.max(cum * (-LOG2E), axis=0, keepdims=True))
        for hd in range(nh):
            lo = (hd // 2) * LANES
            a0 = _aug_lane(hd)
            real = (lane < HEAD_DIM) if hd % 2 == 0 else (lane >= HEAD_DIM)
            q_aug = jnp.where((lane >= a0) & (lane < a0 + BIAS_PARTS), 1.0, 0.0)
            v_aug = jnp.where(lane == a0, 1.0, 0.0)
            q_ref[0, hd, rows, :] = jnp.where(real, q_all[:, lo:lo + LANES], q_aug).astype(BF16)
            k_ref[0, hd, rows, :] = jnp.where(real, proj[:, d + lo:d + lo + LANES],
                                              k_aug[:, hd * LANES:(hd + 1) * LANES]).astype(BF16)
            v_ref[0, hd, rows, :] = jnp.where(real, proj[:, 2 * d + lo:2 * d + lo + LANES],
                                              v_aug).astype(BF16)
    b_row = jnp.concatenate([b_top, jnp.zeros((1, LANES - nh), F32)], axis=1)
    st_ref[0, 0] = jnp.concatenate([q_top, k_top, b_row, jnp.zeros((STAT_ROWS - 3, LANES), F32)], axis=0)


def _qkv_call(layer, x, pending, modr, gm, w_in, bf):
    batch, seq, d = x.shape
    nh = bf.shape[1]
    ts = SEQ_TILE
    ns = seq // ts
    w_qkv = w_in[:, :3 * d].astype(BF16)
    w_f = jnp.pad(w_in[:, 3 * d:], ((0, 0), (0, LANES - nh))).astype(BF16)
    place = _bias_placement(nh)
    seg = jnp.asarray(np.arange(d)[:, None] // HEAD_DIM == np.arange(LANES)[None, :], BF16)
    row = lambda j: pl.BlockSpec((1, 1, d), _mod_spec(layer, batch, j))
    head_spec = pl.BlockSpec((1, nh, ts, LANES), lambda b, s: (b, 0, s, 0))
    out_specs = [head_spec] * 3 + [pl.BlockSpec((1, 1, STAT_ROWS, LANES), lambda b, s: (b, s, 0, 0))]
    out_shape = [jax.ShapeDtypeStruct((batch, nh, seq, LANES), BF16)] * 3 + [
        jax.ShapeDtypeStruct((batch, ns, STAT_ROWS, LANES), F32)]
    if pending:
        out_specs = out_specs + [pl.BlockSpec((1, ts, d), lambda b, s: (b, s, 0))]
        out_shape = out_shape + [jax.ShapeDtypeStruct((batch, seq, d), F32)]
    outs = pl.pallas_call(
        functools.partial(_qkv_kernel, bool(pending)),
        grid=(batch, ns),
        in_specs=_residual_specs(pending, batch, ns, ts, d) + [
            row(0), row(1),
            pl.BlockSpec((1, d), lambda b, s: (0, 0)),
            pl.BlockSpec(w_qkv.shape, lambda b, s: (0, 0)),
            pl.BlockSpec(w_f.shape, lambda b, s: (0, 0)),
            pl.BlockSpec((1, nh), lambda b, s: (0, 0)),
            pl.BlockSpec(place.shape, lambda b, s: (0, 0)),
            pl.BlockSpec(seg.shape, lambda b, s: (0, 0)),
        ],
        out_specs=out_specs,
        out_shape=out_shape,
        scratch_shapes=[pltpu.VMEM((1, nh), F32)],
        compiler_params=_params(("arbitrary", "arbitrary")),
        name="fox_qkv",
    )(*_residual_args(x, pending, modr), modr, modr, gm, w_qkv, w_f, bf, place, seg)
    return outs[0], outs[1], outs[2], outs[3], (outs[4] if pending else x)


def _attn_kernel(q_ref, k_ref, v_ref, st_ref, o_ref, m_s, acc_s, level_s):
    hp = pl.program_id(1)
    qi = pl.program_id(2)
    tk = ATTN_TILE
    tq = q_ref.shape[2]
    subs = tq // tk
    m_s[...] = jnp.full(m_s.shape, NEG_BIG, F32)
    acc_s[...] = jnp.zeros(acc_s.shape, F32)
    row = lax.broadcasted_iota(I32, (tk, tk), 0)
    col = lax.broadcasted_iota(I32, (tk, tk), 1)

    def step(kb, lo, hi, diagonal):
        start = pl.multiple_of(kb * tk, tk)
        rows = slice(lo * tk, hi * tk)
        for hh in range(HEADS_PER_STEP):
            kk = k_ref[0, hh, pl.ds(start, tk), :]
            vv = v_ref[0, hh, pl.ds(start, tk), :]
            z = lax.dot_general(q_ref[0, hh, rows, :], kk, NT_DIMS, preferred_element_type=F32)
            if diagonal:
                masked = jnp.where(row >= col, z[:tk], NEG_BIG)
                z = masked if hi - lo == 1 else jnp.concatenate([masked, z[tk:]], axis=0)
            m_prev = m_s[hh, rows, :]
            m_new = jnp.maximum(m_prev, jnp.max(z, axis=1, keepdims=True))
            p = jnp.exp2(z - jnp.concatenate([m_new] * (tk // LANES), axis=1))
            acc_s[hh, rows, :] = (jnp.exp2(m_prev - m_new) * acc_s[hh, rows, :]
                                  + jnp.dot(p.astype(BF16), vv, preferred_element_type=F32))
            m_s[hh, rows, :] = m_new

    for j in range(subs):
        step(subs * qi + j, j, subs, True)

    st = st_ref[0]
    nt = st.shape[0]
    stat = lax.broadcasted_iota(I32, st.shape, 1)
    q_top = jnp.max(jnp.where(stat == 0, st, 0.0), axis=1)
    k_top = jnp.max(jnp.where(stat == 1, st, 0.0), axis=1)
    b_top = jnp.max(jnp.where(stat == 2, st, NEG_BIG), axis=1)
    tile = lax.broadcasted_iota(I32, (nt, LANES), 0)
    head = lax.broadcasted_iota(I32, (nt, LANES), 1)
    level = jnp.zeros((nt, LANES), F32)
    for hh in range(HEADS_PER_STEP):
        earlier = (head == HEADS_PER_STEP * hp + hh) & (tile < subs * qi)
        for j in range(subs):
            q_here = jnp.max(jnp.where(tile == subs * qi + j, q_top, 0.0), axis=0, keepdims=True)
            floor = jnp.min(m_s[hh, j * tk:(j + 1) * tk, :]) - UNDERFLOW_LOG2
            needed = (NORM_SLACK * q_here * k_top + b_top + 1.0 >= floor) & earlier
            level = jnp.maximum(level, jnp.where(needed, float(j + 1), 0.0))
    level = jnp.max(level, axis=1, keepdims=True).astype(I32)
    for t in range(nt):
        level_s[t] = level[t, 0]

    spans = sorted({1, max(subs // 2, 1), subs})

    def body(kb, carry):
        need = level_s[kb]
        below = 0
        for hi in spans:
            @pl.when((need > below) & (need <= hi))
            def _():
                step(kb, 0, hi, False)
            below = hi
        return carry

    lax.fori_loop(0, subs * qi, body, 0)

    lane = lax.broadcasted_iota(I32, (tq, LANES), 1)
    outs = []
    for hh in range(HEADS_PER_STEP):
        a = acc_s[hh]
        outs.append(a / a[:, _aug_lane(hh):_aug_lane(hh) + 1])
    o_ref[0] = jnp.where(lane < HEAD_DIM, outs[0], outs[1]).astype(BF16)


def _attn_call(q, k, v, stats):
    batch, nh, seq, _ = q.shape
    tq = ATTN_QUERY_SUBTILES * ATTN_TILE
    kv_spec = pl.BlockSpec((1, HEADS_PER_STEP, seq, LANES), lambda b, h, i: (b, h, 0, 0))
    return pl.pallas_call(
        _attn_kernel,
        grid=(batch, nh // HEADS_PER_STEP, seq // tq),
        in_specs=[pl.BlockSpec((1, HEADS_PER_STEP, tq, LANES), lambda b, h, i: (b, h, i, 0)),
                  kv_spec, kv_spec,
                  pl.BlockSpec((1,) + stats.shape[1:], lambda b, h, i: (b, 0, 0, 0))],
        out_specs=pl.BlockSpec((1, tq, LANES), lambda b, h, i: (b, i, h)),
        out_shape=jax.ShapeDtypeStruct((batch, seq, nh * HEAD_DIM), BF16),
        scratch_shapes=[pltpu.VMEM((HEADS_PER_STEP, tq, LANES), F32)] * 2 + [pltpu.SMEM((stats.shape[1],), I32)],
        compiler_params=_params(("arbitrary", "arbitrary", "arbitrary")),
        name="fox_attention",
    )(q, k, v, stats)


def _wo_kernel(x_ref, o_ref, g1, sh2, sc2, gf, wo_ref, rwt_ref, rb_ref, xo_ref, hp_ref, lg_ref):
    m = jnp.dot(o_ref[0], wo_ref[...], preferred_element_type=F32)
    xn = x_ref[0] + g1[0] * m
    xo_ref[0] = xn
    hp, lg = _ffn_prenorm(xn, gf[...], sc2[0], sh2[0], rwt_ref[...], rb_ref[...])
    hp_ref[...] = hp
    lg_ref[...] = lg


def _wo_call(layer, x, o, modr, gf, wo, rwt, rb):
    batch, seq, d = x.shape
    e = rwt.shape[0]
    ts = SEQ_TILE
    ns = seq // ts
    row, vec, x_spec, out_specs = _row_layer_specs(layer, batch, ns, ts, d, e)
    return pl.pallas_call(
        _wo_kernel,
        grid=(batch, ns),
        in_specs=[
            x_spec, x_spec, row(2), row(3), row(4), vec,
            pl.BlockSpec((d, d), lambda b, s: (0, 0)),
            pl.BlockSpec((e, d), lambda b, s: (0, 0)),
            pl.BlockSpec((e, 1), lambda b, s: (0, 0)),
        ],
        out_specs=out_specs,
        out_shape=[
            jax.ShapeDtypeStruct((batch, seq, d), F32),
            jax.ShapeDtypeStruct((batch * seq, d // 2), U32),
            jax.ShapeDtypeStruct((e, batch * seq), F32),
        ],
        compiler_params=_params(("arbitrary", "arbitrary")),
        name="fox_out",
    )(x, o, modr, modr, modr, gf, wo, rwt, rb)


def _route_kernel(lg_ref, eid_ref, gate_ref, pos_ref, cnt_ref, carry):
    i = pl.program_id(0)

    @pl.when(i == 0)
    def _():
        carry[...] = jnp.zeros(carry.shape, F32)

    l = lg_ref[...]
    e, ts = l.shape
    eidx = lax.broadcasted_iota(I32, (e, ts), 0)
    work = l
    top_v, top_i, hot = [], [], []
    for _ in range(TOP_K):
        m = jnp.max(work, axis=0, keepdims=True)
        sel = jnp.min(jnp.where(work == m, eidx, e), axis=0, keepdims=True)
        o = eidx == sel
        top_v.append(m)
        top_i.append(sel)
        hot.append(o)
        work = jnp.where(o, -jnp.inf, work)
    ex = [jnp.exp(v - top_v[0]) for v in top_v]
    den = ex[0] + ex[1] + ex[2] + ex[3]
    chosen = jnp.where(hot[0] | hot[1] | hot[2] | hot[3], 1.0, 0.0)
    r = lax.broadcasted_iota(I32, (ts, ts), 0)
    c = lax.broadcasted_iota(I32, (ts, ts), 1)
    before = jnp.where(r < c, 1.0, 0.0).astype(BF16)
    rank = jnp.dot(chosen.astype(BF16), before, preferred_element_type=F32) + carry[...]
    pos = [jnp.sum(jnp.where(o, rank, 0.0), axis=0, keepdims=True) for o in hot]
    carry[...] = carry[...] + jnp.sum(chosen, axis=1, keepdims=True)
    eid_ref[...] = jnp.concatenate(top_i, axis=0)
    gates = [x / den for x in ex] + [jnp.zeros((GATE_COLS - TOP_K, ts), F32)]
    gate_ref[...] = jnp.concatenate(gates, axis=0).T
    pos_ref[...] = jnp.concatenate(pos, axis=0).astype(I32)
    cnt_ref[...] = jnp.broadcast_to(carry[...], cnt_ref.shape)


def _route_call(logits_t):
    e, t = logits_t.shape
    ts = ROUTE_TILE
    out = pl.BlockSpec((TOP_K, ts), lambda i: (0, i))
    return pl.pallas_call(
        _route_kernel,
        grid=(t // ts,),
        in_specs=[pl.BlockSpec((e, ts), lambda i: (0, i))],
        out_specs=[out, pl.BlockSpec((ts, GATE_COLS), lambda i: (i, 0)), out,
                   pl.BlockSpec((e, LANES), lambda i: (0, 0))],
        out_shape=[
            jax.ShapeDtypeStruct((TOP_K, t), I32),
            jax.ShapeDtypeStruct((t, GATE_COLS), F32),
            jax.ShapeDtypeStruct((TOP_K, t), I32),
            jax.ShapeDtypeStruct((e, LANES), F32),
        ],
        scratch_shapes=[pltpu.VMEM((e, 1), F32)],
        compiler_params=_params(("arbitrary",)),
        name="route_topk",
    )(logits_t)


def _dest_kernel(pstart_ref, eid_ref, pos_ref, dest_ref):
    eid = eid_ref[...]
    dest = pos_ref[...]
    for e in range(pstart_ref.shape[0]):
        dest = dest + jnp.where(eid == e, pstart_ref[e], 0)
    dest_ref[...] = dest


def _dest_call(pstart, eid, pos):
    k, t = eid.shape
    tt = min(DEST_TILE, t)
    spec = pl.BlockSpec((k, tt), lambda i, ps: (0, i))
    grid_spec = pltpu.PrefetchScalarGridSpec(
        num_scalar_prefetch=1, grid=(t // tt,), in_specs=[spec, spec], out_specs=spec)
    return pl.pallas_call(
        _dest_kernel,
        grid_spec=grid_spec,
        out_shape=jax.ShapeDtypeStruct((k, t), I32),
        compiler_params=_params(("arbitrary",)),
        name="route_dest",
    )(pstart, eid, pos)


def _sc_worker_chunks(total_chunks):
    worker = lax.axis_index("s") * SC_CORES + lax.axis_index("c")
    per_worker = total_chunks // SC_WORKERS
    return worker * per_worker, per_worker


def _sc_dispatch_call(hp, dest3, rows):
    t, w = hp.shape
    nchunk = dest3.shape[0]
    mesh = plsc.VectorSubcoreMesh(core_axis_name="c", subcore_axis_name="s")

    assert nchunk % (SC_BUFFERS * SC_WORKERS) == 0

    @functools.partial(
        pl.kernel, mesh=mesh, out_type=jax.ShapeDtypeStruct((rows, w), U32),
        scratch_types=[pltpu.VMEM((SC_BUFFERS, TOP_K, SC_CHUNK), I32),
                       pltpu.VMEM((SC_BUFFERS, SC_CHUNK, w), U32),
                       pltpu.SemaphoreType.DMA((SC_BUFFERS,)), pltpu.SemaphoreType.DMA((SC_BUFFERS,))],
        name="moe_dispatch")
    def dispatch(hp_hbm, dest_hbm, xs_hbm, idx_v, rows_v, load_sem, scatter_sem):
        first, count = _sc_worker_chunks(nchunk)

        def load(c, slot):
            return pltpu.make_async_copy(hp_hbm.at[pl.ds(c * SC_CHUNK, SC_CHUNK)], rows_v.at[slot],
                                         load_sem.at[slot])

        def scatter(slot, k):
            return pltpu.make_async_copy(rows_v.at[slot], xs_hbm.at[idx_v.at[slot, k]],
                                         scatter_sem.at[slot])

        def start_load(c, slot):
            pltpu.sync_copy(dest_hbm.at[c], idx_v.at[slot])
            load(c, slot).start()

        start_load(first, 0)

        @pl.loop(0, count, step=SC_BUFFERS)
        def _(j):
            for b in range(SC_BUFFERS):
                c = first + j + b
                nxt = (b + 1) % SC_BUFFERS

                @pl.when(j + b + 1 < count)
                def _():
                    @pl.when(j + b >= 1)
                    def _():
                        for k in range(TOP_K):
                            scatter(nxt, k).wait()
                    start_load(c + 1, nxt)

                load(c, b).wait()
                for k in range(TOP_K):
                    scatter(b, k).start()

        for slot in range(SC_BUFFERS):
            for k in range(TOP_K):
                scatter(slot, k).wait()

    return dispatch(hp, dest3)


def _sc_gather_call(ys, idx):
    n = idx.shape[0]
    _, w = ys.shape
    nchunk = n // SC_CHUNK
    mesh = plsc.VectorSubcoreMesh(core_axis_name="c", subcore_axis_name="s")

    assert nchunk % (SC_BUFFERS * SC_WORKERS) == 0

    @functools.partial(
        pl.kernel, mesh=mesh, out_type=jax.ShapeDtypeStruct((n, w), U32),
        scratch_types=[pltpu.VMEM((SC_BUFFERS, SC_CHUNK), I32), pltpu.VMEM((SC_BUFFERS, SC_CHUNK, w), U32),
                       pltpu.SemaphoreType.DMA((SC_BUFFERS,)), pltpu.SemaphoreType.DMA((SC_BUFFERS,))],
        name="moe_gather")
    def gather(ys_hbm, idx_hbm, out_hbm, idx_v, rows_v, gather_sem, write_sem):
        first, count = _sc_worker_chunks(nchunk)

        def fetch(slot):
            return pltpu.make_async_copy(ys_hbm.at[idx_v.at[slot]], rows_v.at[slot], gather_sem.at[slot])

        def write(c, slot):
            return pltpu.make_async_copy(rows_v.at[slot], out_hbm.at[pl.ds(c * SC_CHUNK, SC_CHUNK)],
                                         write_sem.at[slot])

        def start_fetch(c, slot):
            pltpu.sync_copy(idx_hbm.at[pl.ds(c * SC_CHUNK, SC_CHUNK)], idx_v.at[slot])
            fetch(slot).start()

        start_fetch(first, 0)

        @pl.loop(0, count, step=SC_BUFFERS)
        def _(j):
            for b in range(SC_BUFFERS):
                c = first + j + b
                nxt = (b + 1) % SC_BUFFERS

                @pl.when(j + b + 1 < count)
                def _():
                    @pl.when(j + b >= 1)
                    def _():
                        write(c - 1, nxt).wait()
                    start_fetch(c + 1, nxt)

                fetch(b).wait()
                write(c, b).start()

        for b in range(SC_BUFFERS):
            write(first + count - SC_BUFFERS + b, b).wait()

    return gather(ys, idx)


def _expert_kernel(be_ref, nv_ref, xs_ref, w1_ref, b1_ref, w2_ref, b2_ref, ys_ref, w1s, w2s):
    b = pl.program_id(0)
    e = be_ref[b]
    prev = be_ref[jnp.maximum(b - 1, 0)]
    nvalid = nv_ref[b]
    f = w2_ref.shape[1]

    @pl.when((b == 0) | (e != prev))
    def _():
        w1s[...] = w1_ref[0].astype(BF16)
        w2s[...] = w2_ref[0].astype(BF16)

    def ffn(r0, n):
        rows = r0 + lax.broadcasted_iota(I32, (n, 1), 0)
        xw = jnp.where(rows < nvalid, xs_ref[r0:r0 + n, :], jnp.uint32(0))
        x = _unpack_pairs(xw).astype(BF16)
        gu = jnp.dot(x, w1s[...], preferred_element_type=F32) + b1_ref[0]
        gate = jnp.minimum(gu[:, :f], SWIGLU_LIMIT)
        up = jnp.clip(gu[:, f:], -SWIGLU_LIMIT, SWIGLU_LIMIT)
        act = (up + 1.0) * (gate * _sigmoid(SWIGLU_ALPHA * gate))
        y = jnp.dot(act.astype(BF16), w2s[...], preferred_element_type=F32) + b2_ref[0]
        ys_ref[r0:r0 + n, :] = _pack_pairs(y)

    def blank(r0, n):
        ys_ref[r0:r0 + n, :] = jnp.zeros((n, ys_ref.shape[1]), U32)

    sub = EXPERT_SUBROWS
    half = sub // 2
    for r0 in range(0, xs_ref.shape[0], sub):
        @pl.when(nvalid > r0 + half)
        def _():
            ffn(r0, sub)

        @pl.when((nvalid > r0) & (nvalid <= r0 + half))
        def _():
            ffn(r0, half)
            blank(r0 + half, half)

        @pl.when(nvalid <= r0)
        def _():
            blank(r0, sub)


def _expert_call(layer, block_e, nvalid, xs, w_in, b_in, w_out, b_out):
    rows, w = xs.shape
    depth, ne, d, f2 = w_in.shape
    f = f2 // 2
    bm = EXPERT_ROWS
    x_spec = pl.BlockSpec((bm, w), lambda b, be, nv: (b, 0))
    grid_spec = pltpu.PrefetchScalarGridSpec(
        num_scalar_prefetch=2,
        grid=(rows // bm,),
        in_specs=[
            x_spec,
            pl.BlockSpec((1, d, f2), lambda b, be, nv: (layer * ne + be[b], 0, 0)),
            pl.BlockSpec((1, 1, f2), lambda b, be, nv: (layer * ne + be[b], 0, 0)),
            pl.BlockSpec((1, f, d), lambda b, be, nv: (layer * ne + be[b], 0, 0)),
            pl.BlockSpec((1, 1, d), lambda b, be, nv: (layer * ne + be[b], 0, 0)),
        ],
        out_specs=x_spec,
        scratch_shapes=[pltpu.VMEM((d, f2), BF16), pltpu.VMEM((f, d), BF16)],
    )
    return pl.pallas_call(
        _expert_kernel,
        grid_spec=grid_spec,
        out_shape=jax.ShapeDtypeStruct((rows, w), U32),
        compiler_params=_params(("arbitrary",)),
        name="moe_experts",
    )(block_e, nvalid, xs,
      w_in.reshape(depth * ne, d, f2), b_in.reshape(depth * ne, 1, f2),
      w_out.reshape(depth * ne, f, d), b_out.reshape(depth * ne, 1, d))


def _final_kernel(x_ref, yg_ref, gt_ref, g2, fg, xo_ref):
    xn = _moe_residual(x_ref, yg_ref, gt_ref, g2)
    r = lax.rsqrt(jnp.mean(xn * xn, axis=-1, keepdims=True) + EPS)
    xo_ref[0] = (xn * r) * fg[...]


def _final_call(x, pending, modr, fg):
    batch, seq, d = x.shape
    ts = COMBINE_TILE
    ns = seq // ts
    return pl.pallas_call(
        _final_kernel,
        grid=(batch, ns),
        in_specs=_residual_specs(pending, batch, ns, ts, d) + [pl.BlockSpec((1, d), lambda b, s: (0, 0))],
        out_specs=pl.BlockSpec((1, ts, d), lambda b, s: (b, s, 0)),
        out_shape=jax.ShapeDtypeStruct((batch, seq, d), F32),
        compiler_params=_params(("arbitrary", "arbitrary")),
        name="moe_combine_final",
    )(*_residual_args(x, pending, modr), fg)


def _moe(layer, hp, logits_t, w_in, b_in, w_out, b_out):
    t, w = hp.shape
    ne = logits_t.shape[0]
    bm = EXPERT_ROWS
    eid, gate_t, pos, cnt = _route_call(logits_t)
    counts = cnt[:, 0].astype(I32)
    nblk = (counts + bm - 1) // bm
    bend = jnp.cumsum(nblk)
    bstart = bend - nblk
    pstart = (bstart * bm).astype(I32)
    n_blocks = (t * TOP_K) // bm + ne
    blocks = jnp.arange(n_blocks, dtype=I32)
    block_e = jnp.minimum(jnp.sum(blocks[:, None] >= bend[None, :], axis=1), ne - 1).astype(I32)
    mine = block_e[:, None] == jnp.arange(ne, dtype=I32)[None, :]
    left = jnp.sum(jnp.where(mine, counts[None, :] - (blocks[:, None] - bstart[None, :]) * bm, 0), axis=1)
    nvalid = jnp.where(blocks < bend[-1], jnp.clip(left, 0, bm), 0).astype(I32)
    dest = _dest_call(pstart, eid, pos)
    dest3 = dest.reshape(TOP_K, t // SC_CHUNK, SC_CHUNK).transpose(1, 0, 2)
    xs = _sc_dispatch_call(hp, dest3, n_blocks * bm)
    ys = _expert_call(layer, block_e, nvalid, xs, w_in, b_in, w_out, b_out)
    yg = _sc_gather_call(ys, dest.reshape(TOP_K * t))
    return {"yg": yg.reshape(TOP_K, t, w), "gate_t": gate_t, "layer": layer}


def kernel(x, c, norm_mix_g, norm_ffn_g, ada_w, ada_b, pool_w, pool_scale, fox_w_in, fox_b_f, fox_w_o,
           router_w, router_b, exp_w_in, exp_b_in, exp_w_out, exp_b_out, final_g):
    batch, seq, d = x.shape
    depth = ada_w.shape[0]
    mod = _ada_call(c, ada_w, ada_b)
    modr = mod.reshape(depth * batch * 6, 1, d)
    fg = final_g.reshape(1, d)
    pending = None
    for i in range(depth):
        gm = norm_mix_g[i].reshape(1, d)
        gf = norm_ffn_g[i].reshape(1, d)
        rwt = router_w[i].T
        rb = router_b[i].reshape(-1, 1)
        j = i // 2
        if i % 2 == 0:
            x, hp, lg = _pool_call(i, x, pending, modr, gm, gf, pool_w[j], pool_scale[j].reshape(1, d),
                                   rwt, rb)
        else:
            q, k, v, stats, x = _qkv_call(i, x, pending, modr, gm, fox_w_in[j], fox_b_f[j].reshape(1, -1))
            o = _attn_call(q, k, v, stats)
            x, hp, lg = _wo_call(i, x, o, modr, gf, fox_w_o[j].astype(BF16), rwt, rb)
        pending = _moe(i, hp, lg, exp_w_in, exp_b_in, exp_w_out, exp_b_out)
    return _final_call(x, pending, modr, fg)
```

```python
import functools

import jax
import jax.numpy as jnp
import numpy as np
from jax import lax
from jax.experimental import pallas as pl
from jax.experimental.pallas import tpu as pltpu
from jax.experimental.pallas import tpu_sc as plsc

F32 = jnp.float32
BF16 = jnp.bfloat16
U32 = jnp.uint32
I32 = jnp.int32

POOL_WINDOWS = (2, 4, 8, 16)
POOL_HALO = 16
HEAD_DIM = 64
HEADS_PER_STEP = 2
TOP_K = 4
GATE_COLS = 8
SWIGLU_LIMIT = 7.0
SWIGLU_ALPHA = 1.702
EPS = 1e-6
NEG_BIG = -1e30
LOG2E = 1.4426950408889634
BIAS_PARTS = 3
UNDERFLOW_LOG2 = 160.0
NORM_SLACK = 1.02

LANES = 128
STAT_ROWS = 8
SEQ_TILE = 512
QKV_SUBROWS = 256
ATTN_TILE = 512
ATTN_QUERY_SUBTILES = 4
DEST_TILE = 4096
EXPERT_ROWS = 1024
EXPERT_SUBROWS = 512
COMBINE_TILE = 512
ADA_COLS = 1536
VMEM_LIMIT_BYTES = 56 * 1024 * 1024

SC_CORES = 2
SC_SUBCORES = 16
SC_WORKERS = SC_CORES * SC_SUBCORES
SC_CHUNK = 64
SC_BUFFERS = 2

NT_DIMS = (((1,), (1,)), ((), ()))


def _params(sem, vmem=VMEM_LIMIT_BYTES):
    return pltpu.CompilerParams(dimension_semantics=sem, vmem_limit_bytes=vmem)


def _sigmoid(z):
    return 1.0 / (1.0 + jnp.exp(-z))


def _norm_mod(x, g, scale, shift):
    r = lax.rsqrt(jnp.mean(x * x, axis=-1, keepdims=True) + EPS)
    return (x * r) * (g * (1.0 + scale)) + shift


def _pack_pairs(y):
    w = y.shape[1] // 2
    hi = lax.bitcast_convert_type(y[:, :w].astype(BF16).astype(F32), U32)
    lo = lax.bitcast_convert_type(y[:, w:].astype(BF16).astype(F32), U32)
    return hi | (lo >> 16)


def _unpack_pairs(w):
    hi = lax.bitcast_convert_type(w & jnp.uint32(0xFFFF0000), F32)
    lo = lax.bitcast_convert_type(w << 16, F32)
    return jnp.concatenate([hi, lo], axis=1)


def _split_bf16(x, parts):
    out = []
    r = x
    for _ in range(parts):
        p = r.astype(BF16)
        out.append(p)
        r = r - p.astype(F32)
    return out


def _ffn_prenorm(xn, gf, sc2, sh2, rwt, rb):
    h2 = _norm_mod(xn, gf, sc2, sh2)
    h_hi, h_lo = _split_bf16(h2, 2)
    w_hi, w_lo = _split_bf16(rwt, 2)
    lg = (lax.dot_general(w_hi, h_hi, NT_DIMS, preferred_element_type=F32)
          + lax.dot_general(w_lo, h_hi, NT_DIMS, preferred_element_type=F32)
          + lax.dot_general(w_hi, h_lo, NT_DIMS, preferred_element_type=F32)) + rb
    return _pack_pairs(h2), lg


def _mod_spec(layer, batch, j):
    def index(b, s):
        return ((layer * batch + b) * 6 + j, 0, 0)
    return index


def _moe_residual(x_ref, yg_ref, gt_ref, g2_ref, rows=slice(None)):
    gt = gt_ref[rows, :]
    moe = gt[:, 0:1] * _unpack_pairs(yg_ref[0, rows, :])
    for k in range(1, TOP_K):
        moe = moe + gt[:, k:k + 1] * _unpack_pairs(yg_ref[k, rows, :])
    return x_ref[0, rows, :] + g2_ref[0] * moe


def _residual_in(refs, pending, rows=slice(None)):
    if pending:
        return _moe_residual(*refs[:4], rows=rows)
    return refs[0][0, rows, :]


def _residual_rest(refs, pending):
    return refs[4:] if pending else refs[1:]


def _residual_specs(pending, batch, ns, ts, d):
    specs = [pl.BlockSpec((1, ts, d), lambda b, s: (b, s, 0))]
    if pending:
        _, t, w = pending["yg"].shape
        specs += [
            pl.BlockSpec((TOP_K, ts, w), lambda b, s: (0, b * ns + s, 0)),
            pl.BlockSpec((ts, GATE_COLS), lambda b, s: (b * ns + s, 0)),
            pl.BlockSpec((1, 1, d), _mod_spec(pending["layer"], batch, 5)),
        ]
    return specs


def _residual_args(x, pending, modr):
    return (x, pending["yg"], pending["gate_t"], modr) if pending else (x,)


def _ada_kernel(ct_ref, w_ref, b_ref, o_ref):
    ct = ct_ref[...]
    cond = ct * _sigmoid(ct)
    w = w_ref[0]
    rows = []
    for b in range(ct.shape[1]):
        rows.append(jnp.sum(w * cond[:, b:b + 1], axis=0, keepdims=True))
    o_ref[0] = jnp.concatenate(rows, axis=0) + b_ref[0]


def _ada_call(c, ada_w, ada_b):
    depth, d, n = ada_w.shape
    batch = c.shape[0]
    tn = ADA_COLS
    return pl.pallas_call(
        _ada_kernel,
        grid=(depth, n // tn),
        in_specs=[
            pl.BlockSpec((d, batch), lambda i, j: (0, 0)),
            pl.BlockSpec((1, d, tn), lambda i, j: (i, 0, j)),
            pl.BlockSpec((1, 1, tn), lambda i, j: (i, 0, j)),
        ],
        out_specs=pl.BlockSpec((1, batch, tn), lambda i, j: (i, 0, j)),
        out_shape=jax.ShapeDtypeStruct((depth, batch, n), F32),
        compiler_params=_params(("arbitrary", "arbitrary")),
        name="ada_mod",
    )(c.T, ada_w, ada_b.reshape(depth, 1, n))


def _pool_kernel(pending, *refs):
    x = _residual_in(refs, pending)
    refs = _residual_rest(refs, pending)
    (sh1, sc1, g1, sh2, sc2, gm, gf, pw_ref, ps_ref, rwt_ref, rb_ref,
     xo_ref, hp_ref, eid_ref, gate_ref, pos_ref, cnt_ref, hbuf, carry) = refs
    s = pl.program_id(1)
    ts, d = x.shape
    pg = d // len(POOL_WINDOWS)
    h = _norm_mod(x, gm[...], sc1[0], sh1[0])

    @pl.when(s == 0)
    def _():
        hbuf[0:POOL_HALO, :] = jnp.zeros((POOL_HALO, d), F32)

    @pl.when(s > 0)
    def _():
        hbuf[0:POOL_HALO, :] = hbuf[ts:ts + POOL_HALO, :]

    hbuf[POOL_HALO:POOL_HALO + ts, :] = h
    pos = s * ts + lax.broadcasted_iota(I32, (ts, 1), 0)
    ys = []
    for g, w in enumerate(POOL_WINDOWS):
        lo = g * pg
        hg = h[:, lo:lo + pg]
        win = hbuf[:, lo:lo + pg]
        span = 1
        while span < w:
            win = win + pltpu.roll(win, span, 0)
            span *= 2
        acc = win[POOL_HALO:, :]
        cnt = jnp.minimum(pos + 1, w).astype(F32)
        dlt = acc / cnt - hg
        ys.append(jnp.dot(dlt.astype(BF16), pw_ref[g].astype(BF16), preferred_element_type=F32))
    y = jnp.concatenate(ys, axis=1) * ps_ref[...]
    xn = x + g1[0] * y
    xo_ref[0] = xn
    hp, lg = _ffn_prenorm(xn, gf[...], sc2[0], sh2[0], rwt_ref[...], rb_ref[...])
    hp_ref[...] = hp
    _route_tile(lg, (pl.program_id(0) == 0) & (s == 0), eid_ref, gate_ref, pos_ref, cnt_ref, carry)


def _row_layer_specs(layer, batch, ns, ts, d, e):
    row = lambda j: pl.BlockSpec((1, 1, d), _mod_spec(layer, batch, j))
    vec = pl.BlockSpec((1, d), lambda b, s: (0, 0))
    x_spec = pl.BlockSpec((1, ts, d), lambda b, s: (b, s, 0))
    choice = pl.BlockSpec((TOP_K, ts), lambda b, s: (0, b * ns + s))
    out_specs = [
        x_spec,
        pl.BlockSpec((ts, d // 2), lambda b, s: (b * ns + s, 0)),
        choice,
        pl.BlockSpec((ts, GATE_COLS), lambda b, s: (b * ns + s, 0)),
        choice,
        pl.BlockSpec((e, LANES), lambda b, s: (0, 0)),
    ]
    return row, vec, x_spec, out_specs


def _row_layer_outs(batch, seq, d, e):
    t = batch * seq
    return [
        jax.ShapeDtypeStruct((batch, seq, d), F32),
        jax.ShapeDtypeStruct((t, d // 2), U32),
        jax.ShapeDtypeStruct((TOP_K, t), I32),
        jax.ShapeDtypeStruct((t, GATE_COLS), F32),
        jax.ShapeDtypeStruct((TOP_K, t), I32),
        jax.ShapeDtypeStruct((e, LANES), F32),
    ]


def _pool_call(layer, x, pending, modr, gm, gf, pw, ps, rwt, rb):
    batch, seq, d = x.shape
    e = rwt.shape[0]
    ts = SEQ_TILE
    ns = seq // ts
    g = len(POOL_WINDOWS)
    pg = d // g
    row, vec, _, out_specs = _row_layer_specs(layer, batch, ns, ts, d, e)
    return pl.pallas_call(
        functools.partial(_pool_kernel, bool(pending)),
        grid=(batch, ns),
        in_specs=_residual_specs(pending, batch, ns, ts, d) + [
            row(0), row(1), row(2), row(3), row(4), vec, vec,
            pl.BlockSpec((g, pg, pg), lambda b, s: (0, 0, 0)),
            vec,
            pl.BlockSpec((e, d), lambda b, s: (0, 0)),
            pl.BlockSpec((e, 1), lambda b, s: (0, 0)),
        ],
        out_specs=out_specs,
        out_shape=_row_layer_outs(batch, seq, d, e),
        scratch_shapes=[pltpu.VMEM((ts + POOL_HALO, d), F32), pltpu.VMEM((e, 1), F32)],
        compiler_params=_params(("arbitrary", "arbitrary")),
        name="pool_layer",
    )(*_residual_args(x, pending, modr), modr, modr, modr, modr, modr, gm, gf, pw, ps, rwt, rb)


def _aug_lane(head):
    return HEAD_DIM if head % 2 == 0 else 0


def _bias_placement(nh):
    place = np.zeros((BIAS_PARTS * nh, nh * LANES), np.float32)
    for j in range(BIAS_PARTS):
        for h in range(nh):
            place[j * nh + h, h * LANES + _aug_lane(h) + j] = 1.0
    return jnp.asarray(place, BF16)


def _qkv_kernel(pending, *refs):
    rest = _residual_rest(refs, pending)
    sh1, sc1, gm, w_ref, wf_ref, bf_ref, place_ref, seg_ref, q_ref, k_ref, v_ref, st_ref = rest[:12]
    carry = refs[-1]
    s = pl.program_id(1)
    ts = q_ref.shape[2]
    d = gm.shape[1]
    nh = bf_ref.shape[1]
    sub = QKV_SUBROWS

    @pl.when(s == 0)
    def _():
        carry[...] = jnp.zeros(carry.shape, F32)

    r = lax.broadcasted_iota(I32, (sub, sub), 0)
    c = lax.broadcasted_iota(I32, (sub, sub), 1)
    lower = jnp.where(r >= c, 1.0, 0.0).astype(BF16)
    lane = lax.broadcasted_iota(I32, (sub, LANES), 1)
    q_top = k_top = jnp.zeros((1, LANES), F32)
    b_top = jnp.full((1, nh), NEG_BIG, F32)
    for r0 in range(0, ts, sub):
        rows = slice(r0, r0 + sub)
        x = _residual_in(refs, pending, rows)
        if pending:
            rest[12][0, rows, :] = x
        h = _norm_mod(x, gm[...], sc1[0], sh1[0]).astype(BF16)
        fl = jnp.dot(h, wf_ref[...], preferred_element_type=F32)[:, :nh] + bf_ref[...]
        proj = jnp.dot(h, w_ref[...], preferred_element_type=F32)
        logf = jnp.minimum(fl, 0.0) - jnp.log(1.0 + jnp.exp(-jnp.abs(fl)))
        parts = jnp.concatenate(_split_bf16(logf, 3), axis=1)
        cs = jnp.dot(lower, parts, preferred_element_type=F32)
        cum = cs[:, :nh] + cs[:, nh:2 * nh] + cs[:, 2 * nh:] + carry[...]
        carry[...] = cum[sub - 1:sub, :]
        bias = jnp.concatenate(_split_bf16(cum * (-LOG2E), BIAS_PARTS), axis=1)
        k_aug = jnp.dot(bias, place_ref[...], preferred_element_type=F32)
        q_all = proj[:, :d] * (HEAD_DIM ** -0.5 * LOG2E)
        k_all = proj[:, d:2 * d]
        q_norm = jnp.sqrt(jnp.dot((q_all * q_all).astype(BF16), seg_ref[...], preferred_element_type=F32))
        k_norm = jnp.sqrt(jnp.dot((k_all * k_all).astype(BF16), seg_ref[...], preferred_element_type=F32))
        q_top = jnp.maximum(q_top, jnp.max(q_norm, axis=0, keepdims=True))
        k_top = jnp.maximum(k_top, jnp.max(k_norm, axis=0, keepdims=True))
        b_top = jnp.maximum(b_top, jnp.max(cum * (-LOG2E), axis=0, keepdims=True))
        for hd in range(nh):
            lo = (hd // 2) * LANES
            a0 = _aug_lane(hd)
            real = (lane < HEAD_DIM) if hd % 2 == 0 else (lane >= HEAD_DIM)
            q_aug = jnp.where((lane >= a0) & (lane < a0 + BIAS_PARTS), 1.0, 0.0)
            v_aug = jnp.where(lane == a0, 1.0, 0.0)
            q_ref[0, hd, rows, :] = jnp.where(real, q_all[:, lo:lo + LANES], q_aug).astype(BF16)
            k_ref[0, hd, rows, :] = jnp.where(real, proj[:, d + lo:d + lo + LANES],
                                              k_aug[:, hd * LANES:(hd + 1) * LANES]).astype(BF16)
            v_ref[0, hd, rows, :] = jnp.where(real, proj[:, 2 * d + lo:2 * d + lo + LANES],
                                              v_aug).astype(BF16)
    b_row = jnp.concatenate([b_top, jnp.zeros((1, LANES - nh), F32)], axis=1)
    st_ref[0, 0] = jnp.concatenate([q_top, k_top, b_row, jnp.zeros((STAT_ROWS - 3, LANES), F32)], axis=0)


def _qkv_call(layer, x, pending, modr, gm, w_in, bf):
    batch, seq, d = x.shape
    nh = bf.shape[1]
    ts = SEQ_TILE
    ns = seq // ts
    w_qkv = w_in[:, :3 * d].astype(BF16)
    w_f = jnp.pad(w_in[:, 3 * d:], ((0, 0), (0, LANES - nh))).astype(BF16)
    place = _bias_placement(nh)
    seg = jnp.asarray(np.arange(d)[:, None] // HEAD_DIM == np.arange(LANES)[None, :], BF16)
    row = lambda j: pl.BlockSpec((1, 1, d), _mod_spec(layer, batch, j))
    head_spec = pl.BlockSpec((1, nh, ts, LANES), lambda b, s: (b, 0, s, 0))
    out_specs = [head_spec] * 3 + [pl.BlockSpec((1, 1, STAT_ROWS, LANES), lambda b, s: (b, s, 0, 0))]
    out_shape = [jax.ShapeDtypeStruct((batch, nh, seq, LANES), BF16)] * 3 + [
        jax.ShapeDtypeStruct((batch, ns, STAT_ROWS, LANES), F32)]
    if pending:
        out_specs = out_specs + [pl.BlockSpec((1, ts, d), lambda b, s: (b, s, 0))]
        out_shape = out_shape + [jax.ShapeDtypeStruct((batch, seq, d), F32)]
    outs = pl.pallas_call(
        functools.partial(_qkv_kernel, bool(pending)),
        grid=(batch, ns),
        in_specs=_residual_specs(pending, batch, ns, ts, d) + [
            row(0), row(1),
            pl.BlockSpec((1, d), lambda b, s: (0, 0)),
            pl.BlockSpec(w_qkv.shape, lambda b, s: (0, 0)),
            pl.BlockSpec(w_f.shape, lambda b, s: (0, 0)),
            pl.BlockSpec((1, nh), lambda b, s: (0, 0)),
            pl.BlockSpec(place.shape, lambda b, s: (0, 0)),
            pl.BlockSpec(seg.shape, lambda b, s: (0, 0)),
        ],
        out_specs=out_specs,
        out_shape=out_shape,
        scratch_shapes=[pltpu.VMEM((1, nh), F32)],
        compiler_params=_params(("arbitrary", "arbitrary")),
        name="fox_qkv",
    )(*_residual_args(x, pending, modr), modr, modr, gm, w_qkv, w_f, bf, place, seg)
    return outs[0], outs[1], outs[2], outs[3], (outs[4] if pending else x)


def _attn_kernel(q_ref, k_ref, v_ref, st_ref, o_ref, m_s, acc_s, level_s):
    hp = pl.program_id(1)
    qi = pl.program_id(2)
    tk = ATTN_TILE
    tq = q_ref.shape[2]
    subs = tq // tk
    m_s[...] = jnp.full(m_s.shape, NEG_BIG, F32)
    acc_s[...] = jnp.zeros(acc_s.shape, F32)
    row = lax.broadcasted_iota(I32, (tk, tk), 0)
    col = lax.broadcasted_iota(I32, (tk, tk), 1)

    def step(kb, lo, hi, diagonal):
        start = pl.multiple_of(kb * tk, tk)
        rows = slice(lo * tk, hi * tk)
        for hh in range(HEADS_PER_STEP):
            kk = k_ref[0, hh, pl.ds(start, tk), :]
            vv = v_ref[0, hh, pl.ds(start, tk), :]
            z = lax.dot_general(q_ref[0, hh, rows, :], kk, NT_DIMS, preferred_element_type=F32)
            if diagonal:
                masked = jnp.where(row >= col, z[:tk], NEG_BIG)
                z = masked if hi - lo == 1 else jnp.concatenate([masked, z[tk:]], axis=0)
            m_prev = m_s[hh, rows, :]
            m_new = jnp.maximum(m_prev, jnp.max(z, axis=1, keepdims=True))
            p = jnp.exp2(z - jnp.concatenate([m_new] * (tk // LANES), axis=1))
            acc_s[hh, rows, :] = (jnp.exp2(m_prev - m_new) * acc_s[hh, rows, :]
                                  + jnp.dot(p.astype(BF16), vv, preferred_element_type=F32))
            m_s[hh, rows, :] = m_new

    for j in range(subs):
        step(subs * qi + j, j, subs, True)

    st = st_ref[0]
    nt = st.shape[0]
    stat = lax.broadcasted_iota(I32, st.shape, 1)
    q_top = jnp.max(jnp.where(stat == 0, st, 0.0), axis=1)
    k_top = jnp.max(jnp.where(stat == 1, st, 0.0), axis=1)
    b_top = jnp.max(jnp.where(stat == 2, st, NEG_BIG), axis=1)
    tile = lax.broadcasted_iota(I32, (nt, LANES), 0)
    head = lax.broadcasted_iota(I32, (nt, LANES), 1)
    level = jnp.zeros((nt, LANES), F32)
    for hh in range(HEADS_PER_STEP):
        earlier = (head == HEADS_PER_STEP * hp + hh) & (tile < subs * qi)
        for j in range(subs):
            q_here = jnp.max(jnp.where(tile == subs * qi + j, q_top, 0.0), axis=0, keepdims=True)
            floor = jnp.min(m_s[hh, j * tk:(j + 1) * tk, :]) - UNDERFLOW_LOG2
            needed = (NORM_SLACK * q_here * k_top + b_top + 1.0 >= floor) & earlier
            level = jnp.maximum(level, jnp.where(needed, float(j + 1), 0.0))
    level = jnp.max(level, axis=1, keepdims=True).astype(I32)
    for t in range(nt):
        level_s[t] = level[t, 0]

    spans = sorted({1, max(subs // 2, 1), subs})

    def body(kb, carry):
        need = level_s[kb]
        below = 0
        for hi in spans:
            @pl.when((need > below) & (need <= hi))
            def _():
                step(kb, 0, hi, False)
            below = hi
        return carry

    lax.fori_loop(0, subs * qi, body, 0)

    lane = lax.broadcasted_iota(I32, (tq, LANES), 1)
    outs = []
    for hh in range(HEADS_PER_STEP):
        a = acc_s[hh]
        outs.append(a / a[:, _aug_lane(hh):_aug_lane(hh) + 1])
    o_ref[0] = jnp.where(lane < HEAD_DIM, outs[0], outs[1]).astype(BF16)


def _attn_call(q, k, v, stats):
    batch, nh, seq, _ = q.shape
    tq = ATTN_QUERY_SUBTILES * ATTN_TILE
    kv_spec = pl.BlockSpec((1, HEADS_PER_STEP, seq, LANES), lambda b, h, i: (b, h, 0, 0))
    return pl.pallas_call(
        _attn_kernel,
        grid=(batch, nh // HEADS_PER_STEP, seq // tq),
        in_specs=[pl.BlockSpec((1, HEADS_PER_STEP, tq, LANES), lambda b, h, i: (b, h, i, 0)),
                  kv_spec, kv_spec,
                  pl.BlockSpec((1,) + stats.shape[1:], lambda b, h, i: (b, 0, 0, 0))],
        out_specs=pl.BlockSpec((1, tq, LANES), lambda b, h, i: (b, i, h)),
        out_shape=jax.ShapeDtypeStruct((batch, seq, nh * HEAD_DIM), BF16),
        scratch_shapes=[pltpu.VMEM((HEADS_PER_STEP, tq, LANES), F32)] * 2 + [pltpu.SMEM((stats.shape[1],), I32)],
        compiler_params=_params(("arbitrary", "arbitrary", "arbitrary")),
        name="fox_attention",
    )(q, k, v, stats)


def _wo_kernel(x_ref, o_ref, g1, sh2, sc2, gf, wo_ref, rwt_ref, rb_ref,
               xo_ref, hp_ref, eid_ref, gate_ref, pos_ref, cnt_ref, carry):
    m = jnp.dot(o_ref[0], wo_ref[...], preferred_element_type=F32)
    xn = x_ref[0] + g1[0] * m
    xo_ref[0] = xn
    hp, lg = _ffn_prenorm(xn, gf[...], sc2[0], sh2[0], rwt_ref[...], rb_ref[...])
    hp_ref[...] = hp
    first = (pl.program_id(0) == 0) & (pl.program_id(1) == 0)
    _route_tile(lg, first, eid_ref, gate_ref, pos_ref, cnt_ref, carry)


def _wo_call(layer, x, o, modr, gf, wo, rwt, rb):
    batch, seq, d = x.shape
    e = rwt.shape[0]
    ts = SEQ_TILE
    ns = seq // ts
    row, vec, x_spec, out_specs = _row_layer_specs(layer, batch, ns, ts, d, e)
    return pl.pallas_call(
        _wo_kernel,
        grid=(batch, ns),
        in_specs=[
            x_spec, x_spec, row(2), row(3), row(4), vec,
            pl.BlockSpec((d, d), lambda b, s: (0, 0)),
            pl.BlockSpec((e, d), lambda b, s: (0, 0)),
            pl.BlockSpec((e, 1), lambda b, s: (0, 0)),
        ],
        out_specs=out_specs,
        out_shape=_row_layer_outs(batch, seq, d, e),
        scratch_shapes=[pltpu.VMEM((e, 1), F32)],
        compiler_params=_params(("arbitrary", "arbitrary")),
        name="fox_out",
    )(x, o, modr, modr, modr, gf, wo, rwt, rb)


def _route_tile(l, first, eid_ref, gate_ref, pos_ref, cnt_ref, carry):
    @pl.when(first)
    def _():
        carry[...] = jnp.zeros(carry.shape, F32)

    e, ts = l.shape
    eidx = lax.broadcasted_iota(I32, (e, ts), 0)
    work = l
    top_v, top_i, hot = [], [], []
    for _ in range(TOP_K):
        m = jnp.max(work, axis=0, keepdims=True)
        sel = jnp.min(jnp.where(work == m, eidx, e), axis=0, keepdims=True)
        o = eidx == sel
        top_v.append(m)
        top_i.append(sel)
        hot.append(o)
        work = jnp.where(o, -jnp.inf, work)
    ex = [jnp.exp(v - top_v[0]) for v in top_v]
    den = ex[0] + ex[1] + ex[2] + ex[3]
    chosen = jnp.where(hot[0] | hot[1] | hot[2] | hot[3], 1.0, 0.0)
    r = lax.broadcasted_iota(I32, (ts, ts), 0)
    c = lax.broadcasted_iota(I32, (ts, ts), 1)
    before = jnp.where(r < c, 1.0, 0.0).astype(BF16)
    rank = jnp.dot(chosen.astype(BF16), before, preferred_element_type=F32) + carry[...]
    pos = [jnp.sum(jnp.where(o, rank, 0.0), axis=0, keepdims=True) for o in hot]
    carry[...] = carry[...] + jnp.sum(chosen, axis=1, keepdims=True)
    eid_ref[...] = jnp.concatenate(top_i, axis=0)
    gates = [x / den for x in ex] + [jnp.zeros((GATE_COLS - TOP_K, ts), F32)]
    gate_ref[...] = jnp.concatenate(gates, axis=0).T
    pos_ref[...] = jnp.concatenate(pos, axis=0).astype(I32)
    cnt_ref[...] = jnp.broadcast_to(carry[...], cnt_ref.shape)


def _dest_kernel(pstart_ref, eid_ref, pos_ref, dest_ref):
    eid = eid_ref[...]
    dest = pos_ref[...]
    for e in range(pstart_ref.shape[0]):
        dest = dest + jnp.where(eid == e, pstart_ref[e], 0)
    dest_ref[...] = dest


def _dest_call(pstart, eid, pos):
    k, t = eid.shape
    tt = min(DEST_TILE, t)
    spec = pl.BlockSpec((k, tt), lambda i, ps: (0, i))
    grid_spec = pltpu.PrefetchScalarGridSpec(
        num_scalar_prefetch=1, grid=(t // tt,), in_specs=[spec, spec], out_specs=spec)
    return pl.pallas_call(
        _dest_kernel,
        grid_spec=grid_spec,
        out_shape=jax.ShapeDtypeStruct((k, t), I32),
        compiler_params=_params(("arbitrary",)),
        name="route_dest",
    )(pstart, eid, pos)


def _sc_worker_chunks(total_chunks):
    worker = lax.axis_index("s") * SC_CORES + lax.axis_index("c")
    per_worker = total_chunks // SC_WORKERS
    return worker * per_worker, per_worker


def _sc_dispatch_call(hp, dest3, rows):
    t, w = hp.shape
    nchunk = dest3.shape[0]
    mesh = plsc.VectorSubcoreMesh(core_axis_name="c", subcore_axis_name="s")

    assert nchunk % (SC_BUFFERS * SC_WORKERS) == 0

    @functools.partial(
        pl.kernel, mesh=mesh, out_type=jax.ShapeDtypeStruct((rows, w), U32),
        scratch_types=[pltpu.VMEM((SC_BUFFERS, TOP_K, SC_CHUNK), I32),
                       pltpu.VMEM((SC_BUFFERS, SC_CHUNK, w), U32),
                       pltpu.SemaphoreType.DMA((SC_BUFFERS,)), pltpu.SemaphoreType.DMA((SC_BUFFERS,))],
        name="moe_dispatch")
    def dispatch(hp_hbm, dest_hbm, xs_hbm, idx_v, rows_v, load_sem, scatter_sem):
        first, count = _sc_worker_chunks(nchunk)

        def load(c, slot):
            return pltpu.make_async_copy(hp_hbm.at[pl.ds(c * SC_CHUNK, SC_CHUNK)], rows_v.at[slot],
                                         load_sem.at[slot])

        def scatter(slot, k):
            return pltpu.make_async_copy(rows_v.at[slot], xs_hbm.at[idx_v.at[slot, k]],
                                         scatter_sem.at[slot])

        def start_load(c, slot):
            pltpu.sync_copy(dest_hbm.at[c], idx_v.at[slot])
            load(c, slot).start()

        start_load(first, 0)

        @pl.loop(0, count, step=SC_BUFFERS)
        def _(j):
            for b in range(SC_BUFFERS):
                c = first + j + b
                nxt = (b + 1) % SC_BUFFERS

                @pl.when(j + b + 1 < count)
                def _():
                    @pl.when(j + b >= 1)
                    def _():
                        for k in range(TOP_K):
                            scatter(nxt, k).wait()
                    start_load(c + 1, nxt)

                load(c, b).wait()
                for k in range(TOP_K):
                    scatter(b, k).start()

        for slot in range(SC_BUFFERS):
            for k in range(TOP_K):
                scatter(slot, k).wait()

    return dispatch(hp, dest3)


def _sc_gather_call(ys, idx):
    n = idx.shape[0]
    _, w = ys.shape
    nchunk = n // SC_CHUNK
    mesh = plsc.VectorSubcoreMesh(core_axis_name="c", subcore_axis_name="s")

    assert nchunk % (SC_BUFFERS * SC_WORKERS) == 0

    @functools.partial(
        pl.kernel, mesh=mesh, out_type=jax.ShapeDtypeStruct((n, w), U32),
        scratch_types=[pltpu.VMEM((SC_BUFFERS, SC_CHUNK), I32), pltpu.VMEM((SC_BUFFERS, SC_CHUNK, w), U32),
                       pltpu.SemaphoreType.DMA((SC_BUFFERS,)), pltpu.SemaphoreType.DMA((SC_BUFFERS,))],
        name="moe_gather")
    def gather(ys_hbm, idx_hbm, out_hbm, idx_v, rows_v, gather_sem, write_sem):
        first, count = _sc_worker_chunks(nchunk)

        def fetch(slot):
            return pltpu.make_async_copy(ys_hbm.at[idx_v.at[slot]], rows_v.at[slot], gather_sem.at[slot])

        def write(c, slot):
            return pltpu.make_async_copy(rows_v.at[slot], out_hbm.at[pl.ds(c * SC_CHUNK, SC_CHUNK)],
                                         write_sem.at[slot])

        def start_fetch(c, slot):
            pltpu.sync_copy(idx_hbm.at[pl.ds(c * SC_CHUNK, SC_CHUNK)], idx_v.at[slot])
            fetch(slot).start()

        start_fetch(first, 0)

        @pl.loop(0, count, step=SC_BUFFERS)
        def _(j):
            for b in range(SC_BUFFERS):
                c = first + j + b
                nxt = (b + 1) % SC_BUFFERS

                @pl.when(j + b + 1 < count)
                def _():
                    @pl.when(j + b >= 1)
                    def _():
                        write(c - 1, nxt).wait()
                    start_fetch(c + 1, nxt)

                fetch(b).wait()
                write(c, b).start()

        for b in range(SC_BUFFERS):
            write(first + count - SC_BUFFERS + b, b).wait()

    return gather(ys, idx)


def _expert_kernel(be_ref, nv_ref, xs_ref, w1_ref, b1_ref, w2_ref, b2_ref, ys_ref, w1s, w2s):
    b = pl.program_id(0)
    e = be_ref[b]
    prev = be_ref[jnp.maximum(b - 1, 0)]
    nvalid = nv_ref[b]
    f = w2_ref.shape[1]

    @pl.when((b == 0) | (e != prev))
    def _():
        w1s[...] = w1_ref[0].astype(BF16)
        w2s[...] = w2_ref[0].astype(BF16)

    def ffn(r0, n):
        rows = r0 + lax.broadcasted_iota(I32, (n, 1), 0)
        xw = jnp.where(rows < nvalid, xs_ref[r0:r0 + n, :], jnp.uint32(0))
        x = _unpack_pairs(xw).astype(BF16)
        gu = jnp.dot(x, w1s[...], preferred_element_type=F32) + b1_ref[0]
        gate = jnp.minimum(gu[:, :f], SWIGLU_LIMIT)
        up = jnp.clip(gu[:, f:], -SWIGLU_LIMIT, SWIGLU_LIMIT)
        act = (up + 1.0) * (gate * _sigmoid(SWIGLU_ALPHA * gate))
        y = jnp.dot(act.astype(BF16), w2s[...], preferred_element_type=F32) + b2_ref[0]
        ys_ref[r0:r0 + n, :] = _pack_pairs(y)

    def blank(r0, n):
        ys_ref[r0:r0 + n, :] = jnp.zeros((n, ys_ref.shape[1]), U32)

    sub = EXPERT_SUBROWS
    half = sub // 2
    for r0 in range(0, xs_ref.shape[0], sub):
        @pl.when(nvalid > r0 + half)
        def _():
            ffn(r0, sub)

        @pl.when((nvalid > r0) & (nvalid <= r0 + half))
        def _():
            ffn(r0, half)
            blank(r0 + half, half)

        @pl.when(nvalid <= r0)
        def _():
            blank(r0, sub)


def _expert_call(layer, block_e, nvalid, xs, w_in, b_in, w_out, b_out):
    rows, w = xs.shape
    depth, ne, d, f2 = w_in.shape
    f = f2 // 2
    bm = EXPERT_ROWS
    x_spec = pl.BlockSpec((bm, w), lambda b, be, nv: (b, 0))
    grid_spec = pltpu.PrefetchScalarGridSpec(
        num_scalar_prefetch=2,
        grid=(rows // bm,),
        in_specs=[
            x_spec,
            pl.BlockSpec((1, d, f2), lambda b, be, nv: (layer * ne + be[b], 0, 0)),
            pl.BlockSpec((1, 1, f2), lambda b, be, nv: (layer * ne + be[b], 0, 0)),
            pl.BlockSpec((1, f, d), lambda b, be, nv: (layer * ne + be[b], 0, 0)),
            pl.BlockSpec((1, 1, d), lambda b, be, nv: (layer * ne + be[b], 0, 0)),
        ],
        out_specs=x_spec,
        scratch_shapes=[pltpu.VMEM((d, f2), BF16), pltpu.VMEM((f, d), BF16)],
    )
    return pl.pallas_call(
        _expert_kernel,
        grid_spec=grid_spec,
        out_shape=jax.ShapeDtypeStruct((rows, w), U32),
        compiler_params=_params(("arbitrary",)),
        name="moe_experts",
    )(block_e, nvalid, xs,
      w_in.reshape(depth * ne, d, f2), b_in.reshape(depth * ne, 1, f2),
      w_out.reshape(depth * ne, f, d), b_out.reshape(depth * ne, 1, d))


def _final_kernel(x_ref, yg_ref, gt_ref, g2, fg, xo_ref):
    xn = _moe_residual(x_ref, yg_ref, gt_ref, g2)
    r = lax.rsqrt(jnp.mean(xn * xn, axis=-1, keepdims=True) + EPS)
    xo_ref[0] = (xn * r) * fg[...]


def _final_call(x, pending, modr, fg):
    batch, seq, d = x.shape
    ts = COMBINE_TILE
    ns = seq // ts
    return pl.pallas_call(
        _final_kernel,
        grid=(batch, ns),
        in_specs=_residual_specs(pending, batch, ns, ts, d) + [pl.BlockSpec((1, d), lambda b, s: (0, 0))],
        out_specs=pl.BlockSpec((1, ts, d), lambda b, s: (b, s, 0)),
        out_shape=jax.ShapeDtypeStruct((batch, seq, d), F32),
        compiler_params=_params(("arbitrary", "arbitrary")),
        name="moe_combine_final",
    )(*_residual_args(x, pending, modr), fg)


def _moe(layer, hp, eid, gate_t, pos, cnt, w_in, b_in, w_out, b_out):
    t, w = hp.shape
    ne = cnt.shape[0]
    bm = EXPERT_ROWS
    counts = cnt[:, 0].astype(I32)
    nblk = (counts + bm - 1) // bm
    bend = jnp.cumsum(nblk)
    bstart = bend - nblk
    pstart = (bstart * bm).astype(I32)
    n_blocks = (t * TOP_K) // bm + ne
    blocks = jnp.arange(n_blocks, dtype=I32)
    block_e = jnp.minimum(jnp.sum(blocks[:, None] >= bend[None, :], axis=1), ne - 1).astype(I32)
    mine = block_e[:, None] == jnp.arange(ne, dtype=I32)[None, :]
    left = jnp.sum(jnp.where(mine, counts[None, :] - (blocks[:, None] - bstart[None, :]) * bm, 0), axis=1)
    nvalid = jnp.where(blocks < bend[-1], jnp.clip(left, 0, bm), 0).astype(I32)
    dest = _dest_call(pstart, eid, pos)
    dest3 = dest.reshape(TOP_K, t // SC_CHUNK, SC_CHUNK).transpose(1, 0, 2)
    xs = _sc_dispatch_call(hp, dest3, n_blocks * bm)
    ys = _expert_call(layer, block_e, nvalid, xs, w_in, b_in, w_out, b_out)
    yg = _sc_gather_call(ys, dest.reshape(TOP_K * t))
    return {"yg": yg.reshape(TOP_K, t, w), "gate_t": gate_t, "layer": layer}


def kernel(x, c, norm_mix_g, norm_ffn_g, ada_w, ada_b, pool_w, pool_scale, fox_w_in, fox_b_f, fox_w_o,
           router_w, router_b, exp_w_in, exp_b_in, exp_w_out, exp_b_out, final_g):
    batch, seq, d = x.shape
    depth = ada_w.shape[0]
    mod = _ada_call(c, ada_w, ada_b)
    modr = mod.reshape(depth * batch * 6, 1, d)
    fg = final_g.reshape(1, d)
    pending = None
    for i in range(depth):
        gm = norm_mix_g[i].reshape(1, d)
        gf = norm_ffn_g[i].reshape(1, d)
        rwt = router_w[i].T
        rb = router_b[i].reshape(-1, 1)
        j = i // 2
        if i % 2 == 0:
            x, hp, *routing = _pool_call(i, x, pending, modr, gm, gf, pool_w[j],
                                         pool_scale[j].reshape(1, d), rwt, rb)
        else:
            q, k, v, stats, x = _qkv_call(i, x, pending, modr, gm, fox_w_in[j], fox_b_f[j].reshape(1, -1))
            o = _attn_call(q, k, v, stats)
            x, hp, *routing = _wo_call(i, x, o, modr, gf, fox_w_o[j].astype(BF16), rwt, rb)
        pending = _moe(i, hp, *routing, exp_w_in, exp_b_in, exp_w_out, exp_b_out)
    return _final_call(x, pending, modr, fg)
```

```python
import functools

import jax
import jax.numpy as jnp
import numpy as np
from jax import lax
from jax.experimental import pallas as pl
from jax.experimental.pallas import tpu as pltpu
from jax.experimental.pallas import tpu_sc as plsc

F32 = jnp.float32
BF16 = jnp.bfloat16
U32 = jnp.uint32
I32 = jnp.int32

POOL_WINDOWS = (2, 4, 8, 16)
POOL_HALO = 16
HEAD_DIM = 64
HEADS_PER_STEP = 2
TOP_K = 4
GATE_COLS = 8
SWIGLU_LIMIT = 7.0
SWIGLU_ALPHA = 1.702
EPS = 1e-6
NEG_BIG = -1e30
LOG2E = 1.4426950408889634
BIAS_PARTS = 3
UNDERFLOW_LOG2 = 160.0
NORM_SLACK = 1.02

LANES = 128
STAT_ROWS = 8
SEQ_TILE = 512
QKV_SUBROWS = 256
ATTN_TILE = 512
ATTN_QUERY_SUBTILES = 4
DEST_TILE = 4096
EXPERT_ROWS = 1024
EXPERT_SUBROWS = 512
COMBINE_TILE = 1024
ADA_COLS = 1536
VMEM_LIMIT_BYTES = 56 * 1024 * 1024

SC_CORES = 2
SC_SUBCORES = 16
SC_WORKERS = SC_CORES * SC_SUBCORES
SC_CHUNK = 64
SC_BUFFERS = 2

NT_DIMS = (((1,), (1,)), ((), ()))


def _params(sem, vmem=VMEM_LIMIT_BYTES):
    return pltpu.CompilerParams(dimension_semantics=sem, vmem_limit_bytes=vmem)


def _sigmoid(z):
    return 1.0 / (1.0 + jnp.exp(-z))


def _norm_mod(x, g, scale, shift):
    r = lax.rsqrt(jnp.mean(x * x, axis=-1, keepdims=True) + EPS)
    return (x * r) * (g * (1.0 + scale)) + shift


def _pack_pairs(y):
    w = y.shape[1] // 2
    hi = lax.bitcast_convert_type(y[:, :w].astype(BF16).astype(F32), U32)
    lo = lax.bitcast_convert_type(y[:, w:].astype(BF16).astype(F32), U32)
    return hi | (lo >> 16)


def _unpack_pairs(w):
    hi = lax.bitcast_convert_type(w & jnp.uint32(0xFFFF0000), F32)
    lo = lax.bitcast_convert_type(w << 16, F32)
    return jnp.concatenate([hi, lo], axis=1)


def _split_bf16(x, parts):
    out = []
    r = x
    for _ in range(parts):
        p = r.astype(BF16)
        out.append(p)
        r = r - p.astype(F32)
    return out


def _ffn_prenorm(xn, gf, sc2, sh2, rwt, rb):
    h2 = _norm_mod(xn, gf, sc2, sh2)
    h_hi, h_lo = _split_bf16(h2, 2)
    w_hi, w_lo = _split_bf16(rwt, 2)
    lg = (lax.dot_general(w_hi, h_hi, NT_DIMS, preferred_element_type=F32)
          + lax.dot_general(w_lo, h_hi, NT_DIMS, preferred_element_type=F32)
          + lax.dot_general(w_hi, h_lo, NT_DIMS, preferred_element_type=F32)) + rb
    return _pack_pairs(h2), lg


def _mod_spec(layer, batch, j):
    def index(b, s):
        return ((layer * batch + b) * 6 + j, 0, 0)
    return index


def _moe_residual(x_ref, yg_ref, gt_ref, g2_ref, rows=slice(None)):
    gt = gt_ref[rows, :]
    moe = gt[:, 0:1] * _unpack_pairs(yg_ref[0, rows, :])
    for k in range(1, TOP_K):
        moe = moe + gt[:, k:k + 1] * _unpack_pairs(yg_ref[k, rows, :])
    return x_ref[0, rows, :] + g2_ref[0] * moe


def _residual_in(refs, pending, rows=slice(None)):
    if pending:
        return _moe_residual(*refs[:4], rows=rows)
    return refs[0][0, rows, :]


def _residual_rest(refs, pending):
    return refs[4:] if pending else refs[1:]


def _residual_specs(pending, batch, ns, ts, d):
    specs = [pl.BlockSpec((1, ts, d), lambda b, s: (b, s, 0))]
    if pending:
        _, t, w = pending["yg"].shape
        specs += [
            pl.BlockSpec((TOP_K, ts, w), lambda b, s: (0, b * ns + s, 0)),
            pl.BlockSpec((ts, GATE_COLS), lambda b, s: (b * ns + s, 0)),
            pl.BlockSpec((1, 1, d), _mod_spec(pending["layer"], batch, 5)),
        ]
    return specs


def _residual_args(x, pending, modr):
    return (x, pending["yg"], pending["gate_t"], modr) if pending else (x,)


def _ada_kernel(ct_ref, w_ref, b_ref, o_ref):
    ct = ct_ref[...]
    cond = ct * _sigmoid(ct)
    w = w_ref[0]
    rows = []
    for b in range(ct.shape[1]):
        rows.append(jnp.sum(w * cond[:, b:b + 1], axis=0, keepdims=True))
    o_ref[0] = jnp.concatenate(rows, axis=0) + b_ref[0]


def _ada_call(c, ada_w, ada_b):
    depth, d, n = ada_w.shape
    batch = c.shape[0]
    tn = ADA_COLS
    return pl.pallas_call(
        _ada_kernel,
        grid=(depth, n // tn),
        in_specs=[
            pl.BlockSpec((d, batch), lambda i, j: (0, 0)),
            pl.BlockSpec((1, d, tn), lambda i, j: (i, 0, j)),
            pl.BlockSpec((1, 1, tn), lambda i, j: (i, 0, j)),
        ],
        out_specs=pl.BlockSpec((1, batch, tn), lambda i, j: (i, 0, j)),
        out_shape=jax.ShapeDtypeStruct((depth, batch, n), F32),
        compiler_params=_params(("arbitrary", "arbitrary")),
        name="ada_mod",
    )(c.T, ada_w, ada_b.reshape(depth, 1, n))


def _pool_kernel(pending, *refs):
    x = _residual_in(refs, pending)
    refs = _residual_rest(refs, pending)
    (sh1, sc1, g1, sh2, sc2, gm, gf, pw_ref, ps_ref, rwt_ref, rb_ref,
     xo_ref, hp_ref, eid_ref, gate_ref, pos_ref, cnt_ref, hbuf, carry) = refs
    s = pl.program_id(1)
    ts, d = x.shape
    pg = d // len(POOL_WINDOWS)
    h = _norm_mod(x, gm[...], sc1[0], sh1[0])

    @pl.when(s == 0)
    def _():
        hbuf[0:POOL_HALO, :] = jnp.zeros((POOL_HALO, d), F32)

    @pl.when(s > 0)
    def _():
        hbuf[0:POOL_HALO, :] = hbuf[ts:ts + POOL_HALO, :]

    hbuf[POOL_HALO:POOL_HALO + ts, :] = h
    pos = s * ts + lax.broadcasted_iota(I32, (ts, 1), 0)
    ys = []
    for g, w in enumerate(POOL_WINDOWS):
        lo = g * pg
        hg = h[:, lo:lo + pg]
        win = hbuf[:, lo:lo + pg]
        span = 1
        while span < w:
            win = win + pltpu.roll(win, span, 0)
            span *= 2
        acc = win[POOL_HALO:, :]
        cnt = jnp.minimum(pos + 1, w).astype(F32)
        dlt = acc / cnt - hg
        ys.append(jnp.dot(dlt.astype(BF16), pw_ref[g].astype(BF16), preferred_element_type=F32))
    y = jnp.concatenate(ys, axis=1) * ps_ref[...]
    xn = x + g1[0] * y
    xo_ref[0] = xn
    hp, lg = _ffn_prenorm(xn, gf[...], sc2[0], sh2[0], rwt_ref[...], rb_ref[...])
    hp_ref[...] = hp
    _route_tile(lg, (pl.program_id(0) == 0) & (s == 0), eid_ref, gate_ref, pos_ref, cnt_ref, carry)


def _row_layer_specs(layer, batch, ns, ts, d, e):
    row = lambda j: pl.BlockSpec((1, 1, d), _mod_spec(layer, batch, j))
    vec = pl.BlockSpec((1, d), lambda b, s: (0, 0))
    x_spec = pl.BlockSpec((1, ts, d), lambda b, s: (b, s, 0))
    choice = pl.BlockSpec((TOP_K, ts), lambda b, s: (0, b * ns + s))
    out_specs = [
        x_spec,
        pl.BlockSpec((ts, d // 2), lambda b, s: (b * ns + s, 0)),
        choice,
        pl.BlockSpec((ts, GATE_COLS), lambda b, s: (b * ns + s, 0)),
        choice,
        pl.BlockSpec((e, LANES), lambda b, s: (0, 0)),
    ]
    return row, vec, x_spec, out_specs


def _row_layer_outs(batch, seq, d, e):
    t = batch * seq
    return [
        jax.ShapeDtypeStruct((batch, seq, d), F32),
        jax.ShapeDtypeStruct((t, d // 2), U32),
        jax.ShapeDtypeStruct((TOP_K, t), I32),
        jax.ShapeDtypeStruct((t, GATE_COLS), F32),
        jax.ShapeDtypeStruct((TOP_K, t), I32),
        jax.ShapeDtypeStruct((e, LANES), F32),
    ]


def _pool_call(layer, x, pending, modr, gm, gf, pw, ps, rwt, rb):
    batch, seq, d = x.shape
    e = rwt.shape[0]
    ts = SEQ_TILE
    ns = seq // ts
    g = len(POOL_WINDOWS)
    pg = d // g
    row, vec, _, out_specs = _row_layer_specs(layer, batch, ns, ts, d, e)
    return pl.pallas_call(
        functools.partial(_pool_kernel, bool(pending)),
        grid=(batch, ns),
        in_specs=_residual_specs(pending, batch, ns, ts, d) + [
            row(0), row(1), row(2), row(3), row(4), vec, vec,
            pl.BlockSpec((g, pg, pg), lambda b, s: (0, 0, 0)),
            vec,
            pl.BlockSpec((e, d), lambda b, s: (0, 0)),
            pl.BlockSpec((e, 1), lambda b, s: (0, 0)),
        ],
        out_specs=out_specs,
        out_shape=_row_layer_outs(batch, seq, d, e),
        scratch_shapes=[pltpu.VMEM((ts + POOL_HALO, d), F32), pltpu.VMEM((e, 1), F32)],
        compiler_params=_params(("arbitrary", "arbitrary")),
        name="pool_layer",
    )(*_residual_args(x, pending, modr), modr, modr, modr, modr, modr, gm, gf, pw, ps, rwt, rb)


def _aug_lane(head):
    return HEAD_DIM if head % 2 == 0 else 0


def _bias_placement(nh):
    place = np.zeros((BIAS_PARTS * nh, nh * LANES), np.float32)
    for j in range(BIAS_PARTS):
        for h in range(nh):
            place[j * nh + h, h * LANES + _aug_lane(h) + j] = 1.0
    return jnp.asarray(place, BF16)


def _qkv_kernel(pending, *refs):
    rest = _residual_rest(refs, pending)
    sh1, sc1, gm, w_ref, wf_ref, bf_ref, place_ref, seg_ref, q_ref, k_ref, v_ref, st_ref = rest[:12]
    carry = refs[-1]
    s = pl.program_id(1)
    ts = q_ref.shape[2]
    d = gm.shape[1]
    nh = bf_ref.shape[1]
    sub = QKV_SUBROWS

    @pl.when(s == 0)
    def _():
        carry[...] = jnp.zeros(carry.shape, F32)

    r = lax.broadcasted_iota(I32, (sub, sub), 0)
    c = lax.broadcasted_iota(I32, (sub, sub), 1)
    lower = jnp.where(r >= c, 1.0, 0.0).astype(BF16)
    lane = lax.broadcasted_iota(I32, (sub, LANES), 1)
    q_top = k_top = jnp.zeros((1, LANES), F32)
    b_top = jnp.full((1, nh), NEG_BIG, F32)
    for r0 in range(0, ts, sub):
        rows = slice(r0, r0 + sub)
        x = _residual_in(refs, pending, rows)
        if pending:
            rest[12][0, rows, :] = x
        h = _norm_mod(x, gm[...], sc1[0], sh1[0]).astype(BF16)
        fl = jnp.dot(h, wf_ref[...], preferred_element_type=F32)[:, :nh] + bf_ref[...]
        proj = jnp.dot(h, w_ref[...], preferred_element_type=F32)
        logf = jnp.minimum(fl, 0.0) - jnp.log(1.0 + jnp.exp(-jnp.abs(fl)))
        parts = jnp.concatenate(_split_bf16(logf, 3), axis=1)
        cs = jnp.dot(lower, parts, preferred_element_type=F32)
        cum = cs[:, :nh] + cs[:, nh:2 * nh] + cs[:, 2 * nh:] + carry[...]
        carry[...] = cum[sub - 1:sub, :]
        bias = jnp.concatenate(_split_bf16(cum * (-LOG2E), BIAS_PARTS), axis=1)
        k_aug = jnp.dot(bias, place_ref[...], preferred_element_type=F32)
        q_all = proj[:, :d] * (HEAD_DIM ** -0.5 * LOG2E)
        k_all = proj[:, d:2 * d]
        q_norm = jnp.sqrt(jnp.dot((q_all * q_all).astype(BF16), seg_ref[...], preferred_element_type=F32))
        k_norm = jnp.sqrt(jnp.dot((k_all * k_all).astype(BF16), seg_ref[...], preferred_element_type=F32))
        q_top = jnp.maximum(q_top, jnp.max(q_norm, axis=0, keepdims=True))
        k_top = jnp.maximum(k_top, jnp.max(k_norm, axis=0, keepdims=True))
        b_top = jnp.maximum(b_top, jnp.max(cum * (-LOG2E), axis=0, keepdims=True))
        for hd in range(nh):
            lo = (hd // 2) * LANES
            a0 = _aug_lane(hd)
            real = (lane < HEAD_DIM) if hd % 2 == 0 else (lane >= HEAD_DIM)
            q_aug = jnp.where((lane >= a0) & (lane < a0 + BIAS_PARTS), 1.0, 0.0)
            v_aug = jnp.where(lane == a0, 1.0, 0.0)
            q_ref[0, hd, rows, :] = jnp.where(real, q_all[:, lo:lo + LANES], q_aug).astype(BF16)
            k_ref[0, hd, rows, :] = jnp.where(real, proj[:, d + lo:d + lo + LANES],
                                              k_aug[:, hd * LANES:(hd + 1) * LANES]).astype(BF16)
            v_ref[0, hd, rows, :] = jnp.where(real, proj[:, 2 * d + lo:2 * d + lo + LANES],
                                              v_aug).astype(BF16)
    b_row = jnp.concatenate([b_top, jnp.zeros((1, LANES - nh), F32)], axis=1)
    st_ref[0, 0] = jnp.concatenate([q_top, k_top, b_row, jnp.zeros((STAT_ROWS - 3, LANES), F32)], axis=0)


def _qkv_call(layer, x, pending, modr, gm, w_in, bf):
    batch, seq, d = x.shape
    nh = bf.shape[1]
    ts = SEQ_TILE
    ns = seq // ts
    w_qkv = w_in[:, :3 * d].astype(BF16)
    w_f = jnp.pad(w_in[:, 3 * d:], ((0, 0), (0, LANES - nh))).astype(BF16)
    place = _bias_placement(nh)
    seg = jnp.asarray(np.arange(d)[:, None] // HEAD_DIM == np.arange(LANES)[None, :], BF16)
    row = lambda j: pl.BlockSpec((1, 1, d), _mod_spec(layer, batch, j))
    head_spec = pl.BlockSpec((1, nh, ts, LANES), lambda b, s: (b, 0, s, 0))
    out_specs = [head_spec] * 3 + [pl.BlockSpec((1, 1, STAT_ROWS, LANES), lambda b, s: (b, s, 0, 0))]
    out_shape = [jax.ShapeDtypeStruct((batch, nh, seq, LANES), BF16)] * 3 + [
        jax.ShapeDtypeStruct((batch, ns, STAT_ROWS, LANES), F32)]
    if pending:
        out_specs = out_specs + [pl.BlockSpec((1, ts, d), lambda b, s: (b, s, 0))]
        out_shape = out_shape + [jax.ShapeDtypeStruct((batch, seq, d), F32)]
    outs = pl.pallas_call(
        functools.partial(_qkv_kernel, bool(pending)),
        grid=(batch, ns),
        in_specs=_residual_specs(pending, batch, ns, ts, d) + [
            row(0), row(1),
            pl.BlockSpec((1, d), lambda b, s: (0, 0)),
            pl.BlockSpec(w_qkv.shape, lambda b, s: (0, 0)),
            pl.BlockSpec(w_f.shape, lambda b, s: (0, 0)),
            pl.BlockSpec((1, nh), lambda b, s: (0, 0)),
            pl.BlockSpec(place.shape, lambda b, s: (0, 0)),
            pl.BlockSpec(seg.shape, lambda b, s: (0, 0)),
        ],
        out_specs=out_specs,
        out_shape=out_shape,
        scratch_shapes=[pltpu.VMEM((1, nh), F32)],
        compiler_params=_params(("arbitrary", "arbitrary")),
        name="fox_qkv",
    )(*_residual_args(x, pending, modr), modr, modr, gm, w_qkv, w_f, bf, place, seg)
    return outs[0], outs[1], outs[2], outs[3], (outs[4] if pending else x)


def _attn_kernel(q_ref, k_ref, v_ref, st_ref, o_ref, m_s, acc_s, level_s):
    hp = pl.program_id(1)
    qi = pl.program_id(2)
    tk = ATTN_TILE
    tq = q_ref.shape[2]
    subs = tq // tk
    m_s[...] = jnp.full(m_s.shape, NEG_BIG, F32)
    acc_s[...] = jnp.zeros(acc_s.shape, F32)
    row = lax.broadcasted_iota(I32, (tk, tk), 0)
    col = lax.broadcasted_iota(I32, (tk, tk), 1)

    def step(kb, lo, hi, diagonal):
        start = pl.multiple_of(kb * tk, tk)
        rows = slice(lo * tk, hi * tk)
        for hh in range(HEADS_PER_STEP):
            kk = k_ref[0, hh, pl.ds(start, tk), :]
            vv = v_ref[0, hh, pl.ds(start, tk), :]
            z = lax.dot_general(q_ref[0, hh, rows, :], kk, NT_DIMS, preferred_element_type=F32)
            if diagonal:
                masked = jnp.where(row >= col, z[:tk], NEG_BIG)
                z = masked if hi - lo == 1 else jnp.concatenate([masked, z[tk:]], axis=0)
            m_prev = m_s[hh, rows, :]
            m_new = jnp.maximum(m_prev, jnp.max(z, axis=1, keepdims=True))
            p = jnp.exp2(z - jnp.concatenate([m_new] * (tk // LANES), axis=1))
            acc_s[hh, rows, :] = (jnp.exp2(m_prev - m_new) * acc_s[hh, rows, :]
                                  + jnp.dot(p.astype(BF16), vv, preferred_element_type=F32))
            m_s[hh, rows, :] = m_new

    for j in range(subs):
        step(subs * qi + j, j, subs, True)

    @pl.when(qi > 0)
    def _():
        st = st_ref[0]
        nt = st.shape[0]
        stat = lax.broadcasted_iota(I32, st.shape, 1)
        q_top = jnp.max(jnp.where(stat == 0, st, 0.0), axis=1)
        k_top = jnp.max(jnp.where(stat == 1, st, 0.0), axis=1)
        b_top = jnp.max(jnp.where(stat == 2, st, NEG_BIG), axis=1)
        tile = lax.broadcasted_iota(I32, (nt, LANES), 0)
        head = lax.broadcasted_iota(I32, (nt, LANES), 1)
        level = jnp.zeros((nt, LANES), F32)
        for hh in range(HEADS_PER_STEP):
            earlier = (head == HEADS_PER_STEP * hp + hh) & (tile < subs * qi)
            for j in range(subs):
                q_here = jnp.max(jnp.where(tile == subs * qi + j, q_top, 0.0), axis=0, keepdims=True)
                floor = jnp.min(m_s[hh, j * tk:(j + 1) * tk, :], axis=0, keepdims=True) - UNDERFLOW_LOG2
                needed = (NORM_SLACK * q_here * k_top + b_top + 1.0 >= floor) & earlier
                level = jnp.maximum(level, jnp.where(needed, float(j + 1), 0.0))
        level = jnp.max(level, axis=1, keepdims=True).astype(I32)
        for t in range(nt):
            level_s[t] = level[t, 0]

        spans = sorted({1, max(subs // 2, 1), subs})

        def body(kb, carry):
            need = level_s[kb]
            below = 0
            for hi in spans:
                @pl.when((need > below) & (need <= hi))
                def _():
                    step(kb, 0, hi, False)
                below = hi
            return carry

        lax.fori_loop(0, subs * qi, body, 0)

    lane = lax.broadcasted_iota(I32, (tq, LANES), 1)
    outs = []
    for hh in range(HEADS_PER_STEP):
        a = acc_s[hh]
        outs.append(a / a[:, _aug_lane(hh):_aug_lane(hh) + 1])
    o_ref[0] = jnp.where(lane < HEAD_DIM, outs[0], outs[1]).astype(BF16)


def _attn_call(q, k, v, stats):
    batch, nh, seq, _ = q.shape
    tq = ATTN_QUERY_SUBTILES * ATTN_TILE
    kv_spec = pl.BlockSpec((1, HEADS_PER_STEP, seq, LANES), lambda b, h, i: (b, h, 0, 0))
    return pl.pallas_call(
        _attn_kernel,
        grid=(batch, nh // HEADS_PER_STEP, seq // tq),
        in_specs=[pl.BlockSpec((1, HEADS_PER_STEP, tq, LANES), lambda b, h, i: (b, h, i, 0)),
                  kv_spec, kv_spec,
                  pl.BlockSpec((1,) + stats.shape[1:], lambda b, h, i: (b, 0, 0, 0))],
        out_specs=pl.BlockSpec((1, tq, LANES), lambda b, h, i: (b, i, h)),
        out_shape=jax.ShapeDtypeStruct((batch, seq, nh * HEAD_DIM), BF16),
        scratch_shapes=[pltpu.VMEM((HEADS_PER_STEP, tq, LANES), F32)] * 2 + [pltpu.SMEM((stats.shape[1],), I32)],
        compiler_params=_params(("arbitrary", "arbitrary", "arbitrary")),
        name="fox_attention",
    )(q, k, v, stats)


def _wo_kernel(x_ref, o_ref, g1, sh2, sc2, gf, wo_ref, rwt_ref, rb_ref,
               xo_ref, hp_ref, eid_ref, gate_ref, pos_ref, cnt_ref, carry):
    m = jnp.dot(o_ref[0], wo_ref[...], preferred_element_type=F32)
    xn = x_ref[0] + g1[0] * m
    xo_ref[0] = xn
    hp, lg = _ffn_prenorm(xn, gf[...], sc2[0], sh2[0], rwt_ref[...], rb_ref[...])
    hp_ref[...] = hp
    first = (pl.program_id(0) == 0) & (pl.program_id(1) == 0)
    _route_tile(lg, first, eid_ref, gate_ref, pos_ref, cnt_ref, carry)


def _wo_call(layer, x, o, modr, gf, wo, rwt, rb):
    batch, seq, d = x.shape
    e = rwt.shape[0]
    ts = SEQ_TILE
    ns = seq // ts
    row, vec, x_spec, out_specs = _row_layer_specs(layer, batch, ns, ts, d, e)
    return pl.pallas_call(
        _wo_kernel,
        grid=(batch, ns),
        in_specs=[
            x_spec, x_spec, row(2), row(3), row(4), vec,
            pl.BlockSpec((d, d), lambda b, s: (0, 0)),
            pl.BlockSpec((e, d), lambda b, s: (0, 0)),
            pl.BlockSpec((e, 1), lambda b, s: (0, 0)),
        ],
        out_specs=out_specs,
        out_shape=_row_layer_outs(batch, seq, d, e),
        scratch_shapes=[pltpu.VMEM((e, 1), F32)],
        compiler_params=_params(("arbitrary", "arbitrary")),
        name="fox_out",
    )(x, o, modr, modr, modr, gf, wo, rwt, rb)


def _route_tile(l, first, eid_ref, gate_ref, pos_ref, cnt_ref, carry):
    @pl.when(first)
    def _():
        carry[...] = jnp.zeros(carry.shape, F32)

    e, ts = l.shape
    eidx = lax.broadcasted_iota(I32, (e, ts), 0)
    work = l
    top_v, top_i, hot = [], [], []
    for _ in range(TOP_K):
        m = jnp.max(work, axis=0, keepdims=True)
        sel = jnp.min(jnp.where(work == m, eidx, e), axis=0, keepdims=True)
        o = eidx == sel
        top_v.append(m)
        top_i.append(sel)
        hot.append(o)
        work = jnp.where(o, -jnp.inf, work)
    ex = [jnp.exp(v - top_v[0]) for v in top_v]
    den = ex[0] + ex[1] + ex[2] + ex[3]
    chosen = jnp.where(hot[0] | hot[1] | hot[2] | hot[3], 1.0, 0.0)
    r = lax.broadcasted_iota(I32, (ts, ts), 0)
    c = lax.broadcasted_iota(I32, (ts, ts), 1)
    before = jnp.where(r < c, 1.0, 0.0).astype(BF16)
    rank = jnp.dot(chosen.astype(BF16), before, preferred_element_type=F32) + carry[...]
    pos = [jnp.sum(jnp.where(o, rank, 0.0), axis=0, keepdims=True) for o in hot]
    carry[...] = carry[...] + jnp.sum(chosen, axis=1, keepdims=True)
    eid_ref[...] = jnp.concatenate(top_i, axis=0)
    gates = [x / den for x in ex] + [jnp.zeros((GATE_COLS - TOP_K, ts), F32)]
    gate_ref[...] = jnp.concatenate(gates, axis=0).T
    pos_ref[...] = jnp.concatenate(pos, axis=0).astype(I32)
    cnt_ref[...] = jnp.broadcast_to(carry[...], cnt_ref.shape)


def _dest_kernel(pstart_ref, eid_ref, pos_ref, dest_ref):
    eid = eid_ref[...]
    dest = pos_ref[...]
    for e in range(pstart_ref.shape[0]):
        dest = dest + jnp.where(eid == e, pstart_ref[e], 0)
    dest_ref[...] = dest


def _dest_call(pstart, eid, pos):
    k, t = eid.shape
    tt = min(DEST_TILE, t)
    spec = pl.BlockSpec((k, tt), lambda i, ps: (0, i))
    grid_spec = pltpu.PrefetchScalarGridSpec(
        num_scalar_prefetch=1, grid=(t // tt,), in_specs=[spec, spec], out_specs=spec)
    return pl.pallas_call(
        _dest_kernel,
        grid_spec=grid_spec,
        out_shape=jax.ShapeDtypeStruct((k, t), I32),
        compiler_params=_params(("arbitrary",)),
        name="route_dest",
    )(pstart, eid, pos)


def _sc_worker_chunks(total_chunks):
    worker = lax.axis_index("s") * SC_CORES + lax.axis_index("c")
    per_worker = total_chunks // SC_WORKERS
    return worker * per_worker, per_worker


def _sc_dispatch_call(hp, dest3, rows):
    t, w = hp.shape
    nchunk = dest3.shape[0]
    mesh = plsc.VectorSubcoreMesh(core_axis_name="c", subcore_axis_name="s")

    assert nchunk % (SC_BUFFERS * SC_WORKERS) == 0

    @functools.partial(
        pl.kernel, mesh=mesh, out_type=jax.ShapeDtypeStruct((rows, w), U32),
        scratch_types=[pltpu.VMEM((SC_BUFFERS, TOP_K, SC_CHUNK), I32),
                       pltpu.VMEM((SC_BUFFERS, SC_CHUNK, w), U32),
                       pltpu.SemaphoreType.DMA((SC_BUFFERS,)), pltpu.SemaphoreType.DMA((SC_BUFFERS,))],
        name="moe_dispatch")
    def dispatch(hp_hbm, dest_hbm, xs_hbm, idx_v, rows_v, load_sem, scatter_sem):
        first, count = _sc_worker_chunks(nchunk)

        def load(c, slot):
            return pltpu.make_async_copy(hp_hbm.at[pl.ds(c * SC_CHUNK, SC_CHUNK)], rows_v.at[slot],
                                         load_sem.at[slot])

        def scatter(slot, k):
            return pltpu.make_async_copy(rows_v.at[slot], xs_hbm.at[idx_v.at[slot, k]],
                                         scatter_sem.at[slot])

        def start_load(c, slot):
            pltpu.sync_copy(dest_hbm.at[c], idx_v.at[slot])
            load(c, slot).start()

        start_load(first, 0)

        @pl.loop(0, count, step=SC_BUFFERS)
        def _(j):
            for b in range(SC_BUFFERS):
                c = first + j + b
                nxt = (b + 1) % SC_BUFFERS

                @pl.when(j + b + 1 < count)
                def _():
                    @pl.when(j + b >= 1)
                    def _():
                        for k in range(TOP_K):
                            scatter(nxt, k).wait()
                    start_load(c + 1, nxt)

                load(c, b).wait()
                for k in range(TOP_K):
                    scatter(b, k).start()

        for slot in range(SC_BUFFERS):
            for k in range(TOP_K):
                scatter(slot, k).wait()

    return dispatch(hp, dest3)


def _sc_gather_call(ys, idx):
    n = idx.shape[0]
    _, w = ys.shape
    nchunk = n // SC_CHUNK
    mesh = plsc.VectorSubcoreMesh(core_axis_name="c", subcore_axis_name="s")

    assert nchunk % (SC_BUFFERS * SC_WORKERS) == 0

    @functools.partial(
        pl.kernel, mesh=mesh, out_type=jax.ShapeDtypeStruct((n, w), U32),
        scratch_types=[pltpu.VMEM((SC_BUFFERS, SC_CHUNK), I32), pltpu.VMEM((SC_BUFFERS, SC_CHUNK, w), U32),
                       pltpu.SemaphoreType.DMA((SC_BUFFERS,)), pltpu.SemaphoreType.DMA((SC_BUFFERS,))],
        name="moe_gather")
    def gather(ys_hbm, idx_hbm, out_hbm, idx_v, rows_v, gather_sem, write_sem):
        first, count = _sc_worker_chunks(nchunk)

        def fetch(slot):
            return pltpu.make_async_copy(ys_hbm.at[idx_v.at[slot]], rows_v.at[slot], gather_sem.at[slot])

        def write(c, slot):
            return pltpu.make_async_copy(rows_v.at[slot], out_hbm.at[pl.ds(c * SC_CHUNK, SC_CHUNK)],
                                         write_sem.at[slot])

        def start_fetch(c, slot):
            pltpu.sync_copy(idx_hbm.at[pl.ds(c * SC_CHUNK, SC_CHUNK)], idx_v.at[slot])
            fetch(slot).start()

        start_fetch(first, 0)

        @pl.loop(0, count, step=SC_BUFFERS)
        def _(j):
            for b in range(SC_BUFFERS):
                c = first + j + b
                nxt = (b + 1) % SC_BUFFERS

                @pl.when(j + b + 1 < count)
                def _():
                    @pl.when(j + b >= 1)
                    def _():
                        write(c - 1, nxt).wait()
                    start_fetch(c + 1, nxt)

                fetch(b).wait()
                write(c, b).start()

        for b in range(SC_BUFFERS):
            write(first + count - SC_BUFFERS + b, b).wait()

    return gather(ys, idx)


def _expert_kernel(be_ref, nv_ref, xs_ref, w1_ref, b1_ref, w2_ref, b2_ref, ys_ref, w1s, w2s):
    b = pl.program_id(0)
    e = be_ref[b]
    prev = be_ref[jnp.maximum(b - 1, 0)]
    nvalid = nv_ref[b]
    f = w2_ref.shape[1]

    @pl.when((b == 0) | (e != prev))
    def _():
        w1s[...] = w1_ref[0].astype(BF16)
        w2s[...] = w2_ref[0].astype(BF16)

    def ffn(r0, n):
        rows = r0 + lax.broadcasted_iota(I32, (n, 1), 0)
        xw = jnp.where(rows < nvalid, xs_ref[r0:r0 + n, :], jnp.uint32(0))
        x = _unpack_pairs(xw).astype(BF16)
        gu = jnp.dot(x, w1s[...], preferred_element_type=F32) + b1_ref[0]
        gate = jnp.minimum(gu[:, :f], SWIGLU_LIMIT)
        up = jnp.clip(gu[:, f:], -SWIGLU_LIMIT, SWIGLU_LIMIT)
        act = (up + 1.0) * (gate * _sigmoid(SWIGLU_ALPHA * gate))
        y = jnp.dot(act.astype(BF16), w2s[...], preferred_element_type=F32) + b2_ref[0]
        ys_ref[r0:r0 + n, :] = _pack_pairs(y)

    def blank(r0, n):
        ys_ref[r0:r0 + n, :] = jnp.zeros((n, ys_ref.shape[1]), U32)

    sub = EXPERT_SUBROWS
    half = sub // 2
    for r0 in range(0, xs_ref.shape[0], sub):
        @pl.when(nvalid > r0 + half)
        def _():
            ffn(r0, sub)

        @pl.when((nvalid > r0) & (nvalid <= r0 + half))
        def _():
            ffn(r0, half)
            blank(r0 + half, half)

        @pl.when(nvalid <= r0)
        def _():
            blank(r0, sub)


def _expert_call(layer, block_e, nvalid, xs, w_in, b_in, w_out, b_out):
    rows, w = xs.shape
    depth, ne, d, f2 = w_in.shape
    f = f2 // 2
    bm = EXPERT_ROWS
    x_spec = pl.BlockSpec((bm, w), lambda b, be, nv: (b, 0))
    grid_spec = pltpu.PrefetchScalarGridSpec(
        num_scalar_prefetch=2,
        grid=(rows // bm,),
        in_specs=[
            x_spec,
            pl.BlockSpec((1, d, f2), lambda b, be, nv: (layer * ne + be[b], 0, 0)),
            pl.BlockSpec((1, 1, f2), lambda b, be, nv: (layer * ne + be[b], 0, 0)),
            pl.BlockSpec((1, f, d), lambda b, be, nv: (layer * ne + be[b], 0, 0)),
            pl.BlockSpec((1, 1, d), lambda b, be, nv: (layer * ne + be[b], 0, 0)),
        ],
        out_specs=x_spec,
        scratch_shapes=[pltpu.VMEM((d, f2), BF16), pltpu.VMEM((f, d), BF16)],
    )
    return pl.pallas_call(
        _expert_kernel,
        grid_spec=grid_spec,
        out_shape=jax.ShapeDtypeStruct((rows, w), U32),
        compiler_params=_params(("arbitrary",)),
        name="moe_experts",
    )(block_e, nvalid, xs,
      w_in.reshape(depth * ne, d, f2), b_in.reshape(depth * ne, 1, f2),
      w_out.reshape(depth * ne, f, d), b_out.reshape(depth * ne, 1, d))


def _final_kernel(x_ref, yg_ref, gt_ref, g2, fg, xo_ref):
    xn = _moe_residual(x_ref, yg_ref, gt_ref, g2)
    r = lax.rsqrt(jnp.mean(xn * xn, axis=-1, keepdims=True) + EPS)
    xo_ref[0] = (xn * r) * fg[...]


def _final_call(x, pending, modr, fg):
    batch, seq, d = x.shape
    ts = COMBINE_TILE
    ns = seq // ts
    return pl.pallas_call(
        _final_kernel,
        grid=(batch, ns),
        in_specs=_residual_specs(pending, batch, ns, ts, d) + [pl.BlockSpec((1, d), lambda b, s: (0, 0))],
        out_specs=pl.BlockSpec((1, ts, d), lambda b, s: (b, s, 0)),
        out_shape=jax.ShapeDtypeStruct((batch, seq, d), F32),
        compiler_params=_params(("arbitrary", "arbitrary")),
        name="moe_combine_final",
    )(*_residual_args(x, pending, modr), fg)


def _moe(layer, hp, eid, gate_t, pos, cnt, w_in, b_in, w_out, b_out):
    t, w = hp.shape
    ne = cnt.shape[0]
    bm = EXPERT_ROWS
    counts = cnt[:, 0].astype(I32)
    nblk = (counts + bm - 1) // bm
    bend = jnp.cumsum(nblk)
    bstart = bend - nblk
    pstart = (bstart * bm).astype(I32)
    n_blocks = (t * TOP_K) // bm + ne
    blocks = jnp.arange(n_blocks, dtype=I32)
    block_e = jnp.minimum(jnp.sum(blocks[:, None] >= bend[None, :], axis=1), ne - 1).astype(I32)
    mine = block_e[:, None] == jnp.arange(ne, dtype=I32)[None, :]
    left = jnp.sum(jnp.where(mine, counts[None, :] - (blocks[:, None] - bstart[None, :]) * bm, 0), axis=1)
    nvalid = jnp.where(blocks < bend[-1], jnp.clip(left, 0, bm), 0).astype(I32)
    dest = _dest_call(pstart, eid, pos)
    dest3 = dest.reshape(TOP_K, t // SC_CHUNK, SC_CHUNK).transpose(1, 0, 2)
    xs = _sc_dispatch_call(hp, dest3, n_blocks * bm)
    ys = _expert_call(layer, block_e, nvalid, xs, w_in, b_in, w_out, b_out)
    yg = _sc_gather_call(ys, dest.reshape(TOP_K * t))
    return {"yg": yg.reshape(TOP_K, t, w), "gate_t": gate_t, "layer": layer}


def kernel(x, c, norm_mix_g, norm_ffn_g, ada_w, ada_b, pool_w, pool_scale, fox_w_in, fox_b_f, fox_w_o,
           router_w, router_b, exp_w_in, exp_b_in, exp_w_out, exp_b_out, final_g):
    batch, seq, d = x.shape
    depth = ada_w.shape[0]
    mod = _ada_call(c, ada_w, ada_b)
    modr = mod.reshape(depth * batch * 6, 1, d)
    fg = final_g.reshape(1, d)
    pending = None
    for i in range(depth):
        gm = norm_mix_g[i].reshape(1, d)
        gf = norm_ffn_g[i].reshape(1, d)
        rwt = router_w[i].T
        rb = router_b[i].reshape(-1, 1)
        j = i // 2
        if i % 2 == 0:
            x, hp, *routing = _pool_call(i, x, pending, modr, gm, gf, pool_w[j],
                                         pool_scale[j].reshape(1, d), rwt, rb)
        else:
            q, k, v, stats, x = _qkv_call(i, x, pending, modr, gm, fox_w_in[j], fox_b_f[j].reshape(1, -1))
            o = _attn_call(q, k, v, stats)
            x, hp, *routing = _wo_call(i, x, o, modr, gf, fox_w_o[j].astype(BF16), rwt, rb)
        pending = _moe(i, hp, *routing, exp_w_in, exp_b_in, exp_w_out, exp_b_out)
    return _final_call(x, pending, modr, fg)
```
